```python
import jax
import jax.numpy as jnp
from jax import lax
import numpy as np

D_MODEL = 1024
BATCH = 8
SEQ = 2048
DEPTH = 4
DEC_BATCH = 2
DEC_SEQ = 8192
PAST_LEN = 128

MEM_LEN = 256
GRID_W = 64
HEAD_DIM = 64
Q_BLOCK = 128
ROPE_THETA = 10000.0
EPS = 1e-6
MLA_HEADS = 6
MLA_Q_RANK = 256
MLA_KV_RANK = 128
MLA_NOPE = 64
MLA_ROPE = 32
MLA_V = 64
MLA_QK = MLA_NOPE + MLA_ROPE
NA_HEADS = 6
NA_WIN_R = 8
NA_WIN_C = 16
DIL_WINDOWS = (128, 512, 2048)
DIL_RATES = (1, 4, 16)
DIL_GROUPS = 3
DIL_HEADS = 4
DIL_ALL = DIL_GROUPS * DIL_HEADS
X_HEADS = 4
X_HEAD_DIM = D_MODEL // X_HEADS
D_FF = 2816
CONV_W = 3
SPLIT_CQ = MLA_Q_RANK
SPLIT_CKV = SPLIT_CQ + MLA_KV_RANK
SPLIT_KR = SPLIT_CKV + MLA_ROPE
SPLIT_NA = SPLIT_KR + 3 * NA_HEADS * HEAD_DIM
D_IN = SPLIT_NA + 3 * DIL_ALL * HEAD_DIM
MIX_OUT = MLA_HEADS * MLA_V + NA_HEADS * HEAD_DIM + DIL_HEADS * HEAD_DIM

kernel_name = 'hybrid_mla_natten_dilated_encoder'


def rmsnorm(x, g):
    xf = x.astype(jnp.float32)
    y = xf * lax.rsqrt(jnp.mean(xf * xf, axis=-1, keepdims=True) + EPS)
    return (y * g.astype(jnp.float32)).astype(x.dtype)


def rope(x, pos):
    half = x.shape[-1] // 2
    inv = ROPE_THETA ** (-jnp.arange(half, dtype=jnp.float32) / half)
    ang = pos.astype(jnp.float32)[:, None] * inv[None, :]
    cos = jnp.cos(ang)[:, None, :].astype(x.dtype)
    sin = jnp.sin(ang)[:, None, :].astype(x.dtype)
    x1, x2 = x[..., :half], x[..., half:]
    return jnp.concatenate([x1 * cos - x2 * sin, x2 * cos + x1 * sin], axis=-1)


def block_dense_attention(q, k, v):
    B, S, H, dk = q.shape
    nb = S // Q_BLOCK
    scale = dk ** -0.5
    qb = q.reshape(B, nb, Q_BLOCK, H, dk).transpose(1, 0, 2, 3, 4)

    def one(qblk):
        s = jnp.einsum('bqhd,bkhd->bhqk', qblk, k).astype(jnp.float32) * scale
        p = jax.nn.softmax(s, axis=-1).astype(v.dtype)
        return jnp.einsum('bhqk,bkhd->bqhd', p, v)

    o = lax.map(one, qb)
    return o.transpose(1, 0, 2, 3, 4).reshape(B, S, H, v.shape[-1])


def mla_attention(c_q, c_kv, k_r, lp, pos):
    B, S, _ = c_q.shape
    c_q = rmsnorm(c_q, lp['mla_q_norm'])
    c_kv = rmsnorm(c_kv, lp['mla_kv_norm'])
    q = (c_q @ lp['w_uq']).reshape(B, S, MLA_HEADS, MLA_QK)
    k_nope = (c_kv @ lp['w_uk']).reshape(B, S, MLA_HEADS, MLA_NOPE)
    v = (c_kv @ lp['w_uv']).reshape(B, S, MLA_HEADS, MLA_V)
    k = jnp.concatenate([k_nope, jnp.broadcast_to(k_r[:, :, None, :], (B, S, MLA_HEADS, MLA_ROPE))], axis=-1)
    q = rmsnorm(q, lp['mla_qn'])
    k = rmsnorm(k, lp['mla_kn'])
    q = jnp.concatenate([q[..., :MLA_NOPE], rope(q[..., MLA_NOPE:], pos)], axis=-1)
    k = jnp.concatenate([k[..., :MLA_NOPE], rope(k[..., MLA_NOPE:], pos)], axis=-1)
    return block_dense_attention(q, k, v)


def neighbourhood_attention(q, k, v, rpb):
    B, S, H, d = q.shape
    rows = S // GRID_W
    kr_ = min(NA_WIN_R, rows)
    kc = NA_WIN_C
    qg = q.reshape(B, rows, GRID_W, H, d).transpose(1, 0, 2, 3, 4)
    kg = k.reshape(B, rows, GRID_W, H, d)
    vg = v.reshape(B, rows, GRID_W, H, d)
    r_idx = jnp.arange(rows)
    r_start = jnp.clip(r_idx - kr_ // 2, 0, rows - kr_)
    c_idx = np.arange(GRID_W)
    c_start = np.clip(c_idx - kc // 2, 0, GRID_W - kc)
    c_keys = c_start[:, None] + np.arange(kc)[None, :]
    c_off = c_keys - c_idx[:, None] + (NA_WIN_C - 1)
    bias_c = rpb[:, :, c_off]
    scale = d ** -0.5

    def one(args):
        q_row, r, rs = args
        k_rows = lax.dynamic_slice_in_dim(kg, rs, kr_, axis=1)
        v_rows = lax.dynamic_slice_in_dim(vg, rs, kr_, axis=1)
        k_nb = k_rows[:, :, c_keys]
        v_nb = v_rows[:, :, c_keys]
        s = jnp.einsum('bqhd,brqjhd->bhqrj', q_row, k_nb).astype(jnp.float32) * scale
        r_off = rs + jnp.arange(kr_) - r + (NA_WIN_R - 1)
        bias = jnp.take(bias_c, r_off, axis=1).transpose(0, 2, 1, 3)
        s = s + bias[None].astype(jnp.float32)
        p = jax.nn.softmax(s.reshape(B, H, GRID_W, kr_ * kc), axis=-1).reshape(s.shape).astype(v.dtype)
        return jnp.einsum('bhqrj,brqjhd->bqhd', p, v_nb)

    o = lax.map(one, (qg, r_idx, r_start))
    return o.transpose(1, 0, 2, 3, 4).reshape(B, S, H * d)


def dilated_attention(q, k, v):
    B, S, _, d = q.shape
    G, Hg = DIL_GROUPS, DIL_HEADS
    offs = jnp.asarray(np.stack([r * np.arange(-(w // (2 * r)), w // (2 * r) + 1)
                                 for w, r in zip(DIL_WINDOWS, DIL_RATES)]))
    nb = S // Q_BLOCK
    qb = q.reshape(B, nb, Q_BLOCK, G, Hg, d).transpose(1, 0, 2, 3, 4, 5)
    kG = k.reshape(B, S, G, Hg, d).transpose(2, 0, 1, 3, 4)
    vG = v.reshape(B, S, G, Hg, d).transpose(2, 0, 1, 3, 4)
    scale = d ** -0.5

    def one(args):
        qblk, b0 = args
        pos = b0 + jnp.arange(Q_BLOCK)
        idx = pos[None, :, None] + offs[:, None, :]
        valid = (idx >= 0) & (idx < S)
        idx = jnp.clip(idx, 0, S - 1)
        kn = jax.vmap(lambda t, i: jnp.take(t, i, axis=1))(kG, idx)
        vn = jax.vmap(lambda t, i: jnp.take(t, i, axis=1))(vG, idx)
        s = jnp.einsum('bqghd,gbqjhd->gbhqj', qblk, kn).astype(jnp.float32) * scale
        s = jnp.where(valid[:, None, None], s, -jnp.inf)
        lse = jax.nn.logsumexp(s, axis=-1, keepdims=True)
        p = jnp.exp(s - lse).astype(v.dtype)
        o = jnp.einsum('gbhqj,gbqjhd->gbhqd', p, vn)
        alpha = jax.nn.softmax(lse[..., 0], axis=0).astype(v.dtype)
        return jnp.einsum('gbhq,gbhqd->bqhd', alpha, o)

    o = lax.map(one, (qb, jnp.arange(nb) * Q_BLOCK))
    return o.transpose(1, 0, 2, 3, 4).reshape(B, S, Hg * d)


def token_mixer(h, lp, pos):
    B, S, _ = h.shape
    z = h @ lp['w_in']
    c_q = z[..., :SPLIT_CQ]
    c_kv = z[..., SPLIT_CQ:SPLIT_CKV]
    k_r = z[..., SPLIT_CKV:SPLIT_KR]
    na = z[..., SPLIT_KR:SPLIT_NA].reshape(B, S, 3, NA_HEADS, HEAD_DIM)
    dl = z[..., SPLIT_NA:].reshape(B, S, 3, DIL_ALL, HEAD_DIM)
    o_a = mla_attention(c_q, c_kv, k_r, lp, pos).reshape(B, S, MLA_HEADS * MLA_V)
    q_b = rmsnorm(na[:, :, 0], lp['na_qn'])
    k_b = rmsnorm(na[:, :, 1], lp['na_kn'])
    o_b = neighbourhood_attention(q_b, k_b, na[:, :, 2], lp['na_rpb'])
    q_c = rope(rmsnorm(dl[:, :, 0], lp['dil_qn']), pos)
    k_c = rope(rmsnorm(dl[:, :, 1], lp['dil_kn']), pos)
    o_c = dilated_attention(q_c, k_c, dl[:, :, 2])
    return jnp.concatenate([o_a, o_b, o_c], axis=-1) @ lp['w_o']


def memory_cross_attention(h, mem, lp):
    B, S, _ = h.shape
    M = mem.shape[1]
    q = (h @ lp['w_cq']).reshape(B, S, X_HEADS, X_HEAD_DIM)
    kv = (rmsnorm(mem, lp['norm_mem']) @ lp['w_ckv']).reshape(B, M, 2, X_HEADS, X_HEAD_DIM)
    q = rmsnorm(q, lp['x_qn'])
    k = rmsnorm(kv[:, :, 0], lp['x_kn'])
    v = kv[:, :, 1]
    s = jnp.einsum('bqhd,bkhd->bhqk', q, k).astype(jnp.float32) * (X_HEAD_DIM ** -0.5)
    p = jax.nn.softmax(s, axis=-1).astype(v.dtype)
    o = jnp.einsum('bhqk,bkhd->bqhd', p, v).reshape(B, S, D_MODEL)
    return o @ lp['w_co']


def conv_ffn(h, lp):
    S = h.shape[1]
    u = h @ lp['w_up']
    pad = CONV_W // 2
    up = jnp.pad(u, ((0, 0), (pad, pad), (0, 0)))
    w = lp['conv_w']
    u = sum(up[:, j:j + S] * w[j] for j in range(CONV_W)) + lp['conv_b']
    a, g = u[..., :D_FF], u[..., D_FF:]
    return (jax.nn.silu(g) * a) @ lp['w_down']


def encoder_trunk(x, mem, layers):
    S = x.shape[1]
    pos = jnp.arange(S)
    for l in range(DEPTH):
        lp = layers[l]
        x = x + token_mixer(rmsnorm(x, lp['norm_mix']), lp, pos)
        x = x + memory_cross_attention(rmsnorm(x, lp['norm_cross']), mem, lp)
        x = x + conv_ffn(rmsnorm(x, lp['norm_ffn']), lp)
    return x


def setup_inputs(seed: int = 0) -> dict:
    key = jax.random.key(seed)
    ks = jax.random.split(key, 32)
    f32 = jnp.float32
    L = DEPTH
    res = (3 * DEPTH) ** -0.5

    def nrm(k, shape, scale):
        return jax.random.normal(k, shape, f32) * scale

    def gain(k, n):
        return 1.0 + 0.02 * jax.random.normal(k, (L, n), f32)

    return {
        'x_prompt': nrm(ks[0], (BATCH, SEQ, D_MODEL), 1.0),
        'x_sample': nrm(ks[1], (DEC_BATCH, DEC_SEQ, D_MODEL), 1.0),
        'mem_prompt': nrm(ks[2], (BATCH, MEM_LEN, D_MODEL), 1.0),
        'mem_sample': nrm(ks[3], (DEC_BATCH, MEM_LEN, D_MODEL), 1.0),
        'norm_mix': gain(ks[4], D_MODEL),
        'w_in': nrm(ks[5], (L, D_MODEL, D_IN), D_MODEL ** -0.5),
        'mla_q_norm': gain(ks[6], MLA_Q_RANK),
        'mla_kv_norm': gain(ks[7], MLA_KV_RANK),
        'w_uq': nrm(ks[8], (L, MLA_Q_RANK, MLA_HEADS * MLA_QK), MLA_Q_RANK ** -0.5),
        'w_uk': nrm(ks[9], (L, MLA_KV_RANK, MLA_HEADS * MLA_NOPE), MLA_KV_RANK ** -0.5),
        'w_uv': nrm(ks[10], (L, MLA_KV_RANK, MLA_HEADS * MLA_V), MLA_KV_RANK ** -0.5),
        'mla_qn': gain(ks[11], MLA_QK),
        'mla_kn': gain(ks[12], MLA_QK),
        'na_qn': gain(ks[13], HEAD_DIM),
        'na_kn': gain(ks[14], HEAD_DIM),
        'na_rpb': nrm(ks[15], (L, NA_HEADS, 2 * NA_WIN_R - 1, 2 * NA_WIN_C - 1), 0.1),
        'dil_qn': gain(ks[16], HEAD_DIM),
        'dil_kn': gain(ks[17], HEAD_DIM),
        'w_o': nrm(ks[18], (L, MIX_OUT, D_MODEL), MIX_OUT ** -0.5 * res),
        'norm_cross': gain(ks[19], D_MODEL),
        'norm_mem': gain(ks[20], D_MODEL),
        'w_cq': nrm(ks[21], (L, D_MODEL, D_MODEL), D_MODEL ** -0.5),
        'w_ckv': nrm(ks[22], (L, D_MODEL, 2 * D_MODEL), D_MODEL ** -0.5),
        'x_qn': gain(ks[23], X_HEAD_DIM),
        'x_kn': gain(ks[24], X_HEAD_DIM),
        'w_co': nrm(ks[25], (L, D_MODEL, D_MODEL), D_MODEL ** -0.5 * res),
        'norm_ffn': gain(ks[26], D_MODEL),
        'w_up': nrm(ks[27], (L, D_MODEL, 2 * D_FF), D_MODEL ** -0.5),
        'conv_w': nrm(ks[28], (L, CONV_W, 2 * D_FF), CONV_W ** -0.5),
        'conv_b': nrm(ks[29], (L, 2 * D_FF), 0.01),
        'w_down': nrm(ks[30], (L, D_FF, D_MODEL), D_FF ** -0.5 * res),
    }


def reference(x_prompt, x_sample, mem_prompt, mem_sample, norm_mix, w_in, mla_q_norm, mla_kv_norm,
              w_uq, w_uk, w_uv, mla_qn, mla_kn, na_qn, na_kn, na_rpb, dil_qn, dil_kn, w_o,
              norm_cross, norm_mem, w_cq, w_ckv, x_qn, x_kn, w_co, norm_ffn, w_up, conv_w, conv_b, w_down):
    layers = []
    for l in range(DEPTH):
        layers.append({
            'norm_mix': norm_mix[l], 'w_in': w_in[l], 'mla_q_norm': mla_q_norm[l],
            'mla_kv_norm': mla_kv_norm[l], 'w_uq': w_uq[l], 'w_uk': w_uk[l], 'w_uv': w_uv[l],
            'mla_qn': mla_qn[l], 'mla_kn': mla_kn[l], 'na_qn': na_qn[l], 'na_kn': na_kn[l],
            'na_rpb': na_rpb[l], 'dil_qn': dil_qn[l], 'dil_kn': dil_kn[l], 'w_o': w_o[l],
            'norm_cross': norm_cross[l], 'norm_mem': norm_mem[l], 'w_cq': w_cq[l], 'w_ckv': w_ckv[l],
            'x_qn': x_qn[l], 'x_kn': x_kn[l], 'w_co': w_co[l], 'norm_ffn': norm_ffn[l],
            'w_up': w_up[l], 'conv_w': conv_w[l], 'conv_b': conv_b[l], 'w_down': w_down[l],
        })
    y_prompt = encoder_trunk(x_prompt, mem_prompt, layers)
    y_sample = encoder_trunk(x_sample, mem_sample, layers)
    return (y_prompt, y_sample)
```

```python
import functools

import jax
import jax.numpy as jnp
import numpy as np
from jax import lax
from jax.experimental import pallas as pl
from jax.experimental.pallas import tpu as pltpu

D_MODEL = 1024
DEPTH = 4
GRID_W = 64
HEAD_DIM = 64
ROPE_THETA = 10000.0
EPS = 1e-6
MLA_HEADS = 6
MLA_Q_RANK = 256
MLA_KV_RANK = 128
MLA_NOPE = 64
MLA_ROPE = 32
MLA_V = 64
MLA_QK = MLA_NOPE + MLA_ROPE
NA_HEADS = 6
NA_WIN_R = 8
NA_WIN_C = 16
DIL_WINDOWS = (128, 512, 2048)
DIL_RATES = (1, 4, 16)
DIL_GROUPS = 3
DIL_HEADS = 4
DIL_ALL = DIL_GROUPS * DIL_HEADS
X_HEADS = 4
X_HEAD_DIM = D_MODEL // X_HEADS
D_FF = 2816
CONV_W = 3
SPLIT_CQ = MLA_Q_RANK
SPLIT_CKV = SPLIT_CQ + MLA_KV_RANK
SPLIT_KR = SPLIT_CKV + MLA_ROPE
SPLIT_NA = SPLIT_KR + 3 * NA_HEADS * HEAD_DIM
D_IN = SPLIT_NA + 3 * DIL_ALL * HEAD_DIM

LANES = 128
NEG = -1e30
TM = 512
DIL_TQ = 128
DIL_SB = DIL_TQ * max(DIL_RATES)
NA_RB = 8
FF_CH = 256
N_FF_CH = D_FF // FF_CH
VMEM_LIMIT = 56 * 1024 * 1024

COL_MLA = 0
COL_NA = 512
COL_DIL = COL_NA + 3 * NA_HEADS * HEAD_DIM
D_IN_P = COL_DIL + 3 * DIL_ALL * HEAD_DIM

_BF = jnp.bfloat16
_F32 = jnp.float32


def _dot(a, b):
    return jnp.dot(a, b, preferred_element_type=_F32)


def _dot_t(a, b):
    return lax.dot_general(a, b, (((1,), (1,)), ((), ())), preferred_element_type=_F32)


def _rms(x, g):
    ms = jnp.mean(x * x, axis=-1, keepdims=True)
    return x * lax.rsqrt(ms + EPS) * g


def _lane(shape):
    return lax.broadcasted_iota(jnp.int32, shape, len(shape) - 1)


def _pair_norm(x, first_head):
    sq = x * x
    tot = jnp.sum(sq, axis=-1, keepdims=True)
    s0 = jnp.sum(jnp.where(first_head, sq, 0.0), axis=-1, keepdims=True)
    inv0 = lax.rsqrt(s0 * (1.0 / HEAD_DIM) + EPS)
    inv1 = lax.rsqrt((tot - s0) * (1.0 / HEAD_DIM) + EPS)
    return x * jnp.where(first_head, inv0, inv1)


def _mix_in_kernel(x_ref, gmix_ref, win_ref, gcq_ref, gckv_ref, wuq_ref, wuk_ref, wuv_ref,
                   gmq_ref, gmk_ref, gkr_ref, gnq_ref, gnk_ref, gdq_ref, gdk_ref,
                   tqc_ref, tqs1_ref, tqs2_ref, tkc_ref, tks_ref, tdc_ref, tds_ref,
                   qm_ref, km_ref, vm_ref, qn_ref, kn_ref, vn_ref, qd_ref, kd_ref, vd_ref,
                   perm_ref):
    tm = x_ref.shape[0]
    h = _rms(x_ref[...], gmix_ref[...]).astype(_BF)

    z = _dot(h, win_ref[:, COL_MLA:COL_NA])
    cq = _rms(z[:, 0:MLA_Q_RANK], gcq_ref[...]).astype(_BF)
    ckv = _rms(z[:, MLA_Q_RANK:MLA_Q_RANK + MLA_KV_RANK], gckv_ref[...]).astype(_BF)
    kr = z[:, 384:512]
    lane = _lane((tm, LANES))
    kr_lanes = (lane >= MLA_NOPE) & (lane < MLA_QK)
    ss_r = jnp.sum(jnp.where(kr_lanes, kr * kr, 0.0), axis=-1, keepdims=True)
    krg = kr * gkr_ref[...]
    kr_rot = krg * tkc_ref[...] + pltpu.roll(krg, 64, 1) * tks_ref[...]
    qpre = _dot(cq, wuq_ref[...])
    kpre = _dot(ckv, wuk_ref[...])
    vpre = _dot(ckv, wuv_ref[...])
    for hd in range(MLA_HEADS):
        qh = qpre[:, hd * LANES:(hd + 1) * LANES]
        inv_q = lax.rsqrt(jnp.sum(qh * qh, axis=-1, keepdims=True) * (1.0 / MLA_QK) + EPS)
        qg = qh * gmq_ref[...]
        qr = qg * tqc_ref[...] + pltpu.roll(qg, 16, 1) * tqs1_ref[...] + pltpu.roll(qg, 112, 1) * tqs2_ref[...]
        qm_ref[hd] = (qr * (inv_q * (MLA_QK ** -0.5))).astype(_BF)
        kh = kpre[:, hd * LANES:(hd + 1) * LANES]
        inv_k = lax.rsqrt((jnp.sum(kh * kh, axis=-1, keepdims=True) + ss_r) * (1.0 / MLA_QK) + EPS)
        km_ref[hd] = ((kh * gmk_ref[...] + kr_rot) * inv_k).astype(_BF)
    for p in range(MLA_HEADS // 2):
        vm_ref[p] = vpre[:, p * LANES:(p + 1) * LANES].astype(_BF)

    z = _dot(h, win_ref[:, COL_NA:COL_DIL])
    first = lane < HEAD_DIM
    npair = NA_HEADS // 2
    for p in range(npair):
        base = 0 if p < 2 else 768
        pp = p if p < 2 else 0
        width = 256 if p < 2 else 128
        q2 = z[:, base + pp * LANES: base + (pp + 1) * LANES]
        k2 = z[:, base + width + pp * LANES: base + width + (pp + 1) * LANES]
        v2 = z[:, base + 2 * width + pp * LANES: base + 2 * width + (pp + 1) * LANES]
        qn_ref[p] = (_pair_norm(q2, first) * (gnq_ref[...] * HEAD_DIM ** -0.5)).astype(_BF)
        kn_ref[p] = (_pair_norm(k2, first) * gnk_ref[...]).astype(_BF)
        vn_ref[p] = v2.astype(_BF)

    first_d = (lane % HEAD_DIM) < (HEAD_DIM // 2)

    def put(out_ref, p, val, rate):
        if rate == 1:
            out_ref[p] = val.astype(_BF)
        else:
            rows = tm // rate
            perm_ref[...] = val
            for rho in range(rate):
                out_ref[p, rho * rows:(rho + 1) * rows, :] = perm_ref[pl.ds(rho, rows, stride=rate), :].astype(_BF)

    for g in range(DIL_GROUPS):
        z = _dot(h, win_ref[:, COL_DIL + g * 768: COL_DIL + (g + 1) * 768])
        for pp in range(2):
            p = 2 * g + pp
            q2 = z[:, pp * LANES:(pp + 1) * LANES]
            k2 = z[:, 256 + pp * LANES: 256 + (pp + 1) * LANES]
            v2 = z[:, 512 + pp * LANES: 512 + (pp + 1) * LANES]
            qx = _pair_norm(q2, first_d) * (gdq_ref[...] * HEAD_DIM ** -0.5)
            qx = qx * tdc_ref[...] + pltpu.roll(qx, 64, 1) * tds_ref[...]
            kx = _pair_norm(k2, first_d) * gdk_ref[...]
            kx = kx * tdc_ref[...] + pltpu.roll(kx, 64, 1) * tds_ref[...]
            put(qd_ref, p, qx, DIL_RATES[g])
            put(kd_ref, p, kx, DIL_RATES[g])
            put(vd_ref, p, v2, DIL_RATES[g])


def _const_spec(shape):
    nd = len(shape)
    return pl.BlockSpec(shape, lambda i, _nd=nd: (0,) * _nd)


def _mix_in(x, lw, tabs, seq):
    t = x.shape[0]
    nblk_seq = seq // TM
    tab_spec = pl.BlockSpec((TM, LANES), lambda i: (i % nblk_seq, 0))
    weights = [lw['g_mix'], lw['w_in'], lw['g_cq'], lw['g_ckv'], lw['w_uq'], lw['w_uk'], lw['w_uv'],
               lw['g_mq'], lw['g_mk'], lw['g_kr'], lw['g_nq'], lw['g_nk'], lw['g_dq'], lw['g_dk']]
    tables = [tabs['q_c'], tabs['q_s1'], tabs['q_s2'], tabs['k_c'], tabs['k_s'], tabs['d_c'], tabs['d_s']]

    def out(n):
        return (jax.ShapeDtypeStruct((n, t, LANES), _BF), pl.BlockSpec((n, TM, LANES), lambda i: (0, i, 0)))

    outs = [out(6), out(6), out(3), out(3), out(3), out(3), out(6), out(6), out(6)]
    return pl.pallas_call(
        _mix_in_kernel,
        grid=(t // TM,),
        in_specs=[pl.BlockSpec((TM, D_MODEL), lambda i: (i, 0))]
        + [_const_spec(w.shape) for w in weights] + [tab_spec] * len(tables),
        out_specs=[o[1] for o in outs],
        out_shape=[o[0] for o in outs],
        scratch_shapes=[pltpu.VMEM((TM, LANES), _F32)],
        compiler_params=pltpu.CompilerParams(dimension_semantics=("parallel",), vmem_limit_bytes=VMEM_LIMIT),
        name="mix_in",
    )(x, *weights, *tables)


def _mla_kernel(q_ref, k_ref, v_ref, o_ref, m_sc, l_sc, acc_sc):
    j = pl.program_id(3)

    @pl.when(j == 0)
    def _():
        m_sc[...] = jnp.full(m_sc.shape, NEG, _F32)
        l_sc[...] = jnp.zeros(l_sc.shape, _F32)
        acc_sc[...] = jnp.zeros(acc_sc.shape, _F32)

    v = v_ref[0]
    for hh in range(2):
        s = _dot_t(q_ref[hh], k_ref[hh])
        m_prev = m_sc[hh]
        m_new = jnp.maximum(m_prev, jnp.max(s, axis=1, keepdims=True))
        alpha = jnp.exp(m_prev - m_new)
        p = jnp.exp(s - m_new)
        l_sc[hh] = alpha * l_sc[hh] + jnp.sum(p, axis=1, keepdims=True)
        acc_sc[hh] = alpha * acc_sc[hh] + _dot(p.astype(_BF), v)
        m_sc[hh] = m_new

    @pl.when(j == pl.num_programs(3) - 1)
    def _():
        lane = _lane(acc_sc.shape[1:])
        o = jnp.where(lane < MLA_V, acc_sc[0] / l_sc[0], acc_sc[1] / l_sc[1])
        o_ref[0] = o.astype(o_ref.dtype)


def _mla(q, k, v, batch, seq, tq=512, tk=512):
    t = q.shape[1]
    nq, nk = seq // tq, seq // tk
    return pl.pallas_call(
        _mla_kernel,
        grid=(batch, MLA_HEADS // 2, nq, nk),
        in_specs=[pl.BlockSpec((2, tq, LANES), lambda b, p, i, j: (p, b * nq + i, 0)),
                  pl.BlockSpec((2, tk, LANES), lambda b, p, i, j: (p, b * nk + j, 0)),
                  pl.BlockSpec((1, tk, LANES), lambda b, p, i, j: (p, b * nk + j, 0))],
        out_specs=pl.BlockSpec((1, tq, LANES), lambda b, p, i, j: (p, b * nq + i, 0)),
        out_shape=jax.ShapeDtypeStruct((MLA_HEADS // 2, t, LANES), _BF),
        scratch_shapes=[pltpu.VMEM((2, tq, 1), _F32), pltpu.VMEM((2, tq, 1), _F32),
                        pltpu.VMEM((2, tq, LANES), _F32)],
        compiler_params=pltpu.CompilerParams(
            dimension_semantics=("parallel", "parallel", "parallel", "arbitrary"), vmem_limit_bytes=VMEM_LIMIT),
        name="mla_attention",
    )(q, k, v)


def _na_kernel(q_ref, k_ref, v_ref, bias_ref, o_ref, *, rows):
    i = pl.program_id(2)
    lane = _lane((GRID_W, LANES))
    first = lane < HEAD_DIM
    nkeys = NA_WIN_R * GRID_W

    def body(rr, carry):
        r = i * NA_RB + rr
        rs = jnp.clip(r - NA_WIN_R // 2, 0, rows - NA_WIN_R)
        delta = r - rs
        q = q_ref[0, pl.ds(pl.multiple_of(rr * GRID_W, GRID_W), GRID_W), :]
        kw = k_ref[0, pl.ds(pl.multiple_of(rs * GRID_W, GRID_W), nkeys), :]
        vw = v_ref[0, pl.ds(pl.multiple_of(rs * GRID_W, GRID_W), nkeys), :]
        outs = []
        for hh in range(2):
            qm = jnp.where(first if hh == 0 else jnp.logical_not(first), q, jnp.zeros_like(q))
            s = _dot_t(qm, kw) + bias_ref[0, delta, hh]
            m = jnp.max(s, axis=1, keepdims=True)
            p = jnp.exp(s - m)
            l = jnp.sum(p, axis=1, keepdims=True)
            outs.append(_dot(p.astype(_BF), vw) / l)
        o = jnp.where(first, outs[0], outs[1])
        o_ref[0, pl.ds(pl.multiple_of(rr * GRID_W, GRID_W), GRID_W), :] = o.astype(o_ref.dtype)
        return carry

    lax.fori_loop(0, NA_RB, body, 0)


def _na(q, k, v, bias, batch, seq):
    t = q.shape[1]
    rows = seq // GRID_W
    assert rows >= NA_WIN_R and rows % NA_RB == 0
    nrb = rows // NA_RB
    qrows = NA_RB * GRID_W
    return pl.pallas_call(
        functools.partial(_na_kernel, rows=rows),
        grid=(batch, NA_HEADS // 2, nrb),
        in_specs=[pl.BlockSpec((1, qrows, LANES), lambda b, p, i: (p, b * nrb + i, 0)),
                  pl.BlockSpec((1, seq, LANES), lambda b, p, i: (p, b, 0)),
                  pl.BlockSpec((1, seq, LANES), lambda b, p, i: (p, b, 0)),
                  pl.BlockSpec((1, NA_WIN_R, 2, GRID_W, NA_WIN_R * GRID_W), lambda b, p, i: (p, 0, 0, 0, 0))],
        out_specs=pl.BlockSpec((1, qrows, LANES), lambda b, p, i: (p, b * nrb + i, 0)),
        out_shape=jax.ShapeDtypeStruct((NA_HEADS // 2, t, LANES), _BF),
        compiler_params=pltpu.CompilerParams(
            dimension_semantics=("parallel", "parallel", "arbitrary"), vmem_limit_bytes=VMEM_LIMIT),
        name="na_attention",
    )(q, k, v, bias)


def _dil_kernel(q0_ref, q1_ref, q2_ref, k0_ref, k1_ref, k2_ref, v0_ref, v1_ref, v2_ref, band_ref,
                o_ref, og_sc, lse_sc, *, seq):
    sb = pl.program_id(2)
    q_refs = (q0_ref, q1_ref, q2_ref)
    k_refs = (k0_ref, k1_ref, k2_ref)
    v_refs = (v0_ref, v1_ref, v2_ref)
    lane = _lane((DIL_TQ, LANES))
    first_qk = (lane % HEAD_DIM) < (HEAD_DIM // 2)
    first_v = lane < HEAD_DIM
    n_tiles = DIL_SB // DIL_TQ
    half = DIL_TQ // 2

    for g in range(DIL_GROUPS):
        rate = DIL_RATES[g]
        length = seq // rate
        per_blk = TM // rate
        chunk = min(half, per_blk)
        n_chunks = (2 * DIL_TQ) // chunk
        tiles_per_res = n_tiles // rate
        q_ref, k_ref, v_ref = q_refs[g], k_refs[g], v_refs[g]

        def row_of(m, rho, _per_blk=per_blk):
            return (m // _per_blk) * TM + rho * _per_blk + (m % _per_blk)

        def body(tt, carry, _g=g, _rate=rate, _length=length, _per_blk=per_blk, _chunk=chunk,
                 _n_chunks=n_chunks, _tpr=tiles_per_res, _q=q_ref, _k=k_ref, _v=v_ref):
            rho = tt // _tpr
            mt = tt % _tpr
            m0 = sb * (DIL_SB // _rate) + mt * DIL_TQ
            q = jnp.concatenate(
                [_q[0, pl.ds(pl.multiple_of(row_of(m0 + c * _chunk, rho), _chunk), _chunk), :]
                 for c in range(DIL_TQ // _chunk)], axis=0)
            ks, vs = [], []
            for c in range(_n_chunks):
                ms = jnp.clip(m0 - half + c * _chunk, 0, _length - _chunk)
                row = pl.multiple_of(row_of(ms, rho), _chunk)
                ks.append(_k[0, pl.ds(row, _chunk), :])
                vs.append(_v[0, pl.ds(row, _chunk), :])
            kw = jnp.concatenate(ks, axis=0)
            vw = jnp.concatenate(vs, axis=0)
            edge = (m0 == 0).astype(jnp.int32) + 2 * (m0 == _length - DIL_TQ).astype(jnp.int32)
            bias = band_ref[edge]
            o_pair, lse_pair = None, None
            for hh in range(2):
                qm = jnp.where(first_qk if hh == 0 else jnp.logical_not(first_qk), q, jnp.zeros_like(q))
                s = _dot_t(qm, kw) + bias
                m = jnp.max(s, axis=1, keepdims=True)
                p = jnp.exp(s - m)
                l = jnp.sum(p, axis=1, keepdims=True)
                o = _dot(p.astype(_BF), vw) / l
                lse = jnp.broadcast_to(m + jnp.log(l), (DIL_TQ, LANES))
                if hh == 0:
                    o_pair, lse_pair = o, lse
                else:
                    o_pair = jnp.where(first_v, o_pair, o)
                    lse_pair = jnp.where(first_v, lse_pair, lse)
            base = rho + _rate * mt * DIL_TQ
            if _rate == 1:
                dst = pl.ds(pl.multiple_of(base, DIL_TQ), DIL_TQ)
            else:
                dst = pl.ds(base, DIL_TQ, stride=_rate)
            og_sc[_g, dst, :] = o_pair
            lse_sc[_g, dst, :] = lse_pair
            return carry

        lax.fori_loop(0, n_tiles, body, 0)

    m = jnp.maximum(jnp.maximum(lse_sc[0], lse_sc[1]), lse_sc[2])
    w0 = jnp.exp(lse_sc[0] - m)
    w1 = jnp.exp(lse_sc[1] - m)
    w2 = jnp.exp(lse_sc[2] - m)
    o = (w0 * og_sc[0] + w1 * og_sc[1] + w2 * og_sc[2]) / (w0 + w1 + w2)
    o_ref[0] = o.astype(o_ref.dtype)


def _dil(q, k, v, band, batch, seq):
    t = q.shape[1]
    assert seq % DIL_SB == 0
    nsb = seq // DIL_SB

    def q_spec(g):
        return pl.BlockSpec((1, DIL_SB, LANES), lambda b, sp, s, _g=g: (2 * _g + sp, b * nsb + s, 0))

    def kv_spec(g):
        return pl.BlockSpec((1, seq, LANES), lambda b, sp, s, _g=g: (2 * _g + sp, b, 0))

    return pl.pallas_call(
        functools.partial(_dil_kernel, seq=seq),
        grid=(batch, 2, nsb),
        in_specs=[kv_spec(0), kv_spec(1), kv_spec(2), kv_spec(0), kv_spec(1), kv_spec(2),
                  kv_spec(0), kv_spec(1), kv_spec(2),
                  pl.BlockSpec((4, DIL_TQ, 2 * DIL_TQ), lambda b, sp, s: (0, 0, 0))],
        out_specs=pl.BlockSpec((1, DIL_SB, LANES), lambda b, sp, s: (sp, b * nsb + s, 0)),
        out_shape=jax.ShapeDtypeStruct((2, t, LANES), _BF),
        scratch_shapes=[pltpu.VMEM((DIL_GROUPS, DIL_SB, LANES), _F32),
                        pltpu.VMEM((DIL_GROUPS, DIL_SB, LANES), _F32)],
        compiler_params=pltpu.CompilerParams(
            dimension_semantics=("parallel", "parallel", "arbitrary"), vmem_limit_bytes=VMEM_LIMIT),
        name="dil_attention",
    )(q, q, q, k, k, k, v, v, v, band)


def _mem_kv_kernel(mem_ref, gmem_ref, wckv_ref, gxk_ref, k_ref, v_ref):
    hm = _rms(mem_ref[0], gmem_ref[...]).astype(_BF)
    kv = _dot(hm, wckv_ref[...])
    for hd in range(X_HEADS):
        kh = kv[:, hd * X_HEAD_DIM:(hd + 1) * X_HEAD_DIM]
        k_ref[0, :, hd * X_HEAD_DIM:(hd + 1) * X_HEAD_DIM] = _rms(kh, gxk_ref[...]).astype(_BF)
    v_ref[0] = kv[:, D_MODEL:].astype(_BF)


def _mem_kv(mem, lw):
    b, m, _ = mem.shape
    return pl.pallas_call(
        _mem_kv_kernel,
        grid=(b,),
        in_specs=[pl.BlockSpec((1, m, D_MODEL), lambda i: (i, 0, 0)),
                  _const_spec(lw['g_mem'].shape), _const_spec(lw['w_ckv'].shape), _const_spec(lw['g_xk'].shape)],
        out_specs=[pl.BlockSpec((1, m, D_MODEL), lambda i: (i, 0, 0))] * 2,
        out_shape=[jax.ShapeDtypeStruct((b, m, D_MODEL), _BF)] * 2,
        compiler_params=pltpu.CompilerParams(dimension_semantics=("parallel",), vmem_limit_bytes=VMEM_LIMIT),
        name="mem_kv",
    )(mem, lw['g_mem'], lw['w_ckv'], lw['g_xk'])


def _mix_out_kernel(x_ref, oa_ref, ob_ref, oc_ref, wo_ref, gcross_ref, wcq_ref, gxq_ref,
                    mk_ref, mv_ref, wco_ref, y_ref):
    mix = jnp.concatenate([oa_ref[p] for p in range(3)] + [ob_ref[p] for p in range(3)]
                          + [oc_ref[p] for p in range(2)], axis=1)
    x1 = x_ref[...] + _dot(mix, wo_ref[...])
    hc = _rms(x1, gcross_ref[...]).astype(_BF)
    q = _dot(hc, wcq_ref[...])
    outs = []
    for hd in range(X_HEADS):
        sl = slice(hd * X_HEAD_DIM, (hd + 1) * X_HEAD_DIM)
        qh = (_rms(q[:, sl], gxq_ref[...]) * X_HEAD_DIM ** -0.5).astype(_BF)
        s = _dot_t(qh, mk_ref[0, :, sl])
        m = jnp.max(s, axis=1, keepdims=True)
        p = jnp.exp(s - m)
        l = jnp.sum(p, axis=1, keepdims=True)
        outs.append((_dot(p.astype(_BF), mv_ref[0, :, sl]) / l).astype(_BF))
    o = jnp.concatenate(outs, axis=1)
    y_ref[...] = x1 + _dot(o, wco_ref[...])


def _mix_out(x, oa, ob, oc, mk, mv, lw, seq):
    t = x.shape[0]
    nblk_seq = seq // TM
    mem_len = mk.shape[1]

    def grp(n):
        return pl.BlockSpec((n, TM, LANES), lambda i: (0, i, 0))

    mem_spec = pl.BlockSpec((1, mem_len, D_MODEL), lambda i: (i // nblk_seq, 0, 0))
    return pl.pallas_call(
        _mix_out_kernel,
        grid=(t // TM,),
        in_specs=[pl.BlockSpec((TM, D_MODEL), lambda i: (i, 0)), grp(3), grp(3), grp(2),
                  _const_spec(lw['w_o'].shape), _const_spec(lw['g_cross'].shape),
                  _const_spec(lw['w_cq'].shape), _const_spec(lw['g_xq'].shape),
                  mem_spec, mem_spec, _const_spec(lw['w_co'].shape)],
        out_specs=pl.BlockSpec((TM, D_MODEL), lambda i: (i, 0)),
        out_shape=jax.ShapeDtypeStruct((t, D_MODEL), _F32),
        compiler_params=pltpu.CompilerParams(dimension_semantics=("parallel",), vmem_limit_bytes=VMEM_LIMIT),
        name="mix_out",
    )(x, oa, ob, oc, lw['w_o'], lw['g_cross'], lw['w_cq'], lw['g_xq'], mk, mv, lw['w_co'])


def _ffn_kernel(x_ref, xp_ref, xn_ref, gffn_ref, wup_ref, cw_ref, cb_ref, wdn_ref, y_ref, *, nblk_seq):
    i = pl.program_id(0)
    tm = x_ref.shape[0]
    x = x_ref[...]
    g = gffn_ref[...]
    is_first = (i % nblk_seq) == 0
    is_last = (i % nblk_seq) == nblk_seq - 1
    hp = jnp.where(is_first, 0.0, _rms(xp_ref[...], g))
    hn = jnp.where(is_last, 0.0, _rms(xn_ref[...], g))
    h = jnp.concatenate([hp, _rms(x, g), hn], axis=0).astype(_BF)
    acc = x
    for c in range(N_FF_CH):
        u = _dot(h, wup_ref[c])
        cw = cw_ref[c]
        up = pltpu.roll(u, 1, 0)[8:8 + tm]
        un = pltpu.roll(u, tm + 15, 0)[8:8 + tm]
        uc = up * cw[0:1] + u[8:8 + tm] * cw[1:2] + un * cw[2:3] + cb_ref[c]
        a = uc[:, :FF_CH]
        gate = uc[:, FF_CH:]
        act = (gate * (1.0 / (1.0 + jnp.exp(-gate))) * a).astype(_BF)
        acc = acc + _dot(act, wdn_ref[c])
    y_ref[...] = acc


def _ffn(x, lw, seq):
    t = x.shape[0]
    nblk_seq = seq // TM
    nhalo = t // 8
    per = TM // 8
    return pl.pallas_call(
        functools.partial(_ffn_kernel, nblk_seq=nblk_seq),
        grid=(t // TM,),
        in_specs=[pl.BlockSpec((TM, D_MODEL), lambda i: (i, 0)),
                  pl.BlockSpec((8, D_MODEL), lambda i: (jnp.maximum(i * per - 1, 0), 0)),
                  pl.BlockSpec((8, D_MODEL), lambda i: (jnp.minimum((i + 1) * per, nhalo - 1), 0)),
                  _const_spec(lw['g_ffn'].shape), _const_spec(lw['w_up'].shape),
                  _const_spec(lw['conv_w'].shape), _const_spec(lw['conv_b'].shape),
                  _const_spec(lw['w_down'].shape)],
        out_specs=pl.BlockSpec((TM, D_MODEL), lambda i: (i, 0)),
        out_shape=jax.ShapeDtypeStruct((t, D_MODEL), _F32),
        compiler_params=pltpu.CompilerParams(dimension_semantics=("parallel",), vmem_limit_bytes=VMEM_LIMIT),
        name="conv_ffn",
    )(x, x, x, lw['g_ffn'], lw['w_up'], lw['conv_w'], lw['conv_b'], lw['w_down'])


def _w_in_columns():
    cols = np.full((D_IN_P,), -1, np.int64)
    cols[0:SPLIT_CKV] = np.arange(SPLIT_CKV)
    a = SPLIT_CKV + np.arange(16)
    b = SPLIT_CKV + 16 + np.arange(16)
    cols[384:400], cols[400:416] = b, a
    cols[448:464], cols[464:480] = a, b
    d = np.arange(HEAD_DIM)

    def na_col(kind, head):
        return SPLIT_KR + (kind * NA_HEADS + head) * HEAD_DIM + d

    def dil_col(kind, head):
        return SPLIT_NA + (kind * DIL_ALL + head) * HEAD_DIM + d

    pos = COL_NA
    for pairs in ((0, 1), (2,)):
        for kind in range(3):
            for p in pairs:
                cols[pos:pos + 128] = np.concatenate([na_col(kind, 2 * p), na_col(kind, 2 * p + 1)])
                pos += 128
    assert pos == COL_DIL
    for g in range(DIL_GROUPS):
        for kind in range(3):
            for pp in range(2):
                h0, h1 = dil_col(kind, 4 * g + 2 * pp), dil_col(kind, 4 * g + 2 * pp + 1)
                if kind < 2:
                    grp = np.concatenate([h0[:32], h1[:32], h0[32:], h1[32:]])
                else:
                    grp = np.concatenate([h0, h1])
                cols[pos:pos + 128] = grp
                pos += 128
    assert pos == D_IN_P
    return cols


_W_IN_COLS = _w_in_columns()


def _pad_heads(w, real, n_heads):
    k = w.shape[0]
    w = w.reshape(k, n_heads, real)
    return jnp.pad(w, ((0, 0), (0, 0), (0, LANES - real))).reshape(k, n_heads * LANES)


def _row(v):
    return v.reshape(1, -1).astype(_F32)


def _prep_layer(p):
    src = jnp.asarray(np.maximum(_W_IN_COLS, 0))
    keep = jnp.asarray(_W_IN_COLS >= 0)
    w_in = jnp.where(keep[None, :], p['w_in'][:, src], 0.0).astype(_BF)
    kn, qn = p['mla_kn'], p['mla_qn']
    zeros32 = jnp.zeros((32,), _F32)
    g_kr = jnp.concatenate([kn[80:96], kn[64:80], zeros32, kn[64:80], kn[80:96], zeros32])
    dq, dk = p['dil_qn'], p['dil_kn']
    w_up = p['w_up'].reshape(D_MODEL, 2, N_FF_CH, FF_CH).transpose(2, 0, 1, 3).reshape(N_FF_CH, D_MODEL, 2 * FF_CH)
    conv_w = p['conv_w'].reshape(CONV_W, 2, N_FF_CH, FF_CH).transpose(2, 0, 1, 3).reshape(N_FF_CH, CONV_W, 2 * FF_CH)
    conv_b = p['conv_b'].reshape(2, N_FF_CH, FF_CH).transpose(1, 0, 2).reshape(N_FF_CH, 1, 2 * FF_CH)
    return {
        'g_mix': _row(p['norm_mix']), 'w_in': w_in,
        'g_cq': _row(p['mla_q_norm']), 'g_ckv': _row(p['mla_kv_norm']),
        'w_uq': _pad_heads(p['w_uq'], MLA_QK, MLA_HEADS).astype(_BF),
        'w_uk': _pad_heads(p['w_uk'], MLA_NOPE, MLA_HEADS).astype(_BF),
        'w_uv': p['w_uv'].astype(_BF),
        'g_mq': _row(jnp.concatenate([qn, zeros32])),
        'g_mk': _row(jnp.concatenate([kn[:MLA_NOPE], jnp.zeros((64,), _F32)])),
        'g_kr': _row(g_kr),
        'g_nq': _row(jnp.tile(p['na_qn'], 2)), 'g_nk': _row(jnp.tile(p['na_kn'], 2)),
        'g_dq': _row(jnp.concatenate([dq[:32], dq[:32], dq[32:], dq[32:]])),
        'g_dk': _row(jnp.concatenate([dk[:32], dk[:32], dk[32:], dk[32:]])),
        'w_o': p['w_o'].astype(_BF),
        'g_cross': _row(p['norm_cross']), 'w_cq': p['w_cq'].astype(_BF),
        'g_xq': _row(p['x_qn']), 'g_xk': _row(p['x_kn']), 'g_mem': _row(p['norm_mem']),
        'w_ckv': p['w_ckv'].astype(_BF), 'w_co': p['w_co'].astype(_BF),
        'g_ffn': _row(p['norm_ffn']), 'w_up': w_up.astype(_BF),
        'conv_w': conv_w.astype(_F32), 'conv_b': conv_b.astype(_F32),
        'w_down': p['w_down'].reshape(N_FF_CH, FF_CH, D_MODEL).astype(_BF),
        'na_bias': _na_bias(p['na_rpb']),
    }


def _na_bias(rpb):
    dl = np.arange(NA_WIN_R)
    j = np.arange(NA_WIN_R)
    r_off = j[None, :] - dl[:, None] + (NA_WIN_R - 1)
    qc = np.arange(GRID_W)
    kc = np.arange(GRID_W)
    c_start = np.clip(qc - NA_WIN_C // 2, 0, GRID_W - NA_WIN_C)
    valid = (kc[None, :] >= c_start[:, None]) & (kc[None, :] < c_start[:, None] + NA_WIN_C)
    c_off = np.clip(kc[None, :] - qc[:, None] + (NA_WIN_C - 1), 0, 2 * NA_WIN_C - 2)
    b = rpb[:, r_off]
    b = b[:, :, :, c_off]
    b = jnp.where(jnp.asarray(valid)[None, None, None], b, NEG)
    b = b.transpose(0, 1, 3, 2, 4).reshape(NA_HEADS // 2, 2, NA_WIN_R, GRID_W, NA_WIN_R * GRID_W)
    return b.transpose(0, 2, 1, 3, 4).astype(_F32)


def _band_bias():
    i = np.arange(DIL_TQ)[:, None]
    c = np.arange(2 * DIL_TQ)[None, :]
    band = (c >= i) & (c <= i + DIL_TQ)
    out = []
    for edge in range(4):
        ok = band
        if edge & 1:
            ok = ok & (c >= DIL_TQ // 2)
        if edge & 2:
            ok = ok & (c < 2 * DIL_TQ - DIL_TQ // 2)
        out.append(np.where(ok, 0.0, NEG))
    return jnp.asarray(np.stack(out), _F32)


def _rope_tables(seq):
    pos = jnp.arange(seq, dtype=_F32)[:, None]

    def cs(half):
        inv = ROPE_THETA ** (-jnp.arange(half, dtype=_F32) / half)
        ang = pos * inv[None, :]
        return jnp.cos(ang), jnp.sin(ang)

    c16, s16 = cs(MLA_ROPE // 2)
    c32, s32 = cs(HEAD_DIM // 2)
    one = jnp.ones((seq, 64), _F32)
    z16 = jnp.zeros((seq, 16), _F32)
    z32 = jnp.zeros((seq, 32), _F32)
    z64 = jnp.zeros((seq, 64), _F32)
    return {
        'q_c': jnp.concatenate([one, c16, c16, z32], axis=1),
        'q_s1': jnp.concatenate([z64, z16, s16, z32], axis=1),
        'q_s2': jnp.concatenate([z64, -s16, z16, z32], axis=1),
        'k_c': jnp.concatenate([z64, c16, c16, z32], axis=1),
        'k_s': jnp.concatenate([z64, -s16, s16, z32], axis=1),
        'd_c': jnp.concatenate([c32, c32, c32, c32], axis=1),
        'd_s': jnp.concatenate([-s32, -s32, s32, s32], axis=1),
    }


def _trunk_layer(x, mem, lw, tabs, band, batch, seq):
    qm, km, vm, qn, kn, vn, qd, kd, vd = _mix_in(x, lw, tabs, seq)
    oa = _mla(qm, km, vm, batch, seq)
    ob = _na(qn, kn, vn, lw['na_bias'], batch, seq)
    oc = _dil(qd, kd, vd, band, batch, seq)
    mk, mv = _mem_kv(mem, lw)
    x = _mix_out(x, oa, ob, oc, mk, mv, lw, seq)
    return _ffn(x, lw, seq)


def kernel(x_prompt, x_sample, mem_prompt, mem_sample, norm_mix, w_in, mla_q_norm, mla_kv_norm, w_uq, w_uk, w_uv, mla_qn, mla_kn, na_qn, na_kn, na_rpb, dil_qn, dil_kn, w_o, norm_cross, norm_mem, w_cq, w_ckv, x_qn, x_kn, w_co, norm_ffn, w_up, conv_w, conv_b, w_down):
    stacked = {
        'norm_mix': norm_mix, 'w_in': w_in, 'mla_q_norm': mla_q_norm, 'mla_kv_norm': mla_kv_norm,
        'w_uq': w_uq, 'w_uk': w_uk, 'w_uv': w_uv, 'mla_qn': mla_qn, 'mla_kn': mla_kn,
        'na_qn': na_qn, 'na_kn': na_kn, 'na_rpb': na_rpb, 'dil_qn': dil_qn, 'dil_kn': dil_kn,
        'w_o': w_o, 'norm_cross': norm_cross, 'norm_mem': norm_mem, 'w_cq': w_cq, 'w_ckv': w_ckv,
        'x_qn': x_qn, 'x_kn': x_kn, 'w_co': w_co, 'norm_ffn': norm_ffn, 'w_up': w_up,
        'conv_w': conv_w, 'conv_b': conv_b, 'w_down': w_down,
    }
    layers = [_prep_layer({k: v[l] for k, v in stacked.items()}) for l in range(DEPTH)]
    band = _band_bias()
    outs = []
    for x, mem in ((x_prompt, mem_prompt), (x_sample, mem_sample)):
        batch, seq, _ = x.shape
        tabs = _rope_tables(seq)
        xt = x.reshape(batch * seq, D_MODEL)
        for lw in layers:
            xt = _trunk_layer(xt, mem, lw, tabs, band, batch, seq)
        outs.append(xt.reshape(batch, seq, D_MODEL))
    return tuple(outs)
```

```python
import functools

import jax
import jax.numpy as jnp
import numpy as np
from jax import lax
from jax.experimental import pallas as pl
from jax.experimental.pallas import tpu as pltpu

D_MODEL = 1024
DEPTH = 4
GRID_W = 64
HEAD_DIM = 64
ROPE_THETA = 10000.0
EPS = 1e-6
MLA_HEADS = 6
MLA_Q_RANK = 256
MLA_KV_RANK = 128
MLA_NOPE = 64
MLA_ROPE = 32
MLA_V = 64
MLA_QK = MLA_NOPE + MLA_ROPE
NA_HEADS = 6
NA_WIN_R = 8
NA_WIN_C = 16
DIL_WINDOWS = (128, 512, 2048)
DIL_RATES = (1, 4, 16)
DIL_GROUPS = 3
DIL_HEADS = 4
DIL_ALL = DIL_GROUPS * DIL_HEADS
X_HEADS = 4
X_HEAD_DIM = D_MODEL // X_HEADS
D_FF = 2816
CONV_W = 3
SPLIT_CQ = MLA_Q_RANK
SPLIT_CKV = SPLIT_CQ + MLA_KV_RANK
SPLIT_KR = SPLIT_CKV + MLA_ROPE
SPLIT_NA = SPLIT_KR + 3 * NA_HEADS * HEAD_DIM
D_IN = SPLIT_NA + 3 * DIL_ALL * HEAD_DIM

LANES = 128
NEG = -1e30
LOG2E = 1.4426950408889634
TM = 512
DIL_TQ = 128
DIL_SB = DIL_TQ * max(DIL_RATES)
NA_RB = 8
FF_CH = 256
N_FF_CH = D_FF // FF_CH
VMEM_LIMIT = 56 * 1024 * 1024

COL_MLA = 0
COL_NA = 512
COL_DIL = COL_NA + 3 * NA_HEADS * HEAD_DIM
D_IN_P = COL_DIL + 3 * DIL_ALL * HEAD_DIM

_BF = jnp.bfloat16
_F32 = jnp.float32


def _dot(a, b):
    return jnp.dot(a, b, preferred_element_type=_F32)


def _dot_t(a, b):
    return lax.dot_general(a, b, (((1,), (1,)), ((), ())), preferred_element_type=_F32)


def _rms(x, g):
    ms = jnp.mean(x * x, axis=-1, keepdims=True)
    return x * lax.rsqrt(ms + EPS) * g


def _lane(shape):
    return lax.broadcasted_iota(jnp.int32, shape, len(shape) - 1)


def _pair_norm(x, first_head):
    sq = x * x
    tot = jnp.sum(sq, axis=-1, keepdims=True)
    s0 = jnp.sum(jnp.where(first_head, sq, 0.0), axis=-1, keepdims=True)
    inv0 = lax.rsqrt(s0 * (1.0 / HEAD_DIM) + EPS)
    inv1 = lax.rsqrt((tot - s0) * (1.0 / HEAD_DIM) + EPS)
    return x * jnp.where(first_head, inv0, inv1)


def _mix_in_kernel(x_ref, gmix_ref, win_ref, gcq_ref, gckv_ref, wuq_ref, wuk_ref, wuv_ref,
                   gmq_ref, gmk_ref, gkr_ref, gnq_ref, gnk_ref, gdq_ref, gdk_ref,
                   tqc_ref, tqs1_ref, tqs2_ref, tkc_ref, tks_ref, tdc_ref, tds_ref,
                   qm_ref, km_ref, vm_ref, qn_ref, kn_ref, vn_ref, qd_ref, kd_ref, vd_ref,
                   perm_ref):
    tm = x_ref.shape[0]
    h = _rms(x_ref[...], gmix_ref[...]).astype(_BF)

    z = _dot(h, win_ref[:, COL_MLA:COL_NA])
    cq = _rms(z[:, 0:MLA_Q_RANK], gcq_ref[...]).astype(_BF)
    ckv = _rms(z[:, MLA_Q_RANK:MLA_Q_RANK + MLA_KV_RANK], gckv_ref[...]).astype(_BF)
    kr = z[:, 384:512]
    lane = _lane((tm, LANES))
    kr_lanes = (lane >= MLA_NOPE) & (lane < MLA_QK)
    ss_r = jnp.sum(jnp.where(kr_lanes, kr * kr, 0.0), axis=-1, keepdims=True)
    krg = kr * gkr_ref[...]
    kr_rot = krg * tkc_ref[...] + pltpu.roll(krg, 64, 1) * tks_ref[...]
    qpre = _dot(cq, wuq_ref[...])
    kpre = _dot(ckv, wuk_ref[...])
    vpre = _dot(ckv, wuv_ref[...])
    for hd in range(MLA_HEADS):
        qh = qpre[:, hd * LANES:(hd + 1) * LANES]
        inv_q = lax.rsqrt(jnp.sum(qh * qh, axis=-1, keepdims=True) * (1.0 / MLA_QK) + EPS)
        qg = qh * gmq_ref[...]
        qr = qg * tqc_ref[...] + pltpu.roll(qg, 16, 1) * tqs1_ref[...] + pltpu.roll(qg, 112, 1) * tqs2_ref[...]
        qm_ref[hd] = (qr * (inv_q * (MLA_QK ** -0.5 * LOG2E))).astype(_BF)
        kh = kpre[:, hd * LANES:(hd + 1) * LANES]
        inv_k = lax.rsqrt((jnp.sum(kh * kh, axis=-1, keepdims=True) + ss_r) * (1.0 / MLA_QK) + EPS)
        km_ref[hd] = ((kh * gmk_ref[...] + kr_rot) * inv_k).astype(_BF)
        vm_ref[hd] = jnp.where(lane == MLA_V, 1.0, vpre[:, hd * LANES:(hd + 1) * LANES]).astype(_BF)

    z = _dot(h, win_ref[:, COL_NA:COL_DIL])
    first = lane < HEAD_DIM
    npair = NA_HEADS // 2
    for p in range(npair):
        base = 0 if p < 2 else 768
        pp = p if p < 2 else 0
        width = 256 if p < 2 else 128
        q2 = z[:, base + pp * LANES: base + (pp + 1) * LANES]
        k2 = z[:, base + width + pp * LANES: base + width + (pp + 1) * LANES]
        v2 = z[:, base + 2 * width + pp * LANES: base + 2 * width + (pp + 1) * LANES]
        qn_ref[p] = (_pair_norm(q2, first) * (gnq_ref[...] * HEAD_DIM ** -0.5)).astype(_BF)
        kn_ref[p] = (_pair_norm(k2, first) * gnk_ref[...]).astype(_BF)
        vn_ref[p] = v2.astype(_BF)

    first_d = (lane % HEAD_DIM) < (HEAD_DIM // 2)

    def put(out_ref, p, val, rate):
        if rate == 1:
            out_ref[p] = val.astype(_BF)
        else:
            rows = tm // rate
            perm_ref[...] = val
            for rho in range(rate):
                out_ref[p, rho * rows:(rho + 1) * rows, :] = perm_ref[pl.ds(rho, rows, stride=rate), :].astype(_BF)

    for g in range(DIL_GROUPS):
        z = _dot(h, win_ref[:, COL_DIL + g * 768: COL_DIL + (g + 1) * 768])
        for pp in range(2):
            p = 2 * g + pp
            q2 = z[:, pp * LANES:(pp + 1) * LANES]
            k2 = z[:, 256 + pp * LANES: 256 + (pp + 1) * LANES]
            v2 = z[:, 512 + pp * LANES: 512 + (pp + 1) * LANES]
            qx = _pair_norm(q2, first_d) * (gdq_ref[...] * HEAD_DIM ** -0.5)
            qx = qx * tdc_ref[...] + pltpu.roll(qx, 64, 1) * tds_ref[...]
            kx = _pair_norm(k2, first_d) * gdk_ref[...]
            kx = kx * tdc_ref[...] + pltpu.roll(kx, 64, 1) * tds_ref[...]
            put(qd_ref, p, qx, DIL_RATES[g])
            put(kd_ref, p, kx, DIL_RATES[g])
            put(vd_ref, p, v2, DIL_RATES[g])


def _const_spec(shape):
    nd = len(shape)
    return pl.BlockSpec(shape, lambda i, _nd=nd: (0,) * _nd)


def _mix_in(x, lw, tabs, seq):
    t = x.shape[0]
    nblk_seq = seq // TM
    tab_spec = pl.BlockSpec((TM, LANES), lambda i: (i % nblk_seq, 0))
    weights = [lw['g_mix'], lw['w_in'], lw['g_cq'], lw['g_ckv'], lw['w_uq'], lw['w_uk'], lw['w_uv'],
               lw['g_mq'], lw['g_mk'], lw['g_kr'], lw['g_nq'], lw['g_nk'], lw['g_dq'], lw['g_dk']]
    tables = [tabs['q_c'], tabs['q_s1'], tabs['q_s2'], tabs['k_c'], tabs['k_s'], tabs['d_c'], tabs['d_s']]

    def out(n):
        return (jax.ShapeDtypeStruct((n, t, LANES), _BF), pl.BlockSpec((n, TM, LANES), lambda i: (0, i, 0)))

    outs = [out(6), out(6), out(6), out(3), out(3), out(3), out(6), out(6), out(6)]
    return pl.pallas_call(
        _mix_in_kernel,
        grid=(t // TM,),
        in_specs=[pl.BlockSpec((TM, D_MODEL), lambda i: (i, 0))]
        + [_const_spec(w.shape) for w in weights] + [tab_spec] * len(tables),
        out_specs=[o[1] for o in outs],
        out_shape=[o[0] for o in outs],
        scratch_shapes=[pltpu.VMEM((TM, LANES), _F32)],
        compiler_params=pltpu.CompilerParams(dimension_semantics=("parallel",), vmem_limit_bytes=VMEM_LIMIT),
        name="mix_in",
    )(x, *weights, *tables)


def _mla_kernel(q_ref, k_ref, v_ref, o_ref, m_sc, acc_sc, *, tk):
    tq = q_ref.shape[1]
    nk = k_ref.shape[1] // tk
    m_sc[...] = jnp.full(m_sc.shape, NEG, _F32)
    acc_sc[...] = jnp.zeros(acc_sc.shape, _F32)

    def body(j, carry):
        rows = pl.ds(pl.multiple_of(j * tk, tk), tk)
        for hh in range(2):
            s = _dot_t(q_ref[hh], k_ref[hh, rows, :])
            m_prev = m_sc[hh]
            m_new = jnp.maximum(m_prev, jnp.max(s, axis=1, keepdims=True))
            alpha = jnp.exp2(m_prev - m_new)
            p = jnp.exp2(s - jnp.tile(m_new, (1, tk // LANES)))
            acc_sc[hh] = alpha * acc_sc[hh] + _dot(p.astype(_BF), v_ref[hh, rows, :])
            m_sc[hh] = m_new
        return carry

    lax.fori_loop(0, nk, body, 0)
    lane = _lane((tq, LANES))
    o0 = acc_sc[0] / acc_sc[0][:, MLA_V:MLA_V + 1]
    o1 = acc_sc[1] / acc_sc[1][:, MLA_V:MLA_V + 1]
    o_ref[0] = jnp.where(lane < MLA_V, o0, pltpu.roll(o1, MLA_V, 1)).astype(o_ref.dtype)


def _mla(q, k, v, batch, seq, tq=512, tk=512):
    t = q.shape[1]
    nq = seq // tq
    return pl.pallas_call(
        functools.partial(_mla_kernel, tk=tk),
        grid=(batch, MLA_HEADS // 2, nq),
        in_specs=[pl.BlockSpec((2, tq, LANES), lambda b, p, i: (p, b * nq + i, 0)),
                  pl.BlockSpec((2, seq, LANES), lambda b, p, i: (p, b, 0)),
                  pl.BlockSpec((2, seq, LANES), lambda b, p, i: (p, b, 0))],
        out_specs=pl.BlockSpec((1, tq, LANES), lambda b, p, i: (p, b * nq + i, 0)),
        out_shape=jax.ShapeDtypeStruct((MLA_HEADS // 2, t, LANES), _BF),
        scratch_shapes=[pltpu.VMEM((2, tq, LANES), _F32), pltpu.VMEM((2, tq, LANES), _F32)],
        compiler_params=pltpu.CompilerParams(
            dimension_semantics=("parallel", "parallel", "arbitrary"), vmem_limit_bytes=VMEM_LIMIT),
        name="mla_attention",
    )(q, k, v)


def _na_kernel(q_ref, k_ref, v_ref, bias_ref, o_ref, *, rows):
    i = pl.program_id(2)
    lane = _lane((GRID_W, LANES))
    first = lane < HEAD_DIM
    nkeys = NA_WIN_R * GRID_W

    def body(rr, carry):
        r = i * NA_RB + rr
        rs = jnp.clip(r - NA_WIN_R // 2, 0, rows - NA_WIN_R)
        delta = r - rs
        q = q_ref[0, pl.ds(pl.multiple_of(rr * GRID_W, GRID_W), GRID_W), :]
        kw = k_ref[0, pl.ds(pl.multiple_of(rs * GRID_W, GRID_W), nkeys), :]
        vw = v_ref[0, pl.ds(pl.multiple_of(rs * GRID_W, GRID_W), nkeys), :]
        outs = []
        for hh in range(2):
            qm = jnp.where(first if hh == 0 else jnp.logical_not(first), q, jnp.zeros_like(q))
            s = _dot_t(qm, kw) + bias_ref[0, delta, hh]
            m = jnp.max(s, axis=1, keepdims=True)
            p = jnp.exp(s - m)
            l = jnp.sum(p, axis=1, keepdims=True)
            outs.append(_dot(p.astype(_BF), vw) / l)
        o = jnp.where(first, outs[0], outs[1])
        o_ref[0, pl.ds(pl.multiple_of(rr * GRID_W, GRID_W), GRID_W), :] = o.astype(o_ref.dtype)
        return carry

    lax.fori_loop(0, NA_RB, body, 0)


def _na(q, k, v, bias, batch, seq):
    t = q.shape[1]
    rows = seq // GRID_W
    assert rows >= NA_WIN_R and rows % NA_RB == 0
    nrb = rows // NA_RB
    qrows = NA_RB * GRID_W
    return pl.pallas_call(
        functools.partial(_na_kernel, rows=rows),
        grid=(batch, NA_HEADS // 2, nrb),
        in_specs=[pl.BlockSpec((1, qrows, LANES), lambda b, p, i: (p, b * nrb + i, 0)),
                  pl.BlockSpec((1, seq, LANES), lambda b, p, i: (p, b, 0)),
                  pl.BlockSpec((1, seq, LANES), lambda b, p, i: (p, b, 0)),
                  pl.BlockSpec((1, NA_WIN_R, 2, GRID_W, NA_WIN_R * GRID_W), lambda b, p, i: (p, 0, 0, 0, 0))],
        out_specs=pl.BlockSpec((1, qrows, LANES), lambda b, p, i: (p, b * nrb + i, 0)),
        out_shape=jax.ShapeDtypeStruct((NA_HEADS // 2, t, LANES), _BF),
        compiler_params=pltpu.CompilerParams(
            dimension_semantics=("parallel", "parallel", "arbitrary"), vmem_limit_bytes=VMEM_LIMIT),
        name="na_attention",
    )(q, k, v, bias)


def _dil_kernel(q0_ref, q1_ref, q2_ref, k0_ref, k1_ref, k2_ref, v0_ref, v1_ref, v2_ref, band_ref,
                o_ref, og_sc, lse_sc, *, seq):
    sb = pl.program_id(2)
    q_refs = (q0_ref, q1_ref, q2_ref)
    k_refs = (k0_ref, k1_ref, k2_ref)
    v_refs = (v0_ref, v1_ref, v2_ref)
    lane = _lane((DIL_TQ, LANES))
    first_qk = (lane % HEAD_DIM) < (HEAD_DIM // 2)
    first_v = lane < HEAD_DIM
    n_tiles = DIL_SB // DIL_TQ
    half = DIL_TQ // 2

    for g in range(DIL_GROUPS):
        rate = DIL_RATES[g]
        length = seq // rate
        per_blk = TM // rate
        chunk = min(half, per_blk)
        n_chunks = (2 * DIL_TQ) // chunk
        tiles_per_res = n_tiles // rate
        q_ref, k_ref, v_ref = q_refs[g], k_refs[g], v_refs[g]

        def row_of(m, rho, _per_blk=per_blk):
            return (m // _per_blk) * TM + rho * _per_blk + (m % _per_blk)

        def body(tt, carry, _g=g, _rate=rate, _length=length, _per_blk=per_blk, _chunk=chunk,
                 _n_chunks=n_chunks, _tpr=tiles_per_res, _q=q_ref, _k=k_ref, _v=v_ref):
            rho = tt // _tpr
            mt = tt % _tpr
            m0 = sb * (DIL_SB // _rate) + mt * DIL_TQ
            q = jnp.concatenate(
                [_q[0, pl.ds(pl.multiple_of(row_of(m0 + c * _chunk, rho), _chunk), _chunk), :]
                 for c in range(DIL_TQ // _chunk)], axis=0)
            ks, vs = [], []
            for c in range(_n_chunks):
                ms = jnp.clip(m0 - half + c * _chunk, 0, _length - _chunk)
                row = pl.multiple_of(row_of(ms, rho), _chunk)
                ks.append(_k[0, pl.ds(row, _chunk), :])
                vs.append(_v[0, pl.ds(row, _chunk), :])
            kw = jnp.concatenate(ks, axis=0)
            vw = jnp.concatenate(vs, axis=0)
            edge = (m0 == 0).astype(jnp.int32) + 2 * (m0 == _length - DIL_TQ).astype(jnp.int32)
            bias = band_ref[edge]
            o_pair, lse_pair = None, None
            for hh in range(2):
                qm = jnp.where(first_qk if hh == 0 else jnp.logical_not(first_qk), q, jnp.zeros_like(q))
                s = _dot_t(qm, kw) + bias
                m = jnp.max(s, axis=1, keepdims=True)
                p = jnp.exp(s - m)
                l = jnp.sum(p, axis=1, keepdims=True)
                o = _dot(p.astype(_BF), vw) / l
                lse = jnp.broadcast_to(m + jnp.log(l), (DIL_TQ, LANES))
                if hh == 0:
                    o_pair, lse_pair = o, lse
                else:
                    o_pair = jnp.where(first_v, o_pair, o)
                    lse_pair = jnp.where(first_v, lse_pair, lse)
            base = rho + _rate * mt * DIL_TQ
            if _rate == 1:
                dst = pl.ds(pl.multiple_of(base, DIL_TQ), DIL_TQ)
            else:
                dst = pl.ds(base, DIL_TQ, stride=_rate)
            og_sc[_g, dst, :] = o_pair
            lse_sc[_g, dst, :] = lse_pair
            return carry

        lax.fori_loop(0, n_tiles, body, 0)

    m = jnp.maximum(jnp.maximum(lse_sc[0], lse_sc[1]), lse_sc[2])
    w0 = jnp.exp(lse_sc[0] - m)
    w1 = jnp.exp(lse_sc[1] - m)
    w2 = jnp.exp(lse_sc[2] - m)
    o = (w0 * og_sc[0] + w1 * og_sc[1] + w2 * og_sc[2]) / (w0 + w1 + w2)
    o_ref[0] = o.astype(o_ref.dtype)


def _dil(q, k, v, band, batch, seq):
    t = q.shape[1]
    assert seq % DIL_SB == 0
    nsb = seq // DIL_SB

    def q_spec(g):
        return pl.BlockSpec((1, DIL_SB, LANES), lambda b, sp, s, _g=g: (2 * _g + sp, b * nsb + s, 0))

    def kv_spec(g):
        return pl.BlockSpec((1, seq, LANES), lambda b, sp, s, _g=g: (2 * _g + sp, b, 0))

    return pl.pallas_call(
        functools.partial(_dil_kernel, seq=seq),
        grid=(batch, 2, nsb),
        in_specs=[kv_spec(0), kv_spec(1), kv_spec(2), kv_spec(0), kv_spec(1), kv_spec(2),
                  kv_spec(0), kv_spec(1), kv_spec(2),
                  pl.BlockSpec((4, DIL_TQ, 2 * DIL_TQ), lambda b, sp, s: (0, 0, 0))],
        out_specs=pl.BlockSpec((1, DIL_SB, LANES), lambda b, sp, s: (sp, b * nsb + s, 0)),
        out_shape=jax.ShapeDtypeStruct((2, t, LANES), _BF),
        scratch_shapes=[pltpu.VMEM((DIL_GROUPS, DIL_SB, LANES), _F32),
                        pltpu.VMEM((DIL_GROUPS, DIL_SB, LANES), _F32)],
        compiler_params=pltpu.CompilerParams(
            dimension_semantics=("parallel", "parallel", "arbitrary"), vmem_limit_bytes=VMEM_LIMIT),
        name="dil_attention",
    )(q, q, q, k, k, k, v, v, v, band)


def _mem_kv_kernel(mem_ref, gmem_ref, wckv_ref, gxk_ref, k_ref, v_ref):
    hm = _rms(mem_ref[0], gmem_ref[...]).astype(_BF)
    kv = _dot(hm, wckv_ref[...])
    for hd in range(X_HEADS):
        kh = kv[:, hd * X_HEAD_DIM:(hd + 1) * X_HEAD_DIM]
        k_ref[0, :, hd * X_HEAD_DIM:(hd + 1) * X_HEAD_DIM] = _rms(kh, gxk_ref[...]).astype(_BF)
    v_ref[0] = kv[:, D_MODEL:].astype(_BF)


def _mem_kv(mem, lw):
    b, m, _ = mem.shape
    return pl.pallas_call(
        _mem_kv_kernel,
        grid=(b,),
        in_specs=[pl.BlockSpec((1, m, D_MODEL), lambda i: (i, 0, 0)),
                  _const_spec(lw['g_mem'].shape), _const_spec(lw['w_ckv'].shape), _const_spec(lw['g_xk'].shape)],
        out_specs=[pl.BlockSpec((1, m, D_MODEL), lambda i: (i, 0, 0))] * 2,
        out_shape=[jax.ShapeDtypeStruct((b, m, D_MODEL), _BF)] * 2,
        compiler_params=pltpu.CompilerParams(dimension_semantics=("parallel",), vmem_limit_bytes=VMEM_LIMIT),
        name="mem_kv",
    )(mem, lw['g_mem'], lw['w_ckv'], lw['g_xk'])


def _mix_out_kernel(x_ref, oa_ref, ob_ref, oc_ref, wo_ref, gcross_ref, wcq_ref, gxq_ref,
                    mk_ref, mv_ref, wco_ref, y_ref):
    mix = jnp.concatenate([oa_ref[p] for p in range(3)] + [ob_ref[p] for p in range(3)]
                          + [oc_ref[p] for p in range(2)], axis=1)
    x1 = x_ref[...] + _dot(mix, wo_ref[...])
    hc = _rms(x1, gcross_ref[...]).astype(_BF)
    q = _dot(hc, wcq_ref[...])
    outs = []
    for hd in range(X_HEADS):
        sl = slice(hd * X_HEAD_DIM, (hd + 1) * X_HEAD_DIM)
        qh = (_rms(q[:, sl], gxq_ref[...]) * X_HEAD_DIM ** -0.5).astype(_BF)
        s = _dot_t(qh, mk_ref[0, :, sl])
        m = jnp.max(s, axis=1, keepdims=True)
        p = jnp.exp(s - m)
        l = jnp.sum(p, axis=1, keepdims=True)
        outs.append((_dot(p.astype(_BF), mv_ref[0, :, sl]) / l).astype(_BF))
    o = jnp.concatenate(outs, axis=1)
    y_ref[...] = x1 + _dot(o, wco_ref[...])


def _mix_out(x, oa, ob, oc, mk, mv, lw, seq):
    t = x.shape[0]
    nblk_seq = seq // TM
    mem_len = mk.shape[1]

    def grp(n):
        return pl.BlockSpec((n, TM, LANES), lambda i: (0, i, 0))

    mem_spec = pl.BlockSpec((1, mem_len, D_MODEL), lambda i: (i // nblk_seq, 0, 0))
    return pl.pallas_call(
        _mix_out_kernel,
        grid=(t // TM,),
        in_specs=[pl.BlockSpec((TM, D_MODEL), lambda i: (i, 0)), grp(3), grp(3), grp(2),
                  _const_spec(lw['w_o'].shape), _const_spec(lw['g_cross'].shape),
                  _const_spec(lw['w_cq'].shape), _const_spec(lw['g_xq'].shape),
                  mem_spec, mem_spec, _const_spec(lw['w_co'].shape)],
        out_specs=pl.BlockSpec((TM, D_MODEL), lambda i: (i, 0)),
        out_shape=jax.ShapeDtypeStruct((t, D_MODEL), _F32),
        compiler_params=pltpu.CompilerParams(dimension_semantics=("parallel",), vmem_limit_bytes=VMEM_LIMIT),
        name="mix_out",
    )(x, oa, ob, oc, lw['w_o'], lw['g_cross'], lw['w_cq'], lw['g_xq'], mk, mv, lw['w_co'])


def _ffn_kernel(x_ref, xp_ref, xn_ref, gffn_ref, wup_ref, cw_ref, cb_ref, wdn_ref, y_ref, *, nblk_seq):
    i = pl.program_id(0)
    tm = x_ref.shape[0]
    x = x_ref[...]
    g = gffn_ref[...]
    is_first = (i % nblk_seq) == 0
    is_last = (i % nblk_seq) == nblk_seq - 1
    hp = jnp.where(is_first, 0.0, _rms(xp_ref[...], g))
    hn = jnp.where(is_last, 0.0, _rms(xn_ref[...], g))
    h = jnp.concatenate([hp, _rms(x, g), hn], axis=0).astype(_BF)
    acc = x
    for c in range(N_FF_CH):
        u = _dot(h, wup_ref[c])
        cw = cw_ref[c]
        up = pltpu.roll(u, 1, 0)[8:8 + tm]
        un = pltpu.roll(u, tm + 15, 0)[8:8 + tm]
        uc = up * cw[0:1] + u[8:8 + tm] * cw[1:2] + un * cw[2:3] + cb_ref[c]
        a = uc[:, :FF_CH]
        gate = uc[:, FF_CH:]
        act = (gate * (1.0 / (1.0 + jnp.exp(-gate))) * a).astype(_BF)
        acc = acc + _dot(act, wdn_ref[c])
    y_ref[...] = acc


def _ffn(x, lw, seq):
    t = x.shape[0]
    nblk_seq = seq // TM
    nhalo = t // 8
    per = TM // 8
    return pl.pallas_call(
        functools.partial(_ffn_kernel, nblk_seq=nblk_seq),
        grid=(t // TM,),
        in_specs=[pl.BlockSpec((TM, D_MODEL), lambda i: (i, 0)),
                  pl.BlockSpec((8, D_MODEL), lambda i: (jnp.maximum(i * per - 1, 0), 0)),
                  pl.BlockSpec((8, D_MODEL), lambda i: (jnp.minimum((i + 1) * per, nhalo - 1), 0)),
                  _const_spec(lw['g_ffn'].shape), _const_spec(lw['w_up'].shape),
                  _const_spec(lw['conv_w'].shape), _const_spec(lw['conv_b'].shape),
                  _const_spec(lw['w_down'].shape)],
        out_specs=pl.BlockSpec((TM, D_MODEL), lambda i: (i, 0)),
        out_shape=jax.ShapeDtypeStruct((t, D_MODEL), _F32),
        compiler_params=pltpu.CompilerParams(dimension_semantics=("parallel",), vmem_limit_bytes=VMEM_LIMIT),
        name="conv_ffn",
    )(x, x, x, lw['g_ffn'], lw['w_up'], lw['conv_w'], lw['conv_b'], lw['w_down'])


def _w_in_columns():
    cols = np.full((D_IN_P,), -1, np.int64)
    cols[0:SPLIT_CKV] = np.arange(SPLIT_CKV)
    a = SPLIT_CKV + np.arange(16)
    b = SPLIT_CKV + 16 + np.arange(16)
    cols[384:400], cols[400:416] = b, a
    cols[448:464], cols[464:480] = a, b
    d = np.arange(HEAD_DIM)

    def na_col(kind, head):
        return SPLIT_KR + (kind * NA_HEADS + head) * HEAD_DIM + d

    def dil_col(kind, head):
        return SPLIT_NA + (kind * DIL_ALL + head) * HEAD_DIM + d

    pos = COL_NA
    for pairs in ((0, 1), (2,)):
        for kind in range(3):
            for p in pairs:
                cols[pos:pos + 128] = np.concatenate([na_col(kind, 2 * p), na_col(kind, 2 * p + 1)])
                pos += 128
    assert pos == COL_DIL
    for g in range(DIL_GROUPS):
        for kind in range(3):
            for pp in range(2):
                h0, h1 = dil_col(kind, 4 * g + 2 * pp), dil_col(kind, 4 * g + 2 * pp + 1)
                if kind < 2:
                    grp = np.concatenate([h0[:32], h1[:32], h0[32:], h1[32:]])
                else:
                    grp = np.concatenate([h0, h1])
                cols[pos:pos + 128] = grp
                pos += 128
    assert pos == D_IN_P
    return cols


_W_IN_COLS = _w_in_columns()


def _pad_heads(w, real, n_heads):
    k = w.shape[0]
    w = w.reshape(k, n_heads, real)
    return jnp.pad(w, ((0, 0), (0, 0), (0, LANES - real))).reshape(k, n_heads * LANES)


def _row(v):
    return v.reshape(1, -1).astype(_F32)


def _prep_layer(p):
    src = jnp.asarray(np.maximum(_W_IN_COLS, 0))
    keep = jnp.asarray(_W_IN_COLS >= 0)
    w_in = jnp.where(keep[None, :], p['w_in'][:, src], 0.0).astype(_BF)
    kn, qn = p['mla_kn'], p['mla_qn']
    zeros32 = jnp.zeros((32,), _F32)
    g_kr = jnp.concatenate([kn[80:96], kn[64:80], zeros32, kn[64:80], kn[80:96], zeros32])
    dq, dk = p['dil_qn'], p['dil_kn']
    w_up = p['w_up'].reshape(D_MODEL, 2, N_FF_CH, FF_CH).transpose(2, 0, 1, 3).reshape(N_FF_CH, D_MODEL, 2 * FF_CH)
    conv_w = p['conv_w'].reshape(CONV_W, 2, N_FF_CH, FF_CH).transpose(2, 0, 1, 3).reshape(N_FF_CH, CONV_W, 2 * FF_CH)
    conv_b = p['conv_b'].reshape(2, N_FF_CH, FF_CH).transpose(1, 0, 2).reshape(N_FF_CH, 1, 2 * FF_CH)
    return {
        'g_mix': _row(p['norm_mix']), 'w_in': w_in,
        'g_cq': _row(p['mla_q_norm']), 'g_ckv': _row(p['mla_kv_norm']),
        'w_uq': _pad_heads(p['w_uq'], MLA_QK, MLA_HEADS).astype(_BF),
        'w_uk': _pad_heads(p['w_uk'], MLA_NOPE, MLA_HEADS).astype(_BF),
        'w_uv': _pad_heads(p['w_uv'], MLA_V, MLA_HEADS).astype(_BF),
        'g_mq': _row(jnp.concatenate([qn, zeros32])),
        'g_mk': _row(jnp.concatenate([kn[:MLA_NOPE], jnp.zeros((64,), _F32)])),
        'g_kr': _row(g_kr),
        'g_nq': _row(jnp.tile(p['na_qn'], 2)), 'g_nk': _row(jnp.tile(p['na_kn'], 2)),
        'g_dq': _row(jnp.concatenate([dq[:32], dq[:32], dq[32:], dq[32:]])),
        'g_dk': _row(jnp.concatenate([dk[:32], dk[:32], dk[32:], dk[32:]])),
        'w_o': p['w_o'].astype(_BF),
        'g_cross': _row(p['norm_cross']), 'w_cq': p['w_cq'].astype(_BF),
        'g_xq': _row(p['x_qn']), 'g_xk': _row(p['x_kn']), 'g_mem': _row(p['norm_mem']),
        'w_ckv': p['w_ckv'].astype(_BF), 'w_co': p['w_co'].astype(_BF),
        'g_ffn': _row(p['norm_ffn']), 'w_up': w_up.astype(_BF),
        'conv_w': conv_w.astype(_F32), 'conv_b': conv_b.astype(_F32),
        'w_down': p['w_down'].reshape(N_FF_CH, FF_CH, D_MODEL).astype(_BF),
        'na_bias': _na_bias(p['na_rpb']),
    }


def _na_bias(rpb):
    dl = np.arange(NA_WIN_R)
    j = np.arange(NA_WIN_R)
    r_off = j[None, :] - dl[:, None] + (NA_WIN_R - 1)
    qc = np.arange(GRID_W)
    kc = np.arange(GRID_W)
    c_start = np.clip(qc - NA_WIN_C // 2, 0, GRID_W - NA_WIN_C)
    valid = (kc[None, :] >= c_start[:, None]) & (kc[None, :] < c_start[:, None] + NA_WIN_C)
    c_off = np.clip(kc[None, :] - qc[:, None] + (NA_WIN_C - 1), 0, 2 * NA_WIN_C - 2)
    b = rpb[:, r_off]
    b = b[:, :, :, c_off]
    b = jnp.where(jnp.asarray(valid)[None, None, None], b, NEG)
    b = b.transpose(0, 1, 3, 2, 4).reshape(NA_HEADS // 2, 2, NA_WIN_R, GRID_W, NA_WIN_R * GRID_W)
    return b.transpose(0, 2, 1, 3, 4).astype(_F32)


def _band_bias():
    i = np.arange(DIL_TQ)[:, None]
    c = np.arange(2 * DIL_TQ)[None, :]
    band = (c >= i) & (c <= i + DIL_TQ)
    out = []
    for edge in range(4):
        ok = band
        if edge & 1:
            ok = ok & (c >= DIL_TQ // 2)
        if edge & 2:
            ok = ok & (c < 2 * DIL_TQ - DIL_TQ // 2)
        out.append(np.where(ok, 0.0, NEG))
    return jnp.asarray(np.stack(out), _F32)


def _rope_tables(seq):
    pos = jnp.arange(seq, dtype=_F32)[:, None]

    def cs(half):
        inv = ROPE_THETA ** (-jnp.arange(half, dtype=_F32) / half)
        ang = pos * inv[None, :]
        return jnp.cos(ang), jnp.sin(ang)

    c16, s16 = cs(MLA_ROPE // 2)
    c32, s32 = cs(HEAD_DIM // 2)
    one = jnp.ones((seq, 64), _F32)
    z16 = jnp.zeros((seq, 16), _F32)
    z32 = jnp.zeros((seq, 32), _F32)
    z64 = jnp.zeros((seq, 64), _F32)
    return {
        'q_c': jnp.concatenate([one, c16, c16, z32], axis=1),
        'q_s1': jnp.concatenate([z64, z16, s16, z32], axis=1),
        'q_s2': jnp.concatenate([z64, -s16, z16, z32], axis=1),
        'k_c': jnp.concatenate([z64, c16, c16, z32], axis=1),
        'k_s': jnp.concatenate([z64, -s16, s16, z32], axis=1),
        'd_c': jnp.concatenate([c32, c32, c32, c32], axis=1),
        'd_s': jnp.concatenate([-s32, -s32, s32, s32], axis=1),
    }


def _trunk_layer(x, mem, lw, tabs, band, batch, seq):
    qm, km, vm, qn, kn, vn, qd, kd, vd = _mix_in(x, lw, tabs, seq)
    oa = _mla(qm, km, vm, batch, seq)
    ob = _na(qn, kn, vn, lw['na_bias'], batch, seq)
    oc = _dil(qd, kd, vd, band, batch, seq)
    mk, mv = _mem_kv(mem, lw)
    x = _mix_out(x, oa, ob, oc, mk, mv, lw, seq)
    return _ffn(x, lw, seq)


def kernel(x_prompt, x_sample, mem_prompt, mem_sample, norm_mix, w_in, mla_q_norm, mla_kv_norm, w_uq, w_uk, w_uv, mla_qn, mla_kn, na_qn, na_kn, na_rpb, dil_qn, dil_kn, w_o, norm_cross, norm_mem, w_cq, w_ckv, x_qn, x_kn, w_co, norm_ffn, w_up, conv_w, conv_b, w_down):
    stacked = {
        'norm_mix': norm_mix, 'w_in': w_in, 'mla_q_norm': mla_q_norm, 'mla_kv_norm': mla_kv_norm,
        'w_uq': w_uq, 'w_uk': w_uk, 'w_uv': w_uv, 'mla_qn': mla_qn, 'mla_kn': mla_kn,
        'na_qn': na_qn, 'na_kn': na_kn, 'na_rpb': na_rpb, 'dil_qn': dil_qn, 'dil_kn': dil_kn,
        'w_o': w_o, 'norm_cross': norm_cross, 'norm_mem': norm_mem, 'w_cq': w_cq, 'w_ckv': w_ckv,
        'x_qn': x_qn, 'x_kn': x_kn, 'w_co': w_co, 'norm_ffn': norm_ffn, 'w_up': w_up,
        'conv_w': conv_w, 'conv_b': conv_b, 'w_down': w_down,
    }
    layers = [_prep_layer({k: v[l] for k, v in stacked.items()}) for l in range(DEPTH)]
    band = _band_bias()
    outs = []
    for x, mem in ((x_prompt, mem_prompt), (x_sample, mem_sample)):
        batch, seq, _ = x.shape
        tabs = _rope_tables(seq)
        xt = x.reshape(batch * seq, D_MODEL)
        for lw in layers:
            xt = _trunk_layer(xt, mem, lw, tabs, band, batch, seq)
        outs.append(xt.reshape(batch, seq, D_MODEL))
    return tuple(outs)
```

```python
import functools

import jax
import jax.numpy as jnp
import numpy as np
from jax import lax
from jax.experimental import pallas as pl
from jax.experimental.pallas import tpu as pltpu

D_MODEL = 1024
DEPTH = 4
GRID_W = 64
HEAD_DIM = 64
ROPE_THETA = 10000.0
EPS = 1e-6
MLA_HEADS = 6
MLA_Q_RANK = 256
MLA_KV_RANK = 128
MLA_NOPE = 64
MLA_ROPE = 32
MLA_V = 64
MLA_QK = MLA_NOPE + MLA_ROPE
NA_HEADS = 6
NA_WIN_R = 8
NA_WIN_C = 16
DIL_WINDOWS = (128, 512, 2048)
DIL_RATES = (1, 4, 16)
DIL_GROUPS = 3
DIL_HEADS = 4
DIL_ALL = DIL_GROUPS * DIL_HEADS
X_HEADS = 4
X_HEAD_DIM = D_MODEL // X_HEADS
D_FF = 2816
CONV_W = 3
SPLIT_CQ = MLA_Q_RANK
SPLIT_CKV = SPLIT_CQ + MLA_KV_RANK
SPLIT_KR = SPLIT_CKV + MLA_ROPE
SPLIT_NA = SPLIT_KR + 3 * NA_HEADS * HEAD_DIM
D_IN = SPLIT_NA + 3 * DIL_ALL * HEAD_DIM

LANES = 128
NEG = -1e30
LOG2E = 1.4426950408889634
TM = 512
DIL_TQ = 128
DIL_SB = DIL_TQ * max(DIL_RATES)
DIL_UNROLL = 4
NA_RB = 8
FF_CH = 256
N_FF_CH = D_FF // FF_CH
VMEM_LIMIT = 56 * 1024 * 1024

COL_MLA = 0
COL_NA = 512
COL_DIL = COL_NA + 3 * NA_HEADS * HEAD_DIM
D_IN_P = COL_DIL + 3 * DIL_ALL * HEAD_DIM

_BF = jnp.bfloat16
_F32 = jnp.float32


def _dot(a, b):
    return jnp.dot(a, b, preferred_element_type=_F32)


def _dot_t(a, b):
    return lax.dot_general(a, b, (((1,), (1,)), ((), ())), preferred_element_type=_F32)


def _rms(x, g):
    ms = jnp.mean(x * x, axis=-1, keepdims=True)
    return x * lax.rsqrt(ms + EPS) * g


def _lane(shape):
    return lax.broadcasted_iota(jnp.int32, shape, len(shape) - 1)


def _pair_norm(x, first_head):
    sq = x * x
    tot = jnp.sum(sq, axis=-1, keepdims=True)
    s0 = jnp.sum(jnp.where(first_head, sq, 0.0), axis=-1, keepdims=True)
    inv0 = lax.rsqrt(s0 * (1.0 / HEAD_DIM) + EPS)
    inv1 = lax.rsqrt((tot - s0) * (1.0 / HEAD_DIM) + EPS)
    return x * jnp.where(first_head, inv0, inv1)


def _mix_in_kernel(x_ref, gmix_ref, win_ref, gcq_ref, gckv_ref, wuq_ref, wuk_ref, wuv_ref,
                   gmq_ref, gmk_ref, gkr_ref, gnq_ref, gnk_ref, gdq_ref, gdk_ref,
                   tqc_ref, tqs1_ref, tqs2_ref, tkc_ref, tks_ref, tdc_ref, tds_ref,
                   qm_ref, km_ref, vm_ref, qn_ref, kn_ref, vn_ref, qd_ref, kd_ref, vd_ref,
                   perm_ref):
    tm = x_ref.shape[0]
    h = _rms(x_ref[...], gmix_ref[...]).astype(_BF)

    z = _dot(h, win_ref[:, COL_MLA:COL_NA])
    cq = _rms(z[:, 0:MLA_Q_RANK], gcq_ref[...]).astype(_BF)
    ckv = _rms(z[:, MLA_Q_RANK:MLA_Q_RANK + MLA_KV_RANK], gckv_ref[...]).astype(_BF)
    kr = z[:, 384:512]
    lane = _lane((tm, LANES))
    kr_lanes = (lane >= MLA_NOPE) & (lane < MLA_QK)
    ss_r = jnp.sum(jnp.where(kr_lanes, kr * kr, 0.0), axis=-1, keepdims=True)
    krg = kr * gkr_ref[...]
    kr_rot = krg * tkc_ref[...] + pltpu.roll(krg, 64, 1) * tks_ref[...]
    qpre = _dot(cq, wuq_ref[...])
    kpre = _dot(ckv, wuk_ref[...])
    vpre = _dot(ckv, wuv_ref[...])
    for hd in range(MLA_HEADS):
        qh = qpre[:, hd * LANES:(hd + 1) * LANES]
        inv_q = lax.rsqrt(jnp.sum(qh * qh, axis=-1, keepdims=True) * (1.0 / MLA_QK) + EPS)
        qg = qh * gmq_ref[...]
        qr = qg * tqc_ref[...] + pltpu.roll(qg, 16, 1) * tqs1_ref[...] + pltpu.roll(qg, 112, 1) * tqs2_ref[...]
        qm_ref[hd] = (qr * (inv_q * (MLA_QK ** -0.5 * LOG2E))).astype(_BF)
        kh = kpre[:, hd * LANES:(hd + 1) * LANES]
        inv_k = lax.rsqrt((jnp.sum(kh * kh, axis=-1, keepdims=True) + ss_r) * (1.0 / MLA_QK) + EPS)
        km_ref[hd] = ((kh * gmk_ref[...] + kr_rot) * inv_k).astype(_BF)
        vm_ref[hd] = jnp.where(lane == MLA_V, 1.0, vpre[:, hd * LANES:(hd + 1) * LANES]).astype(_BF)

    z = _dot(h, win_ref[:, COL_NA:COL_DIL])
    first = lane < HEAD_DIM
    npair = NA_HEADS // 2
    for p in range(npair):
        base = 0 if p < 2 else 768
        pp = p if p < 2 else 0
        width = 256 if p < 2 else 128
        q2 = z[:, base + pp * LANES: base + (pp + 1) * LANES]
        k2 = z[:, base + width + pp * LANES: base + width + (pp + 1) * LANES]
        v2 = z[:, base + 2 * width + pp * LANES: base + 2 * width + (pp + 1) * LANES]
        qn_ref[p] = (_pair_norm(q2, first) * (gnq_ref[...] * HEAD_DIM ** -0.5)).astype(_BF)
        kn_ref[p] = (_pair_norm(k2, first) * gnk_ref[...]).astype(_BF)
        vn_ref[p] = v2.astype(_BF)

    first_d = (lane % HEAD_DIM) < (HEAD_DIM // 2)

    def put(out_ref, p, val, rate):
        if rate == 1:
            out_ref[p] = val.astype(_BF)
        else:
            rows = tm // rate
            perm_ref[...] = val
            for rho in range(rate):
                out_ref[p, rho * rows:(rho + 1) * rows, :] = perm_ref[pl.ds(rho, rows, stride=rate), :].astype(_BF)

    for g in range(DIL_GROUPS):
        z = _dot(h, win_ref[:, COL_DIL + g * 768: COL_DIL + (g + 1) * 768])
        for pp in range(2):
            p = 2 * g + pp
            q2 = z[:, pp * LANES:(pp + 1) * LANES]
            k2 = z[:, 256 + pp * LANES: 256 + (pp + 1) * LANES]
            v2 = z[:, 512 + pp * LANES: 512 + (pp + 1) * LANES]
            qx = _pair_norm(q2, first_d) * (gdq_ref[...] * HEAD_DIM ** -0.5)
            qx = qx * tdc_ref[...] + pltpu.roll(qx, 64, 1) * tds_ref[...]
            kx = _pair_norm(k2, first_d) * gdk_ref[...]
            kx = kx * tdc_ref[...] + pltpu.roll(kx, 64, 1) * tds_ref[...]
            put(qd_ref, p, qx, DIL_RATES[g])
            put(kd_ref, p, kx, DIL_RATES[g])
            put(vd_ref, p, v2, DIL_RATES[g])


def _const_spec(shape):
    nd = len(shape)
    return pl.BlockSpec(shape, lambda i, _nd=nd: (0,) * _nd)


def _mix_in(x, lw, tabs, seq):
    t = x.shape[0]
    nblk_seq = seq // TM
    tab_spec = pl.BlockSpec((TM, LANES), lambda i: (i % nblk_seq, 0))
    weights = [lw['g_mix'], lw['w_in'], lw['g_cq'], lw['g_ckv'], lw['w_uq'], lw['w_uk'], lw['w_uv'],
               lw['g_mq'], lw['g_mk'], lw['g_kr'], lw['g_nq'], lw['g_nk'], lw['g_dq'], lw['g_dk']]
    tables = [tabs['q_c'], tabs['q_s1'], tabs['q_s2'], tabs['k_c'], tabs['k_s'], tabs['d_c'], tabs['d_s']]

    def out(n):
        return (jax.ShapeDtypeStruct((n, t, LANES), _BF), pl.BlockSpec((n, TM, LANES), lambda i: (0, i, 0)))

    outs = [out(6), out(6), out(6), out(3), out(3), out(3), out(6), out(6), out(6)]
    return pl.pallas_call(
        _mix_in_kernel,
        grid=(t // TM,),
        in_specs=[pl.BlockSpec((TM, D_MODEL), lambda i: (i, 0))]
        + [_const_spec(w.shape) for w in weights] + [tab_spec] * len(tables),
        out_specs=[o[1] for o in outs],
        out_shape=[o[0] for o in outs],
        scratch_shapes=[pltpu.VMEM((TM, LANES), _F32)],
        compiler_params=pltpu.CompilerParams(dimension_semantics=("parallel",), vmem_limit_bytes=VMEM_LIMIT),
        name="mix_in",
    )(x, *weights, *tables)


def _mla_kernel(q_ref, k_ref, v_ref, o_ref, m_sc, acc_sc, *, tk):
    tq = q_ref.shape[1]
    nk = k_ref.shape[1] // tk
    m_sc[...] = jnp.full(m_sc.shape, NEG, _F32)
    acc_sc[...] = jnp.zeros(acc_sc.shape, _F32)

    def body(j, carry):
        rows = pl.ds(pl.multiple_of(j * tk, tk), tk)
        scores = [_dot_t(q_ref[hh], k_ref[hh, rows, :]) for hh in range(2)]
        for hh, s in enumerate(scores):
            m_prev = m_sc[hh]
            m_new = jnp.maximum(m_prev, jnp.max(s, axis=1, keepdims=True))
            alpha = jnp.exp2(m_prev - m_new)
            p = jnp.exp2(s - jnp.tile(m_new, (1, tk // LANES)))
            acc_sc[hh] = alpha * acc_sc[hh] + _dot(p.astype(_BF), v_ref[hh, rows, :])
            m_sc[hh] = m_new
        return carry

    lax.fori_loop(0, nk, body, 0)
    lane = _lane((tq, LANES))
    o0 = acc_sc[0] / acc_sc[0][:, MLA_V:MLA_V + 1]
    o1 = acc_sc[1] / acc_sc[1][:, MLA_V:MLA_V + 1]
    o_ref[0] = jnp.where(lane < MLA_V, o0, pltpu.roll(o1, MLA_V, 1)).astype(o_ref.dtype)


def _mla(q, k, v, batch, seq, tq=512, tk=1024):
    t = q.shape[1]
    nq = seq // tq
    return pl.pallas_call(
        functools.partial(_mla_kernel, tk=tk),
        grid=(batch, MLA_HEADS // 2, nq),
        in_specs=[pl.BlockSpec((2, tq, LANES), lambda b, p, i: (p, b * nq + i, 0)),
                  pl.BlockSpec((2, seq, LANES), lambda b, p, i: (p, b, 0)),
                  pl.BlockSpec((2, seq, LANES), lambda b, p, i: (p, b, 0))],
        out_specs=pl.BlockSpec((1, tq, LANES), lambda b, p, i: (p, b * nq + i, 0)),
        out_shape=jax.ShapeDtypeStruct((MLA_HEADS // 2, t, LANES), _BF),
        scratch_shapes=[pltpu.VMEM((2, tq, LANES), _F32), pltpu.VMEM((2, tq, LANES), _F32)],
        compiler_params=pltpu.CompilerParams(
            dimension_semantics=("parallel", "parallel", "arbitrary"), vmem_limit_bytes=VMEM_LIMIT),
        name="mla_attention",
    )(q, k, v)


def _na_kernel(q_ref, k_ref, v_ref, bias_ref, o_ref, *, rows):
    i = pl.program_id(2)
    lane = _lane((GRID_W, LANES))
    first = lane < HEAD_DIM
    nkeys = NA_WIN_R * GRID_W

    scores, windows = [], []
    for rr in range(NA_RB):
        r = i * NA_RB + rr
        rs = jnp.clip(r - NA_WIN_R // 2, 0, rows - NA_WIN_R)
        start = pl.multiple_of(rs * GRID_W, GRID_W)
        q = q_ref[0, rr * GRID_W:(rr + 1) * GRID_W, :]
        zero = jnp.zeros_like(q)
        q2 = jnp.concatenate([jnp.where(first, q, zero), jnp.where(first, zero, q)], axis=0)
        scores.append(_dot_t(q2, k_ref[0, pl.ds(start, nkeys), :]) + bias_ref[0, r - rs])
        windows.append(start)
    probs = []
    for s in scores:
        p = jnp.exp(s - jnp.max(s, axis=1, keepdims=True))
        probs.append((p.astype(_BF), jnp.sum(p, axis=1, keepdims=True)))
    for rr, ((p, l), start) in enumerate(zip(probs, windows)):
        o2 = _dot(p, v_ref[0, pl.ds(start, nkeys), :]) / l
        o = jnp.where(first, o2[:GRID_W], o2[GRID_W:])
        o_ref[0, rr * GRID_W:(rr + 1) * GRID_W, :] = o.astype(o_ref.dtype)


def _na(q, k, v, bias, batch, seq):
    t = q.shape[1]
    rows = seq // GRID_W
    assert rows >= NA_WIN_R and rows % NA_RB == 0
    nrb = rows // NA_RB
    qrows = NA_RB * GRID_W
    return pl.pallas_call(
        functools.partial(_na_kernel, rows=rows),
        grid=(batch, NA_HEADS // 2, nrb),
        in_specs=[pl.BlockSpec((1, qrows, LANES), lambda b, p, i: (p, b * nrb + i, 0)),
                  pl.BlockSpec((1, seq, LANES), lambda b, p, i: (p, b, 0)),
                  pl.BlockSpec((1, seq, LANES), lambda b, p, i: (p, b, 0)),
                  pl.BlockSpec((1, NA_WIN_R, 2 * GRID_W, NA_WIN_R * GRID_W), lambda b, p, i: (p, 0, 0, 0))],
        out_specs=pl.BlockSpec((1, qrows, LANES), lambda b, p, i: (p, b * nrb + i, 0)),
        out_shape=jax.ShapeDtypeStruct((NA_HEADS // 2, t, LANES), _BF),
        compiler_params=pltpu.CompilerParams(
            dimension_semantics=("parallel", "parallel", "arbitrary"), vmem_limit_bytes=VMEM_LIMIT),
        name="na_attention",
    )(q, k, v, bias)


def _dil_kernel(q0_ref, q1_ref, q2_ref, k0_ref, k1_ref, k2_ref, v0_ref, v1_ref, v2_ref, band_ref,
                o_ref, og_sc, lse_sc, *, seq):
    sb = pl.program_id(2)
    q_refs = (q0_ref, q1_ref, q2_ref)
    k_refs = (k0_ref, k1_ref, k2_ref)
    v_refs = (v0_ref, v1_ref, v2_ref)
    lane = _lane((DIL_TQ, LANES))
    first_qk = (lane % HEAD_DIM) < (HEAD_DIM // 2)
    first_v = lane < HEAD_DIM
    n_tiles = DIL_SB // DIL_TQ
    half = DIL_TQ // 2

    for g in range(DIL_GROUPS):
        rate = DIL_RATES[g]
        length = seq // rate
        per_blk = TM // rate
        chunk = min(half, per_blk)
        n_chunks = (2 * DIL_TQ) // chunk
        tiles_per_res = n_tiles // rate
        q_ref, k_ref, v_ref = q_refs[g], k_refs[g], v_refs[g]

        def row_of(m, rho, _per_blk=per_blk):
            return (m // _per_blk) * TM + rho * _per_blk + (m % _per_blk)

        def body(tg, carry, _g=g, _rate=rate, _length=length, _per_blk=per_blk, _chunk=chunk,
                 _n_chunks=n_chunks, _tpr=tiles_per_res, _q=q_ref, _k=k_ref, _v=v_ref):
            scores, metas = [], []
            for u in range(DIL_UNROLL):
                tt = tg * DIL_UNROLL + u
                rho = tt // _tpr
                mt = tt % _tpr
                m0 = sb * (DIL_SB // _rate) + mt * DIL_TQ
                q = jnp.concatenate(
                    [_q[0, pl.ds(pl.multiple_of(row_of(m0 + c * _chunk, rho), _chunk), _chunk), :]
                     for c in range(DIL_TQ // _chunk)], axis=0)
                zero = jnp.zeros_like(q)
                q2 = jnp.concatenate([jnp.where(first_qk, q, zero), jnp.where(first_qk, zero, q)], axis=0)
                win = []
                for c in range(_n_chunks):
                    ms = jnp.clip(m0 - half + c * _chunk, 0, _length - _chunk)
                    win.append(pl.multiple_of(row_of(ms, rho), _chunk))
                kw = jnp.concatenate([_k[0, pl.ds(row, _chunk), :] for row in win], axis=0)
                edge = (m0 == 0).astype(jnp.int32) + 2 * (m0 == _length - DIL_TQ).astype(jnp.int32)
                scores.append(_dot_t(q2, kw) + band_ref[edge])
                metas.append((rho + _rate * mt * DIL_TQ, win))
            probs = []
            for s in scores:
                m = jnp.max(s, axis=1, keepdims=True)
                p = jnp.exp(s - m)
                probs.append((p.astype(_BF), m, jnp.sum(p, axis=1, keepdims=True)))
            for (p, m, l), (base, win) in zip(probs, metas):
                vw = jnp.concatenate([_v[0, pl.ds(row, _chunk), :] for row in win], axis=0)
                o2 = _dot(p, vw) / l
                lse2 = jnp.broadcast_to(m + jnp.log(l), (2 * DIL_TQ, LANES))
                if _rate == 1:
                    dst = pl.ds(pl.multiple_of(base, DIL_TQ), DIL_TQ)
                else:
                    dst = pl.ds(base, DIL_TQ, stride=_rate)
                og_sc[_g, dst, :] = jnp.where(first_v, o2[:DIL_TQ], o2[DIL_TQ:])
                lse_sc[_g, dst, :] = jnp.where(first_v, lse2[:DIL_TQ], lse2[DIL_TQ:])
            return carry

        lax.fori_loop(0, n_tiles // DIL_UNROLL, body, 0)

    m = jnp.maximum(jnp.maximum(lse_sc[0], lse_sc[1]), lse_sc[2])
    w0 = jnp.exp(lse_sc[0] - m)
    w1 = jnp.exp(lse_sc[1] - m)
    w2 = jnp.exp(lse_sc[2] - m)
    o = (w0 * og_sc[0] + w1 * og_sc[1] + w2 * og_sc[2]) / (w0 + w1 + w2)
    o_ref[0] = o.astype(o_ref.dtype)


def _dil(q, k, v, band, batch, seq):
    t = q.shape[1]
    assert seq % DIL_SB == 0
    nsb = seq // DIL_SB

    def q_spec(g):
        return pl.BlockSpec((1, DIL_SB, LANES), lambda b, sp, s, _g=g: (2 * _g + sp, b * nsb + s, 0))

    def kv_spec(g):
        return pl.BlockSpec((1, seq, LANES), lambda b, sp, s, _g=g: (2 * _g + sp, b, 0))

    return pl.pallas_call(
        functools.partial(_dil_kernel, seq=seq),
        grid=(batch, 2, nsb),
        in_specs=[kv_spec(0), kv_spec(1), kv_spec(2), kv_spec(0), kv_spec(1), kv_spec(2),
                  kv_spec(0), kv_spec(1), kv_spec(2),
                  pl.BlockSpec((4, 2 * DIL_TQ, 2 * DIL_TQ), lambda b, sp, s: (0, 0, 0))],
        out_specs=pl.BlockSpec((1, DIL_SB, LANES), lambda b, sp, s: (sp, b * nsb + s, 0)),
        out_shape=jax.ShapeDtypeStruct((2, t, LANES), _BF),
        scratch_shapes=[pltpu.VMEM((DIL_GROUPS, DIL_SB, LANES), _F32),
                        pltpu.VMEM((DIL_GROUPS, DIL_SB, LANES), _F32)],
        compiler_params=pltpu.CompilerParams(
            dimension_semantics=("parallel", "parallel", "arbitrary"), vmem_limit_bytes=VMEM_LIMIT),
        name="dil_attention",
    )(q, q, q, k, k, k, v, v, v, band)


def _mem_kv_kernel(mem_ref, gmem_ref, wckv_ref, gxk_ref, k_ref, v_ref):
    hm = _rms(mem_ref[0], gmem_ref[...]).astype(_BF)
    kv = _dot(hm, wckv_ref[...])
    for hd in range(X_HEADS):
        kh = kv[:, hd * X_HEAD_DIM:(hd + 1) * X_HEAD_DIM]
        k_ref[0, :, hd * X_HEAD_DIM:(hd + 1) * X_HEAD_DIM] = _rms(kh, gxk_ref[...]).astype(_BF)
    v_ref[0] = kv[:, D_MODEL:].astype(_BF)


def _mem_kv(mem, lw):
    b, m, _ = mem.shape
    return pl.pallas_call(
        _mem_kv_kernel,
        grid=(b,),
        in_specs=[pl.BlockSpec((1, m, D_MODEL), lambda i: (i, 0, 0)),
                  _const_spec(lw['g_mem'].shape), _const_spec(lw['w_ckv'].shape), _const_spec(lw['g_xk'].shape)],
        out_specs=[pl.BlockSpec((1, m, D_MODEL), lambda i: (i, 0, 0))] * 2,
        out_shape=[jax.ShapeDtypeStruct((b, m, D_MODEL), _BF)] * 2,
        compiler_params=pltpu.CompilerParams(dimension_semantics=("parallel",), vmem_limit_bytes=VMEM_LIMIT),
        name="mem_kv",
    )(mem, lw['g_mem'], lw['w_ckv'], lw['g_xk'])


def _mix_out_kernel(x_ref, oa_ref, ob_ref, oc_ref, wo_ref, gcross_ref, wcq_ref, gxq_ref,
                    mk_ref, mv_ref, wco_ref, y_ref):
    mix = jnp.concatenate([oa_ref[p] for p in range(3)] + [ob_ref[p] for p in range(3)]
                          + [oc_ref[p] for p in range(2)], axis=1)
    x1 = x_ref[...] + _dot(mix, wo_ref[...])
    hc = _rms(x1, gcross_ref[...]).astype(_BF)
    q = _dot(hc, wcq_ref[...])
    outs = []
    for hd in range(X_HEADS):
        sl = slice(hd * X_HEAD_DIM, (hd + 1) * X_HEAD_DIM)
        qh = (_rms(q[:, sl], gxq_ref[...]) * X_HEAD_DIM ** -0.5).astype(_BF)
        s = _dot_t(qh, mk_ref[0, :, sl])
        m = jnp.max(s, axis=1, keepdims=True)
        p = jnp.exp(s - m)
        l = jnp.sum(p, axis=1, keepdims=True)
        outs.append((_dot(p.astype(_BF), mv_ref[0, :, sl]) / l).astype(_BF))
    o = jnp.concatenate(outs, axis=1)
    y_ref[...] = x1 + _dot(o, wco_ref[...])


def _mix_out(x, oa, ob, oc, mk, mv, lw, seq):
    t = x.shape[0]
    nblk_seq = seq // TM
    mem_len = mk.shape[1]

    def grp(n):
        return pl.BlockSpec((n, TM, LANES), lambda i: (0, i, 0))

    mem_spec = pl.BlockSpec((1, mem_len, D_MODEL), lambda i: (i // nblk_seq, 0, 0))
    return pl.pallas_call(
        _mix_out_kernel,
        grid=(t // TM,),
        in_specs=[pl.BlockSpec((TM, D_MODEL), lambda i: (i, 0)), grp(3), grp(3), grp(2),
                  _const_spec(lw['w_o'].shape), _const_spec(lw['g_cross'].shape),
                  _const_spec(lw['w_cq'].shape), _const_spec(lw['g_xq'].shape),
                  mem_spec, mem_spec, _const_spec(lw['w_co'].shape)],
        out_specs=pl.BlockSpec((TM, D_MODEL), lambda i: (i, 0)),
        out_shape=jax.ShapeDtypeStruct((t, D_MODEL), _F32),
        compiler_params=pltpu.CompilerParams(dimension_semantics=("parallel",), vmem_limit_bytes=VMEM_LIMIT),
        name="mix_out",
    )(x, oa, ob, oc, lw['w_o'], lw['g_cross'], lw['w_cq'], lw['g_xq'], mk, mv, lw['w_co'])


def _ffn_kernel(x_ref, xp_ref, xn_ref, gffn_ref, wup_ref, cw_ref, cb_ref, wdn_ref, y_ref, *, nblk_seq):
    i = pl.program_id(0)
    tm = x_ref.shape[0]
    x = x_ref[...]
    g = gffn_ref[...]
    is_first = (i % nblk_seq) == 0
    is_last = (i % nblk_seq) == nblk_seq - 1
    hp = jnp.where(is_first, 0.0, _rms(xp_ref[...], g))
    hn = jnp.where(is_last, 0.0, _rms(xn_ref[...], g))
    h = jnp.concatenate([hp, _rms(x, g), hn], axis=0).astype(_BF)
    acc = x
    for c in range(N_FF_CH):
        u = _dot(h, wup_ref[c])
        cw = cw_ref[c]
        up = pltpu.roll(u, 1, 0)[8:8 + tm]
        un = pltpu.roll(u, tm + 15, 0)[8:8 + tm]
        uc = up * cw[0:1] + u[8:8 + tm] * cw[1:2] + un * cw[2:3] + cb_ref[c]
        a = uc[:, :FF_CH]
        gate = uc[:, FF_CH:]
        act = (gate * (1.0 / (1.0 + jnp.exp(-gate))) * a).astype(_BF)
        acc = acc + _dot(act, wdn_ref[c])
    y_ref[...] = acc


def _ffn(x, lw, seq):
    t = x.shape[0]
    nblk_seq = seq // TM
    nhalo = t // 8
    per = TM // 8
    return pl.pallas_call(
        functools.partial(_ffn_kernel, nblk_seq=nblk_seq),
        grid=(t // TM,),
        in_specs=[pl.BlockSpec((TM, D_MODEL), lambda i: (i, 0)),
                  pl.BlockSpec((8, D_MODEL), lambda i: (jnp.maximum(i * per - 1, 0), 0)),
                  pl.BlockSpec((8, D_MODEL), lambda i: (jnp.minimum((i + 1) * per, nhalo - 1), 0)),
                  _const_spec(lw['g_ffn'].shape), _const_spec(lw['w_up'].shape),
                  _const_spec(lw['conv_w'].shape), _const_spec(lw['conv_b'].shape),
                  _const_spec(lw['w_down'].shape)],
        out_specs=pl.BlockSpec((TM, D_MODEL), lambda i: (i, 0)),
        out_shape=jax.ShapeDtypeStruct((t, D_MODEL), _F32),
        compiler_params=pltpu.CompilerParams(dimension_semantics=("parallel",), vmem_limit_bytes=VMEM_LIMIT),
        name="conv_ffn",
    )(x, x, x, lw['g_ffn'], lw['w_up'], lw['conv_w'], lw['conv_b'], lw['w_down'])


def _w_in_columns():
    cols = np.full((D_IN_P,), -1, np.int64)
    cols[0:SPLIT_CKV] = np.arange(SPLIT_CKV)
    a = SPLIT_CKV + np.arange(16)
    b = SPLIT_CKV + 16 + np.arange(16)
    cols[384:400], cols[400:416] = b, a
    cols[448:464], cols[464:480] = a, b
    d = np.arange(HEAD_DIM)

    def na_col(kind, head):
        return SPLIT_KR + (kind * NA_HEADS + head) * HEAD_DIM + d

    def dil_col(kind, head):
        return SPLIT_NA + (kind * DIL_ALL + head) * HEAD_DIM + d

    pos = COL_NA
    for pairs in ((0, 1), (2,)):
        for kind in range(3):
            for p in pairs:
                cols[pos:pos + 128] = np.concatenate([na_col(kind, 2 * p), na_col(kind, 2 * p + 1)])
                pos += 128
    assert pos == COL_DIL
    for g in range(DIL_GROUPS):
        for kind in range(3):
            for pp in range(2):
                h0, h1 = dil_col(kind, 4 * g + 2 * pp), dil_col(kind, 4 * g + 2 * pp + 1)
                if kind < 2:
                    grp = np.concatenate([h0[:32], h1[:32], h0[32:], h1[32:]])
                else:
                    grp = np.concatenate([h0, h1])
                cols[pos:pos + 128] = grp
                pos += 128
    assert pos == D_IN_P
    return cols


_W_IN_COLS = _w_in_columns()


def _pad_heads(w, real, n_heads):
    k = w.shape[0]
    w = w.reshape(k, n_heads, real)
    return jnp.pad(w, ((0, 0), (0, 0), (0, LANES - real))).reshape(k, n_heads * LANES)


def _row(v):
    return v.reshape(1, -1).astype(_F32)


def _prep_layer(p):
    src = jnp.asarray(np.maximum(_W_IN_COLS, 0))
    keep = jnp.asarray(_W_IN_COLS >= 0)
    w_in = jnp.where(keep[None, :], p['w_in'][:, src], 0.0).astype(_BF)
    kn, qn = p['mla_kn'], p['mla_qn']
    zeros32 = jnp.zeros((32,), _F32)
    g_kr = jnp.concatenate([kn[80:96], kn[64:80], zeros32, kn[64:80], kn[80:96], zeros32])
    dq, dk = p['dil_qn'], p['dil_kn']
    w_up = p['w_up'].reshape(D_MODEL, 2, N_FF_CH, FF_CH).transpose(2, 0, 1, 3).reshape(N_FF_CH, D_MODEL, 2 * FF_CH)
    conv_w = p['conv_w'].reshape(CONV_W, 2, N_FF_CH, FF_CH).transpose(2, 0, 1, 3).reshape(N_FF_CH, CONV_W, 2 * FF_CH)
    conv_b = p['conv_b'].reshape(2, N_FF_CH, FF_CH).transpose(1, 0, 2).reshape(N_FF_CH, 1, 2 * FF_CH)
    return {
        'g_mix': _row(p['norm_mix']), 'w_in': w_in,
        'g_cq': _row(p['mla_q_norm']), 'g_ckv': _row(p['mla_kv_norm']),
        'w_uq': _pad_heads(p['w_uq'], MLA_QK, MLA_HEADS).astype(_BF),
        'w_uk': _pad_heads(p['w_uk'], MLA_NOPE, MLA_HEADS).astype(_BF),
        'w_uv': _pad_heads(p['w_uv'], MLA_V, MLA_HEADS).astype(_BF),
        'g_mq': _row(jnp.concatenate([qn, zeros32])),
        'g_mk': _row(jnp.concatenate([kn[:MLA_NOPE], jnp.zeros((64,), _F32)])),
        'g_kr': _row(g_kr),
        'g_nq': _row(jnp.tile(p['na_qn'], 2)), 'g_nk': _row(jnp.tile(p['na_kn'], 2)),
        'g_dq': _row(jnp.concatenate([dq[:32], dq[:32], dq[32:], dq[32:]])),
        'g_dk': _row(jnp.concatenate([dk[:32], dk[:32], dk[32:], dk[32:]])),
        'w_o': p['w_o'].astype(_BF),
        'g_cross': _row(p['norm_cross']), 'w_cq': p['w_cq'].astype(_BF),
        'g_xq': _row(p['x_qn']), 'g_xk': _row(p['x_kn']), 'g_mem': _row(p['norm_mem']),
        'w_ckv': p['w_ckv'].astype(_BF), 'w_co': p['w_co'].astype(_BF),
        'g_ffn': _row(p['norm_ffn']), 'w_up': w_up.astype(_BF),
        'conv_w': conv_w.astype(_F32), 'conv_b': conv_b.astype(_F32),
        'w_down': p['w_down'].reshape(N_FF_CH, FF_CH, D_MODEL).astype(_BF),
        'na_bias': _na_bias(p['na_rpb']),
    }


def _na_bias(rpb):
    dl = np.arange(NA_WIN_R)
    j = np.arange(NA_WIN_R)
    r_off = j[None, :] - dl[:, None] + (NA_WIN_R - 1)
    qc = np.arange(GRID_W)
    kc = np.arange(GRID_W)
    c_start = np.clip(qc - NA_WIN_C // 2, 0, GRID_W - NA_WIN_C)
    valid = (kc[None, :] >= c_start[:, None]) & (kc[None, :] < c_start[:, None] + NA_WIN_C)
    c_off = np.clip(kc[None, :] - qc[:, None] + (NA_WIN_C - 1), 0, 2 * NA_WIN_C - 2)
    b = rpb[:, r_off]
    b = b[:, :, :, c_off]
    b = jnp.where(jnp.asarray(valid)[None, None, None], b, NEG)
    b = b.transpose(0, 1, 3, 2, 4).reshape(NA_HEADS // 2, 2, NA_WIN_R, GRID_W, NA_WIN_R * GRID_W)
    b = b.transpose(0, 2, 1, 3, 4).astype(_F32)
    return b.reshape(NA_HEADS // 2, NA_WIN_R, 2 * GRID_W, NA_WIN_R * GRID_W)


def _band_bias():
    i = np.arange(DIL_TQ)[:, None]
    c = np.arange(2 * DIL_TQ)[None, :]
    band = (c >= i) & (c <= i + DIL_TQ)
    out = []
    for edge in range(4):
        ok = band
        if edge & 1:
            ok = ok & (c >= DIL_TQ // 2)
        if edge & 2:
            ok = ok & (c < 2 * DIL_TQ - DIL_TQ // 2)
        out.append(np.tile(np.where(ok, 0.0, NEG), (2, 1)))
    return jnp.asarray(np.stack(out), _F32)


def _rope_tables(seq):
    pos = jnp.arange(seq, dtype=_F32)[:, None]

    def cs(half):
        inv = ROPE_THETA ** (-jnp.arange(half, dtype=_F32) / half)
        ang = pos * inv[None, :]
        return jnp.cos(ang), jnp.sin(ang)

    c16, s16 = cs(MLA_ROPE // 2)
    c32, s32 = cs(HEAD_DIM // 2)
    one = jnp.ones((seq, 64), _F32)
    z16 = jnp.zeros((seq, 16), _F32)
    z32 = jnp.zeros((seq, 32), _F32)
    z64 = jnp.zeros((seq, 64), _F32)
    return {
        'q_c': jnp.concatenate([one, c16, c16, z32], axis=1),
        'q_s1': jnp.concatenate([z64, z16, s16, z32], axis=1),
        'q_s2': jnp.concatenate([z64, -s16, z16, z32], axis=1),
        'k_c': jnp.concatenate([z64, c16, c16, z32], axis=1),
        'k_s': jnp.concatenate([z64, -s16, s16, z32], axis=1),
        'd_c': jnp.concatenate([c32, c32, c32, c32], axis=1),
        'd_s': jnp.concatenate([-s32, -s32, s32, s32], axis=1),
    }


def _trunk_layer(x, mem, lw, tabs, band, batch, seq):
    qm, km, vm, qn, kn, vn, qd, kd, vd = _mix_in(x, lw, tabs, seq)
    oa = _mla(qm, km, vm, batch, seq)
    ob = _na(qn, kn, vn, lw['na_bias'], batch, seq)
    oc = _dil(qd, kd, vd, band, batch, seq)
    mk, mv = _mem_kv(mem, lw)
    x = _mix_out(x, oa, ob, oc, mk, mv, lw, seq)
    return _ffn(x, lw, seq)


def kernel(x_prompt, x_sample, mem_prompt, mem_sample, norm_mix, w_in, mla_q_norm, mla_kv_norm, w_uq, w_uk, w_uv, mla_qn, mla_kn, na_qn, na_kn, na_rpb, dil_qn, dil_kn, w_o, norm_cross, norm_mem, w_cq, w_ckv, x_qn, x_kn, w_co, norm_ffn, w_up, conv_w, conv_b, w_down):
    stacked = {
        'norm_mix': norm_mix, 'w_in': w_in, 'mla_q_norm': mla_q_norm, 'mla_kv_norm': mla_kv_norm,
        'w_uq': w_uq, 'w_uk': w_uk, 'w_uv': w_uv, 'mla_qn': mla_qn, 'mla_kn': mla_kn,
        'na_qn': na_qn, 'na_kn': na_kn, 'na_rpb': na_rpb, 'dil_qn': dil_qn, 'dil_kn': dil_kn,
        'w_o': w_o, 'norm_cross': norm_cross, 'norm_mem': norm_mem, 'w_cq': w_cq, 'w_ckv': w_ckv,
        'x_qn': x_qn, 'x_kn': x_kn, 'w_co': w_co, 'norm_ffn': norm_ffn, 'w_up': w_up,
        'conv_w': conv_w, 'conv_b': conv_b, 'w_down': w_down,
    }
    layers = [_prep_layer({k: v[l] for k, v in stacked.items()}) for l in range(DEPTH)]
    band = _band_bias()
    outs = []
    for x, mem in ((x_prompt, mem_prompt), (x_sample, mem_sample)):
        batch, seq, _ = x.shape
        tabs = _rope_tables(seq)
        xt = x.reshape(batch * seq, D_MODEL)
        for lw in layers:
            xt = _trunk_layer(xt, mem, lw, tabs, band, batch, seq)
        outs.append(xt.reshape(batch, seq, D_MODEL))
    return tuple(outs)
```

```python
import functools

import jax
import jax.numpy as jnp
import numpy as np
from jax import lax
from jax.experimental import pallas as pl
from jax.experimental.pallas import tpu as pltpu

D_MODEL = 1024
DEPTH = 4
GRID_W = 64
HEAD_DIM = 64
ROPE_THETA = 10000.0
EPS = 1e-6
MLA_HEADS = 6
MLA_Q_RANK = 256
MLA_KV_RANK = 128
MLA_NOPE = 64
MLA_ROPE = 32
MLA_V = 64
MLA_QK = MLA_NOPE + MLA_ROPE
NA_HEADS = 6
NA_WIN_R = 8
NA_WIN_C = 16
DIL_WINDOWS = (128, 512, 2048)
DIL_RATES = (1, 4, 16)
DIL_GROUPS = 3
DIL_HEADS = 4
DIL_ALL = DIL_GROUPS * DIL_HEADS
X_HEADS = 4
X_HEAD_DIM = D_MODEL // X_HEADS
D_FF = 2816
CONV_W = 3
SPLIT_CQ = MLA_Q_RANK
SPLIT_CKV = SPLIT_CQ + MLA_KV_RANK
SPLIT_KR = SPLIT_CKV + MLA_ROPE
SPLIT_NA = SPLIT_KR + 3 * NA_HEADS * HEAD_DIM
D_IN = SPLIT_NA + 3 * DIL_ALL * HEAD_DIM

LANES = 128
NEG = -1e30
LOG2E = 1.4426950408889634
TM = 512
DIL_TQ = 128
DIL_SB = DIL_TQ * max(DIL_RATES)
DIL_UNROLL = 4
NA_RB = 8
FF_CH = 256
N_FF_CH = D_FF // FF_CH
VMEM_LIMIT = 56 * 1024 * 1024

COL_MLA = 0
COL_NA = 512
COL_DIL = COL_NA + 3 * NA_HEADS * HEAD_DIM
D_IN_P = COL_DIL + 3 * DIL_ALL * HEAD_DIM

_BF = jnp.bfloat16
_F32 = jnp.float32


def _dot(a, b):
    return jnp.dot(a, b, preferred_element_type=_F32)


def _dot_t(a, b):
    return lax.dot_general(a, b, (((1,), (1,)), ((), ())), preferred_element_type=_F32)


def _rms(x, g):
    ms = jnp.mean(x * x, axis=-1, keepdims=True)
    return x * lax.rsqrt(ms + EPS) * g


def _lane(shape):
    return lax.broadcasted_iota(jnp.int32, shape, len(shape) - 1)


def _head_join_matrices():
    i = np.arange(2 * LANES)
    same_group = (i[:, None] // LANES) == (i[None, :] // LANES)
    natural = (i[:, None] // HEAD_DIM) == (i[None, :] // HEAD_DIM)
    half = ((i % HEAD_DIM) < HEAD_DIM // 2)
    rotary = same_group & (half[:, None] == half[None, :])
    return [jnp.asarray(m, _BF) for m in (same_group, natural, rotary)]


def _seg_inv(x2, bd_ref, dim):
    ss = _dot((x2 * x2).astype(_BF), bd_ref[...])
    return lax.rsqrt(ss * (1.0 / dim) + EPS)


def _mix_in_kernel(x_ref, gmix_ref, win_ref, gcq_ref, gckv_ref, wuq_ref, wuk_ref, wuv_ref,
                   gmq_ref, gmk_ref, gkr_ref, gnq_ref, gnk_ref, gdq_ref, gdk_ref,
                   bd128_ref, bd64_ref, bdr_ref,
                   tqc_ref, tqs1_ref, tqs2_ref, tkc_ref, tks_ref, tdc_ref, tds_ref,
                   qm_ref, km_ref, vm_ref, qn_ref, kn_ref, vn_ref, qd_ref, kd_ref, vd_ref,
                   perm_ref):
    tm = x_ref.shape[0]
    h = _rms(x_ref[...], gmix_ref[...]).astype(_BF)
    lane = _lane((tm, LANES))

    z = _dot(h, win_ref[:, COL_MLA:COL_NA])
    cq = _rms(z[:, 0:MLA_Q_RANK], gcq_ref[...]).astype(_BF)
    ckv = _rms(z[:, MLA_Q_RANK:MLA_Q_RANK + MLA_KV_RANK], gckv_ref[...]).astype(_BF)
    kr = z[:, 384:512]
    kr_own = jnp.where((lane >= MLA_NOPE) & (lane < MLA_QK), kr, 0.0)
    krg = kr * gkr_ref[...]
    kr_rot = krg * tkc_ref[...] + pltpu.roll(krg, 64, 1) * tks_ref[...]
    qpre = _dot(cq, wuq_ref[...])
    kpre = _dot(ckv, wuk_ref[...])
    vpre = _dot(ckv, wuv_ref[...])
    for c in range(MLA_HEADS // 2):
        q2 = qpre[:, 2 * c * LANES:(2 * c + 2) * LANES]
        k2 = kpre[:, 2 * c * LANES:(2 * c + 2) * LANES]
        inv_q = _seg_inv(q2, bd128_ref, MLA_QK)
        inv_k = _seg_inv(k2 + jnp.concatenate([kr_own, kr_own], axis=1), bd128_ref, MLA_QK)
        for hh in range(2):
            hd = 2 * c + hh
            sl = slice(hh * LANES, (hh + 1) * LANES)
            qg = q2[:, sl] * gmq_ref[...]
            qr = qg * tqc_ref[...] + pltpu.roll(qg, 16, 1) * tqs1_ref[...] + pltpu.roll(qg, 112, 1) * tqs2_ref[...]
            qm_ref[hd] = (qr * inv_q[:, sl]).astype(_BF)
            km_ref[hd] = ((k2[:, sl] * gmk_ref[...] + kr_rot) * inv_k[:, sl]).astype(_BF)
            vm_ref[hd] = jnp.where(lane == MLA_V, 1.0, vpre[:, hd * LANES:(hd + 1) * LANES]).astype(_BF)

    z = _dot(h, win_ref[:, COL_NA:COL_DIL])
    q2 = z[:, 0:256]
    k2 = z[:, 256:512]
    qk2 = z[:, 768:1024]
    qn2 = q2 * _seg_inv(q2, bd64_ref, HEAD_DIM)
    kn2 = k2 * _seg_inv(k2, bd64_ref, HEAD_DIM)
    qkn2 = qk2 * _seg_inv(qk2, bd64_ref, HEAD_DIM)
    for p in range(2):
        sl = slice(p * LANES, (p + 1) * LANES)
        qn_ref[p] = (qn2[:, sl] * gnq_ref[...]).astype(_BF)
        kn_ref[p] = (kn2[:, sl] * gnk_ref[...]).astype(_BF)
        vn_ref[p] = z[:, 512 + p * LANES: 512 + (p + 1) * LANES].astype(_BF)
    qn_ref[2] = (qkn2[:, :LANES] * gnq_ref[...]).astype(_BF)
    kn_ref[2] = (qkn2[:, LANES:] * gnk_ref[...]).astype(_BF)
    vn_ref[2] = z[:, 1024:1152].astype(_BF)

    def put(out_ref, p, val, rate):
        if rate == 1:
            out_ref[p] = val.astype(_BF)
        else:
            rows = tm // rate
            perm_ref[...] = val
            for rho in range(rate):
                out_ref[p, rho * rows:(rho + 1) * rows, :] = perm_ref[pl.ds(rho, rows, stride=rate), :].astype(_BF)

    for g in range(DIL_GROUPS):
        z = _dot(h, win_ref[:, COL_DIL + g * 768: COL_DIL + (g + 1) * 768])
        q2 = z[:, 0:256]
        k2 = z[:, 256:512]
        qn2 = q2 * _seg_inv(q2, bdr_ref, HEAD_DIM)
        kn2 = k2 * _seg_inv(k2, bdr_ref, HEAD_DIM)
        for pp in range(2):
            p = 2 * g + pp
            sl = slice(pp * LANES, (pp + 1) * LANES)
            qx = qn2[:, sl] * gdq_ref[...]
            qx = qx * tdc_ref[...] + pltpu.roll(qx, 64, 1) * tds_ref[...]
            kx = kn2[:, sl] * gdk_ref[...]
            kx = kx * tdc_ref[...] + pltpu.roll(kx, 64, 1) * tds_ref[...]
            put(qd_ref, p, qx, DIL_RATES[g])
            put(kd_ref, p, kx, DIL_RATES[g])
            put(vd_ref, p, z[:, 512 + pp * LANES: 512 + (pp + 1) * LANES], DIL_RATES[g])


def _const_spec(shape):
    nd = len(shape)
    return pl.BlockSpec(shape, lambda i, _nd=nd: (0,) * _nd)


def _mix_in(x, lw, tabs, seq):
    t = x.shape[0]
    nblk_seq = seq // TM
    tab_spec = pl.BlockSpec((TM, LANES), lambda i: (i % nblk_seq, 0))
    weights = [lw['g_mix'], lw['w_in'], lw['g_cq'], lw['g_ckv'], lw['w_uq'], lw['w_uk'], lw['w_uv'],
               lw['g_mq'], lw['g_mk'], lw['g_kr'], lw['g_nq'], lw['g_nk'], lw['g_dq'], lw['g_dk'],
               *_head_join_matrices()]
    tables = [tabs['q_c'], tabs['q_s1'], tabs['q_s2'], tabs['k_c'], tabs['k_s'], tabs['d_c'], tabs['d_s']]

    def out(n):
        return (jax.ShapeDtypeStruct((n, t, LANES), _BF), pl.BlockSpec((n, TM, LANES), lambda i: (0, i, 0)))

    outs = [out(6), out(6), out(6), out(3), out(3), out(3), out(6), out(6), out(6)]
    return pl.pallas_call(
        _mix_in_kernel,
        grid=(t // TM,),
        in_specs=[pl.BlockSpec((TM, D_MODEL), lambda i: (i, 0))]
        + [_const_spec(w.shape) for w in weights] + [tab_spec] * len(tables),
        out_specs=[o[1] for o in outs],
        out_shape=[o[0] for o in outs],
        scratch_shapes=[pltpu.VMEM((TM, LANES), _F32)],
        compiler_params=pltpu.CompilerParams(dimension_semantics=("parallel",), vmem_limit_bytes=VMEM_LIMIT),
        name="mix_in",
    )(x, *weights, *tables)


def _mla_kernel(q_ref, k_ref, v_ref, o_ref, m_sc, acc_sc, *, tk):
    tq = q_ref.shape[1]
    nk = k_ref.shape[1] // tk
    m_sc[...] = jnp.full(m_sc.shape, NEG, _F32)
    acc_sc[...] = jnp.zeros(acc_sc.shape, _F32)

    def body(j, carry):
        rows = pl.ds(pl.multiple_of(j * tk, tk), tk)
        scores = [_dot_t(q_ref[hh], k_ref[hh, rows, :]) for hh in range(2)]
        for hh, s in enumerate(scores):
            m_prev = m_sc[hh]
            m_new = jnp.maximum(m_prev, jnp.max(s, axis=1, keepdims=True))
            alpha = jnp.exp2(m_prev - m_new)
            p = jnp.exp2(s - jnp.tile(m_new, (1, tk // LANES)))
            acc_sc[hh] = alpha * acc_sc[hh] + _dot(p.astype(_BF), v_ref[hh, rows, :])
            m_sc[hh] = m_new
        return carry

    lax.fori_loop(0, nk, body, 0)
    lane = _lane((tq, LANES))
    o0 = acc_sc[0] / acc_sc[0][:, MLA_V:MLA_V + 1]
    o1 = acc_sc[1] / acc_sc[1][:, MLA_V:MLA_V + 1]
    o_ref[0] = jnp.where(lane < MLA_V, o0, pltpu.roll(o1, MLA_V, 1)).astype(o_ref.dtype)


def _mla(q, k, v, batch, seq, tq=512, tk=1024):
    t = q.shape[1]
    nq = seq // tq
    return pl.pallas_call(
        functools.partial(_mla_kernel, tk=tk),
        grid=(batch, MLA_HEADS // 2, nq),
        in_specs=[pl.BlockSpec((2, tq, LANES), lambda b, p, i: (p, b * nq + i, 0)),
                  pl.BlockSpec((2, seq, LANES), lambda b, p, i: (p, b, 0)),
                  pl.BlockSpec((2, seq, LANES), lambda b, p, i: (p, b, 0))],
        out_specs=pl.BlockSpec((1, tq, LANES), lambda b, p, i: (p, b * nq + i, 0)),
        out_shape=jax.ShapeDtypeStruct((MLA_HEADS // 2, t, LANES), _BF),
        scratch_shapes=[pltpu.VMEM((2, tq, LANES), _F32), pltpu.VMEM((2, tq, LANES), _F32)],
        compiler_params=pltpu.CompilerParams(
            dimension_semantics=("parallel", "parallel", "arbitrary"), vmem_limit_bytes=VMEM_LIMIT),
        name="mla_attention",
    )(q, k, v)


def _na_kernel(q_ref, k_ref, v_ref, bias_ref, o_ref, *, rows):
    i = pl.program_id(2)
    lane = _lane((GRID_W, LANES))
    first = lane < HEAD_DIM
    nkeys = NA_WIN_R * GRID_W

    scores, windows = [], []
    for rr in range(NA_RB):
        r = i * NA_RB + rr
        rs = jnp.clip(r - NA_WIN_R // 2, 0, rows - NA_WIN_R)
        start = pl.multiple_of(rs * GRID_W, GRID_W)
        q = q_ref[0, rr * GRID_W:(rr + 1) * GRID_W, :]
        zero = jnp.zeros_like(q)
        q2 = jnp.concatenate([jnp.where(first, q, zero), jnp.where(first, zero, q)], axis=0)
        scores.append(_dot_t(q2, k_ref[0, pl.ds(start, nkeys), :]) + bias_ref[0, r - rs])
        windows.append(start)
    probs = []
    for s in scores:
        p = jnp.exp(s - jnp.max(s, axis=1, keepdims=True))
        probs.append((p.astype(_BF), jnp.sum(p, axis=1, keepdims=True)))
    for rr, ((p, l), start) in enumerate(zip(probs, windows)):
        o2 = _dot(p, v_ref[0, pl.ds(start, nkeys), :]) / l
        o = jnp.where(first, o2[:GRID_W], o2[GRID_W:])
        o_ref[0, rr * GRID_W:(rr + 1) * GRID_W, :] = o.astype(o_ref.dtype)


def _na(q, k, v, bias, batch, seq):
    t = q.shape[1]
    rows = seq // GRID_W
    assert rows >= NA_WIN_R and rows % NA_RB == 0
    nrb = rows // NA_RB
    qrows = NA_RB * GRID_W
    return pl.pallas_call(
        functools.partial(_na_kernel, rows=rows),
        grid=(batch, NA_HEADS // 2, nrb),
        in_specs=[pl.BlockSpec((1, qrows, LANES), lambda b, p, i: (p, b * nrb + i, 0)),
                  pl.BlockSpec((1, seq, LANES), lambda b, p, i: (p, b, 0)),
                  pl.BlockSpec((1, seq, LANES), lambda b, p, i: (p, b, 0)),
                  pl.BlockSpec((1, NA_WIN_R, 2 * GRID_W, NA_WIN_R * GRID_W), lambda b, p, i: (p, 0, 0, 0))],
        out_specs=pl.BlockSpec((1, qrows, LANES), lambda b, p, i: (p, b * nrb + i, 0)),
        out_shape=jax.ShapeDtypeStruct((NA_HEADS // 2, t, LANES), _BF),
        compiler_params=pltpu.CompilerParams(
            dimension_semantics=("parallel", "parallel", "arbitrary"), vmem_limit_bytes=VMEM_LIMIT),
        name="na_attention",
    )(q, k, v, bias)


def _dil_kernel(q0_ref, q1_ref, q2_ref, k0_ref, k1_ref, k2_ref, v0_ref, v1_ref, v2_ref, band_ref,
                o_ref, og_sc, lse_sc, *, seq):
    sb = pl.program_id(2)
    q_refs = (q0_ref, q1_ref, q2_ref)
    k_refs = (k0_ref, k1_ref, k2_ref)
    v_refs = (v0_ref, v1_ref, v2_ref)
    lane = _lane((DIL_TQ, LANES))
    first_qk = (lane % HEAD_DIM) < (HEAD_DIM // 2)
    first_v = lane < HEAD_DIM
    n_tiles = DIL_SB // DIL_TQ
    half = DIL_TQ // 2

    for g in range(DIL_GROUPS):
        rate = DIL_RATES[g]
        length = seq // rate
        per_blk = TM // rate
        chunk = min(half, per_blk)
        n_chunks = (2 * DIL_TQ) // chunk
        tiles_per_res = n_tiles // rate
        q_ref, k_ref, v_ref = q_refs[g], k_refs[g], v_refs[g]

        def row_of(m, rho, _per_blk=per_blk):
            return (m // _per_blk) * TM + rho * _per_blk + (m % _per_blk)

        def body(tg, carry, _g=g, _rate=rate, _length=length, _per_blk=per_blk, _chunk=chunk,
                 _n_chunks=n_chunks, _tpr=tiles_per_res, _q=q_ref, _k=k_ref, _v=v_ref):
            scores, metas = [], []
            for u in range(DIL_UNROLL):
                tt = tg * DIL_UNROLL + u
                rho = tt // _tpr
                mt = tt % _tpr
                m0 = sb * (DIL_SB // _rate) + mt * DIL_TQ
                q = jnp.concatenate(
                    [_q[0, pl.ds(pl.multiple_of(row_of(m0 + c * _chunk, rho), _chunk), _chunk), :]
                     for c in range(DIL_TQ // _chunk)], axis=0)
                zero = jnp.zeros_like(q)
                q2 = jnp.concatenate([jnp.where(first_qk, q, zero), jnp.where(first_qk, zero, q)], axis=0)
                win = []
                for c in range(_n_chunks):
                    ms = jnp.clip(m0 - half + c * _chunk, 0, _length - _chunk)
                    win.append(pl.multiple_of(row_of(ms, rho), _chunk))
                kw = jnp.concatenate([_k[0, pl.ds(row, _chunk), :] for row in win], axis=0)
                edge = (m0 == 0).astype(jnp.int32) + 2 * (m0 == _length - DIL_TQ).astype(jnp.int32)
                scores.append(_dot_t(q2, kw) + band_ref[edge])
                metas.append((rho + _rate * mt * DIL_TQ, win))
            probs = []
            for s in scores:
                m = jnp.max(s, axis=1, keepdims=True)
                p = jnp.exp(s - m)
                probs.append((p.astype(_BF), m, jnp.sum(p, axis=1, keepdims=True)))
            for (p, m, l), (base, win) in zip(probs, metas):
                vw = jnp.concatenate([_v[0, pl.ds(row, _chunk), :] for row in win], axis=0)
                o2 = _dot(p, vw) / l
                lse2 = jnp.broadcast_to(m + jnp.log(l), (2 * DIL_TQ, LANES))
                if _rate == 1:
                    dst = pl.ds(pl.multiple_of(base, DIL_TQ), DIL_TQ)
                else:
                    dst = pl.ds(base, DIL_TQ, stride=_rate)
                og_sc[_g, dst, :] = jnp.where(first_v, o2[:DIL_TQ], o2[DIL_TQ:])
                lse_sc[_g, dst, :] = jnp.where(first_v, lse2[:DIL_TQ], lse2[DIL_TQ:])
            return carry

        lax.fori_loop(0, n_tiles // DIL_UNROLL, body, 0)

    m = jnp.maximum(jnp.maximum(lse_sc[0], lse_sc[1]), lse_sc[2])
    w0 = jnp.exp(lse_sc[0] - m)
    w1 = jnp.exp(lse_sc[1] - m)
    w2 = jnp.exp(lse_sc[2] - m)
    o = (w0 * og_sc[0] + w1 * og_sc[1] + w2 * og_sc[2]) / (w0 + w1 + w2)
    o_ref[0] = o.astype(o_ref.dtype)


def _dil(q, k, v, band, batch, seq):
    t = q.shape[1]
    assert seq % DIL_SB == 0
    nsb = seq // DIL_SB

    def q_spec(g):
        return pl.BlockSpec((1, DIL_SB, LANES), lambda b, sp, s, _g=g: (2 * _g + sp, b * nsb + s, 0))

    def kv_spec(g):
        return pl.BlockSpec((1, seq, LANES), lambda b, sp, s, _g=g: (2 * _g + sp, b, 0))

    return pl.pallas_call(
        functools.partial(_dil_kernel, seq=seq),
        grid=(batch, 2, nsb),
        in_specs=[kv_spec(0), kv_spec(1), kv_spec(2), kv_spec(0), kv_spec(1), kv_spec(2),
                  kv_spec(0), kv_spec(1), kv_spec(2),
                  pl.BlockSpec((4, 2 * DIL_TQ, 2 * DIL_TQ), lambda b, sp, s: (0, 0, 0))],
        out_specs=pl.BlockSpec((1, DIL_SB, LANES), lambda b, sp, s: (sp, b * nsb + s, 0)),
        out_shape=jax.ShapeDtypeStruct((2, t, LANES), _BF),
        scratch_shapes=[pltpu.VMEM((DIL_GROUPS, DIL_SB, LANES), _F32),
                        pltpu.VMEM((DIL_GROUPS, DIL_SB, LANES), _F32)],
        compiler_params=pltpu.CompilerParams(
            dimension_semantics=("parallel", "parallel", "arbitrary"), vmem_limit_bytes=VMEM_LIMIT),
        name="dil_attention",
    )(q, q, q, k, k, k, v, v, v, band)


def _mem_kv_kernel(mem_ref, gmem_ref, wckv_ref, gxk_ref, k_ref, v_ref):
    hm = _rms(mem_ref[0], gmem_ref[...]).astype(_BF)
    kv = _dot(hm, wckv_ref[...])
    for hd in range(X_HEADS):
        kh = kv[:, hd * X_HEAD_DIM:(hd + 1) * X_HEAD_DIM]
        k_ref[0, :, hd * X_HEAD_DIM:(hd + 1) * X_HEAD_DIM] = _rms(kh, gxk_ref[...]).astype(_BF)
    v_ref[0] = kv[:, D_MODEL:].astype(_BF)


def _mem_kv(mem, lw):
    b, m, _ = mem.shape
    return pl.pallas_call(
        _mem_kv_kernel,
        grid=(b,),
        in_specs=[pl.BlockSpec((1, m, D_MODEL), lambda i: (i, 0, 0)),
                  _const_spec(lw['g_mem'].shape), _const_spec(lw['w_ckv'].shape), _const_spec(lw['g_xk'].shape)],
        out_specs=[pl.BlockSpec((1, m, D_MODEL), lambda i: (i, 0, 0))] * 2,
        out_shape=[jax.ShapeDtypeStruct((b, m, D_MODEL), _BF)] * 2,
        compiler_params=pltpu.CompilerParams(dimension_semantics=("parallel",), vmem_limit_bytes=VMEM_LIMIT),
        name="mem_kv",
    )(mem, lw['g_mem'], lw['w_ckv'], lw['g_xk'])


def _mix_out_kernel(x_ref, oa_ref, ob_ref, oc_ref, wo_ref, gcross_ref, wcq_ref, gxq_ref,
                    mk_ref, mv_ref, wco_ref, y_ref):
    mix = jnp.concatenate([oa_ref[p] for p in range(3)] + [ob_ref[p] for p in range(3)]
                          + [oc_ref[p] for p in range(2)], axis=1)
    x1 = x_ref[...] + _dot(mix, wo_ref[...])
    hc = _rms(x1, gcross_ref[...]).astype(_BF)
    q = _dot(hc, wcq_ref[...])
    outs = []
    for hd in range(X_HEADS):
        sl = slice(hd * X_HEAD_DIM, (hd + 1) * X_HEAD_DIM)
        qh = (_rms(q[:, sl], gxq_ref[...]) * X_HEAD_DIM ** -0.5).astype(_BF)
        s = _dot_t(qh, mk_ref[0, :, sl])
        m = jnp.max(s, axis=1, keepdims=True)
        p = jnp.exp(s - m)
        l = jnp.sum(p, axis=1, keepdims=True)
        outs.append((_dot(p.astype(_BF), mv_ref[0, :, sl]) / l).astype(_BF))
    o = jnp.concatenate(outs, axis=1)
    y_ref[...] = x1 + _dot(o, wco_ref[...])


def _mix_out(x, oa, ob, oc, mk, mv, lw, seq):
    t = x.shape[0]
    nblk_seq = seq // TM
    mem_len = mk.shape[1]

    def grp(n):
        return pl.BlockSpec((n, TM, LANES), lambda i: (0, i, 0))

    mem_spec = pl.BlockSpec((1, mem_len, D_MODEL), lambda i: (i // nblk_seq, 0, 0))
    return pl.pallas_call(
        _mix_out_kernel,
        grid=(t // TM,),
        in_specs=[pl.BlockSpec((TM, D_MODEL), lambda i: (i, 0)), grp(3), grp(3), grp(2),
                  _const_spec(lw['w_o'].shape), _const_spec(lw['g_cross'].shape),
                  _const_spec(lw['w_cq'].shape), _const_spec(lw['g_xq'].shape),
                  mem_spec, mem_spec, _const_spec(lw['w_co'].shape)],
        out_specs=pl.BlockSpec((TM, D_MODEL), lambda i: (i, 0)),
        out_shape=jax.ShapeDtypeStruct((t, D_MODEL), _F32),
        compiler_params=pltpu.CompilerParams(dimension_semantics=("parallel",), vmem_limit_bytes=VMEM_LIMIT),
        name="mix_out",
    )(x, oa, ob, oc, lw['w_o'], lw['g_cross'], lw['w_cq'], lw['g_xq'], mk, mv, lw['w_co'])


def _ffn_kernel(x_ref, xp_ref, xn_ref, gffn_ref, wup_ref, cw_ref, cb_ref, wdn_ref, y_ref, *, nblk_seq):
    i = pl.program_id(0)
    tm = x_ref.shape[0]
    x = x_ref[...]
    g = gffn_ref[...]
    is_first = (i % nblk_seq) == 0
    is_last = (i % nblk_seq) == nblk_seq - 1
    hp = jnp.where(is_first, 0.0, _rms(xp_ref[...], g))
    hn = jnp.where(is_last, 0.0, _rms(xn_ref[...], g))
    h = jnp.concatenate([hp, _rms(x, g), hn], axis=0).astype(_BF)
    acc = x
    pending = []
    u_next = _dot(h, wup_ref[0])
    for c in range(N_FF_CH):
        u = u_next
        if c + 1 < N_FF_CH:
            u_next = _dot(h, wup_ref[c + 1])
        cw = cw_ref[c]
        up = pltpu.roll(u, 1, 0)[8:8 + tm]
        un = pltpu.roll(u, tm + 15, 0)[8:8 + tm]
        uc = up * cw[0:1] + u[8:8 + tm] * cw[1:2] + un * cw[2:3] + cb_ref[c]
        a = uc[:, :FF_CH]
        half_gate = 0.5 * uc[:, FF_CH:]
        pending.append((half_gate * (1.0 + jnp.tanh(half_gate)) * a).astype(_BF))
        rest = N_FF_CH - 1 - c
        if len(pending) == 2 and rest != 1 or rest == 0:
            lo = (c + 1 - len(pending)) * FF_CH
            act = pending[0] if len(pending) == 1 else jnp.concatenate(pending, axis=1)
            acc = acc + _dot(act, wdn_ref[lo:(c + 1) * FF_CH, :])
            pending = []
    y_ref[...] = acc


def _ffn(x, lw, seq):
    t = x.shape[0]
    nblk_seq = seq // TM
    nhalo = t // 8
    per = TM // 8
    return pl.pallas_call(
        functools.partial(_ffn_kernel, nblk_seq=nblk_seq),
        grid=(t // TM,),
        in_specs=[pl.BlockSpec((TM, D_MODEL), lambda i: (i, 0)),
                  pl.BlockSpec((8, D_MODEL), lambda i: (jnp.maximum(i * per - 1, 0), 0)),
                  pl.BlockSpec((8, D_MODEL), lambda i: (jnp.minimum((i + 1) * per, nhalo - 1), 0)),
                  _const_spec(lw['g_ffn'].shape), _const_spec(lw['w_up'].shape),
                  _const_spec(lw['conv_w'].shape), _const_spec(lw['conv_b'].shape),
                  _const_spec(lw['w_down'].shape)],
        out_specs=pl.BlockSpec((TM, D_MODEL), lambda i: (i, 0)),
        out_shape=jax.ShapeDtypeStruct((t, D_MODEL), _F32),
        compiler_params=pltpu.CompilerParams(dimension_semantics=("parallel",), vmem_limit_bytes=VMEM_LIMIT),
        name="conv_ffn",
    )(x, x, x, lw['g_ffn'], lw['w_up'], lw['conv_w'], lw['conv_b'], lw['w_down'])


def _w_in_columns():
    cols = np.full((D_IN_P,), -1, np.int64)
    cols[0:SPLIT_CKV] = np.arange(SPLIT_CKV)
    a = SPLIT_CKV + np.arange(16)
    b = SPLIT_CKV + 16 + np.arange(16)
    cols[384:400], cols[400:416] = b, a
    cols[448:464], cols[464:480] = a, b
    d = np.arange(HEAD_DIM)

    def na_col(kind, head):
        return SPLIT_KR + (kind * NA_HEADS + head) * HEAD_DIM + d

    def dil_col(kind, head):
        return SPLIT_NA + (kind * DIL_ALL + head) * HEAD_DIM + d

    pos = COL_NA
    for pairs in ((0, 1), (2,)):
        for kind in range(3):
            for p in pairs:
                cols[pos:pos + 128] = np.concatenate([na_col(kind, 2 * p), na_col(kind, 2 * p + 1)])
                pos += 128
    assert pos == COL_DIL
    for g in range(DIL_GROUPS):
        for kind in range(3):
            for pp in range(2):
                h0, h1 = dil_col(kind, 4 * g + 2 * pp), dil_col(kind, 4 * g + 2 * pp + 1)
                if kind < 2:
                    grp = np.concatenate([h0[:32], h1[:32], h0[32:], h1[32:]])
                else:
                    grp = np.concatenate([h0, h1])
                cols[pos:pos + 128] = grp
                pos += 128
    assert pos == D_IN_P
    return cols


_W_IN_COLS = _w_in_columns()


def _pad_heads(w, real, n_heads):
    k = w.shape[0]
    w = w.reshape(k, n_heads, real)
    return jnp.pad(w, ((0, 0), (0, 0), (0, LANES - real))).reshape(k, n_heads * LANES)


def _row(v):
    return v.reshape(1, -1).astype(_F32)


def _prep_layer(p):
    src = jnp.asarray(np.maximum(_W_IN_COLS, 0))
    keep = jnp.asarray(_W_IN_COLS >= 0)
    w_in = jnp.where(keep[None, :], p['w_in'][:, src], 0.0).astype(_BF)
    kn, qn = p['mla_kn'], p['mla_qn']
    zeros32 = jnp.zeros((32,), _F32)
    g_kr = jnp.concatenate([kn[80:96], kn[64:80], zeros32, kn[64:80], kn[80:96], zeros32])
    dq, dk = p['dil_qn'], p['dil_kn']
    w_up = p['w_up'].reshape(D_MODEL, 2, N_FF_CH, FF_CH).transpose(2, 0, 1, 3).reshape(N_FF_CH, D_MODEL, 2 * FF_CH)
    conv_w = p['conv_w'].reshape(CONV_W, 2, N_FF_CH, FF_CH).transpose(2, 0, 1, 3).reshape(N_FF_CH, CONV_W, 2 * FF_CH)
    conv_b = p['conv_b'].reshape(2, N_FF_CH, FF_CH).transpose(1, 0, 2).reshape(N_FF_CH, 1, 2 * FF_CH)
    return {
        'g_mix': _row(p['norm_mix']), 'w_in': w_in,
        'g_cq': _row(p['mla_q_norm']), 'g_ckv': _row(p['mla_kv_norm']),
        'w_uq': _pad_heads(p['w_uq'], MLA_QK, MLA_HEADS).astype(_BF),
        'w_uk': _pad_heads(p['w_uk'], MLA_NOPE, MLA_HEADS).astype(_BF),
        'w_uv': _pad_heads(p['w_uv'], MLA_V, MLA_HEADS).astype(_BF),
        'g_mq': _row(jnp.concatenate([qn, zeros32]) * (MLA_QK ** -0.5 * LOG2E)),
        'g_mk': _row(jnp.concatenate([kn[:MLA_NOPE], jnp.zeros((64,), _F32)])),
        'g_kr': _row(g_kr),
        'g_nq': _row(jnp.tile(p['na_qn'], 2) * HEAD_DIM ** -0.5), 'g_nk': _row(jnp.tile(p['na_kn'], 2)),
        'g_dq': _row(jnp.concatenate([dq[:32], dq[:32], dq[32:], dq[32:]]) * HEAD_DIM ** -0.5),
        'g_dk': _row(jnp.concatenate([dk[:32], dk[:32], dk[32:], dk[32:]])),
        'w_o': p['w_o'].astype(_BF),
        'g_cross': _row(p['norm_cross']), 'w_cq': p['w_cq'].astype(_BF),
        'g_xq': _row(p['x_qn']), 'g_xk': _row(p['x_kn']), 'g_mem': _row(p['norm_mem']),
        'w_ckv': p['w_ckv'].astype(_BF), 'w_co': p['w_co'].astype(_BF),
        'g_ffn': _row(p['norm_ffn']), 'w_up': w_up.astype(_BF),
        'conv_w': conv_w.astype(_F32), 'conv_b': conv_b.astype(_F32),
        'w_down': p['w_down'].astype(_BF),
        'na_bias': _na_bias(p['na_rpb']),
    }


def _na_bias(rpb):
    dl = np.arange(NA_WIN_R)
    j = np.arange(NA_WIN_R)
    r_off = j[None, :] - dl[:, None] + (NA_WIN_R - 1)
    qc = np.arange(GRID_W)
    kc = np.arange(GRID_W)
    c_start = np.clip(qc - NA_WIN_C // 2, 0, GRID_W - NA_WIN_C)
    valid = (kc[None, :] >= c_start[:, None]) & (kc[None, :] < c_start[:, None] + NA_WIN_C)
    c_off = np.clip(kc[None, :] - qc[:, None] + (NA_WIN_C - 1), 0, 2 * NA_WIN_C - 2)
    b = rpb[:, r_off]
    b = b[:, :, :, c_off]
    b = jnp.where(jnp.asarray(valid)[None, None, None], b, NEG)
    b = b.transpose(0, 1, 3, 2, 4).reshape(NA_HEADS // 2, 2, NA_WIN_R, GRID_W, NA_WIN_R * GRID_W)
    b = b.transpose(0, 2, 1, 3, 4).astype(_F32)
    return b.reshape(NA_HEADS // 2, NA_WIN_R, 2 * GRID_W, NA_WIN_R * GRID_W)


def _band_bias():
    i = np.arange(DIL_TQ)[:, None]
    c = np.arange(2 * DIL_TQ)[None, :]
    band = (c >= i) & (c <= i + DIL_TQ)
    out = []
    for edge in range(4):
        ok = band
        if edge & 1:
            ok = ok & (c >= DIL_TQ // 2)
        if edge & 2:
            ok = ok & (c < 2 * DIL_TQ - DIL_TQ // 2)
        out.append(np.tile(np.where(ok, 0.0, NEG), (2, 1)))
    return jnp.asarray(np.stack(out), _F32)


def _rope_tables(seq):
    pos = jnp.arange(seq, dtype=_F32)[:, None]

    def cs(half):
        inv = ROPE_THETA ** (-jnp.arange(half, dtype=_F32) / half)
        ang = pos * inv[None, :]
        return jnp.cos(ang), jnp.sin(ang)

    c16, s16 = cs(MLA_ROPE // 2)
    c32, s32 = cs(HEAD_DIM // 2)
    one = jnp.ones((seq, 64), _F32)
    z16 = jnp.zeros((seq, 16), _F32)
    z32 = jnp.zeros((seq, 32), _F32)
    z64 = jnp.zeros((seq, 64), _F32)
    return {
        'q_c': jnp.concatenate([one, c16, c16, z32], axis=1),
        'q_s1': jnp.concatenate([z64, z16, s16, z32], axis=1),
        'q_s2': jnp.concatenate([z64, -s16, z16, z32], axis=1),
        'k_c': jnp.concatenate([z64, c16, c16, z32], axis=1),
        'k_s': jnp.concatenate([z64, -s16, s16, z32], axis=1),
        'd_c': jnp.concatenate([c32, c32, c32, c32], axis=1),
        'd_s': jnp.concatenate([-s32, -s32, s32, s32], axis=1),
    }


def _trunk_layer(x, mem, lw, tabs, band, batch, seq):
    qm, km, vm, qn, kn, vn, qd, kd, vd = _mix_in(x, lw, tabs, seq)
    oa = _mla(qm, km, vm, batch, seq)
    ob = _na(qn, kn, vn, lw['na_bias'], batch, seq)
    oc = _dil(qd, kd, vd, band, batch, seq)
    mk, mv = _mem_kv(mem, lw)
    x = _mix_out(x, oa, ob, oc, mk, mv, lw, seq)
    return _ffn(x, lw, seq)


def kernel(x_prompt, x_sample, mem_prompt, mem_sample, norm_mix, w_in, mla_q_norm, mla_kv_norm, w_uq, w_uk, w_uv, mla_qn, mla_kn, na_qn, na_kn, na_rpb, dil_qn, dil_kn, w_o, norm_cross, norm_mem, w_cq, w_ckv, x_qn, x_kn, w_co, norm_ffn, w_up, conv_w, conv_b, w_down):
    stacked = {
        'norm_mix': norm_mix, 'w_in': w_in, 'mla_q_norm': mla_q_norm, 'mla_kv_norm': mla_kv_norm,
        'w_uq': w_uq, 'w_uk': w_uk, 'w_uv': w_uv, 'mla_qn': mla_qn, 'mla_kn': mla_kn,
        'na_qn': na_qn, 'na_kn': na_kn, 'na_rpb': na_rpb, 'dil_qn': dil_qn, 'dil_kn': dil_kn,
        'w_o': w_o, 'norm_cross': norm_cross, 'norm_mem': norm_mem, 'w_cq': w_cq, 'w_ckv': w_ckv,
        'x_qn': x_qn, 'x_kn': x_kn, 'w_co': w_co, 'norm_ffn': norm_ffn, 'w_up': w_up,
        'conv_w': conv_w, 'conv_b': conv_b, 'w_down': w_down,
    }
    layers = [_prep_layer({k: v[l] for k, v in stacked.items()}) for l in range(DEPTH)]
    band = _band_bias()
    outs = []
    for x, mem in ((x_prompt, mem_prompt), (x_sample, mem_sample)):
        batch, seq, _ = x.shape
        tabs = _rope_tables(seq)
        xt = x.reshape(batch * seq, D_MODEL)
        for lw in layers:
            xt = _trunk_layer(xt, mem, lw, tabs, band, batch, seq)
        outs.append(xt.reshape(batch, seq, D_MODEL))
    return tuple(outs)
```

```python
import functools

import jax
import jax.numpy as jnp
import numpy as np
from jax import lax
from jax.experimental import pallas as pl
from jax.experimental.pallas import tpu as pltpu

D_MODEL = 1024
DEPTH = 4
GRID_W = 64
HEAD_DIM = 64
ROPE_THETA = 10000.0
EPS = 1e-6
MLA_HEADS = 6
MLA_Q_RANK = 256
MLA_KV_RANK = 128
MLA_NOPE = 64
MLA_ROPE = 32
MLA_V = 64
MLA_QK = MLA_NOPE + MLA_ROPE
NA_HEADS = 6
NA_WIN_R = 8
NA_WIN_C = 16
DIL_WINDOWS = (128, 512, 2048)
DIL_RATES = (1, 4, 16)
DIL_GROUPS = 3
DIL_HEADS = 4
DIL_ALL = DIL_GROUPS * DIL_HEADS
X_HEADS = 4
X_HEAD_DIM = D_MODEL // X_HEADS
D_FF = 2816
CONV_W = 3
SPLIT_CQ = MLA_Q_RANK
SPLIT_CKV = SPLIT_CQ + MLA_KV_RANK
SPLIT_KR = SPLIT_CKV + MLA_ROPE
SPLIT_NA = SPLIT_KR + 3 * NA_HEADS * HEAD_DIM
D_IN = SPLIT_NA + 3 * DIL_ALL * HEAD_DIM

LANES = 128
NEG = -1e30
LOG2E = 1.4426950408889634
TM = 512
DIL_TQ = 128
DIL_SB = DIL_TQ * max(DIL_RATES)
DIL_UNROLL = 4
NA_RB = 8
FF_CH = 256
N_FF_CH = D_FF // FF_CH
VMEM_LIMIT = 56 * 1024 * 1024

COL_MLA = 0
COL_NA = 512
COL_DIL = COL_NA + 3 * NA_HEADS * HEAD_DIM
D_IN_P = COL_DIL + 3 * DIL_ALL * HEAD_DIM

_BF = jnp.bfloat16
_F32 = jnp.float32


def _dot(a, b):
    return jnp.dot(a, b, preferred_element_type=_F32)


def _dot_t(a, b):
    return lax.dot_general(a, b, (((1,), (1,)), ((), ())), preferred_element_type=_F32)


def _rms(x, g):
    ms = jnp.mean(x * x, axis=-1, keepdims=True)
    return x * lax.rsqrt(ms + EPS) * g


def _lane(shape):
    return lax.broadcasted_iota(jnp.int32, shape, len(shape) - 1)


def _head_join_matrices():
    i = np.arange(2 * LANES)
    same_group = (i[:, None] // LANES) == (i[None, :] // LANES)
    natural = (i[:, None] // HEAD_DIM) == (i[None, :] // HEAD_DIM)
    half = ((i % HEAD_DIM) < HEAD_DIM // 2)
    rotary = same_group & (half[:, None] == half[None, :])
    return [jnp.asarray(m, _BF) for m in (same_group, natural, rotary)]


def _seg_inv(x2, bd_ref, dim):
    ss = _dot((x2 * x2).astype(_BF), bd_ref[...])
    return lax.rsqrt(ss * (1.0 / dim) + EPS)


def _mix_in_kernel(x_ref, gmix_ref, win_ref, gcq_ref, gckv_ref, wuq_ref, wuk_ref, wuv_ref,
                   gmq_ref, gmk_ref, gkr_ref, gnq_ref, gnk_ref, gdq_ref, gdk_ref,
                   bd128_ref, bd64_ref, bdr_ref,
                   tqc_ref, tqs1_ref, tqs2_ref, tkc_ref, tks_ref, tdc_ref, tds_ref,
                   qm_ref, km_ref, vm_ref, qn_ref, kn_ref, vn_ref, qd_ref, kd_ref, vd_ref,
                   perm_ref):
    tm = x_ref.shape[0]
    h = _rms(x_ref[...], gmix_ref[...]).astype(_BF)
    lane = _lane((tm, LANES))

    z = _dot(h, win_ref[:, COL_MLA:COL_NA])
    cq = _rms(z[:, 0:MLA_Q_RANK], gcq_ref[...]).astype(_BF)
    ckv = _rms(z[:, MLA_Q_RANK:MLA_Q_RANK + MLA_KV_RANK], gckv_ref[...]).astype(_BF)
    kr = z[:, 384:512]
    kr_own = jnp.where((lane >= MLA_NOPE) & (lane < MLA_QK), kr, 0.0)
    krg = kr * gkr_ref[...]
    kr_rot = krg * tkc_ref[...] + pltpu.roll(krg, 64, 1) * tks_ref[...]
    qpre = _dot(cq, wuq_ref[...])
    kpre = _dot(ckv, wuk_ref[...])
    vpre = _dot(ckv, wuv_ref[...])
    row_t = lax.broadcasted_iota(jnp.int32, (LANES, tm), 0)
    for c in range(MLA_HEADS // 2):
        q2 = qpre[:, 2 * c * LANES:(2 * c + 2) * LANES]
        k2 = kpre[:, 2 * c * LANES:(2 * c + 2) * LANES]
        inv_q = _seg_inv(q2, bd128_ref, MLA_QK)
        inv_k = _seg_inv(k2 + jnp.concatenate([kr_own, kr_own], axis=1), bd128_ref, MLA_QK)
        for hh in range(2):
            hd = 2 * c + hh
            sl = slice(hh * LANES, (hh + 1) * LANES)
            qg = q2[:, sl] * gmq_ref[...]
            qr = qg * tqc_ref[...] + pltpu.roll(qg, 16, 1) * tqs1_ref[...] + pltpu.roll(qg, 112, 1) * tqs2_ref[...]
            qm_ref[hd] = (qr * inv_q[:, sl]).astype(_BF)
            km_ref[hd] = ((k2[:, sl] * gmk_ref[...] + kr_rot) * inv_k[:, sl]).astype(_BF)
            vt = vpre[:, hd * LANES:(hd + 1) * LANES].T
            vm_ref[hd, 0] = jnp.where(row_t == MLA_V, 1.0, vt).astype(_BF)

    z = _dot(h, win_ref[:, COL_NA:COL_DIL])
    q2 = z[:, 0:256]
    k2 = z[:, 256:512]
    qk2 = z[:, 768:1024]
    qn2 = q2 * _seg_inv(q2, bd64_ref, HEAD_DIM)
    kn2 = k2 * _seg_inv(k2, bd64_ref, HEAD_DIM)
    qkn2 = qk2 * _seg_inv(qk2, bd64_ref, HEAD_DIM)
    for p in range(2):
        sl = slice(p * LANES, (p + 1) * LANES)
        qn_ref[p] = (qn2[:, sl] * gnq_ref[...]).astype(_BF)
        kn_ref[p] = (kn2[:, sl] * gnk_ref[...]).astype(_BF)
        vn_ref[p] = z[:, 512 + p * LANES: 512 + (p + 1) * LANES].astype(_BF)
    qn_ref[2] = (qkn2[:, :LANES] * gnq_ref[...]).astype(_BF)
    kn_ref[2] = (qkn2[:, LANES:] * gnk_ref[...]).astype(_BF)
    vn_ref[2] = z[:, 1024:1152].astype(_BF)

    def put(out_ref, p, val, rate):
        if rate == 1:
            out_ref[p] = val.astype(_BF)
        else:
            rows = tm // rate
            perm_ref[...] = val
            for rho in range(rate):
                out_ref[p, rho * rows:(rho + 1) * rows, :] = perm_ref[pl.ds(rho, rows, stride=rate), :].astype(_BF)

    for g in range(DIL_GROUPS):
        z = _dot(h, win_ref[:, COL_DIL + g * 768: COL_DIL + (g + 1) * 768])
        q2 = z[:, 0:256]
        k2 = z[:, 256:512]
        qn2 = q2 * _seg_inv(q2, bdr_ref, HEAD_DIM)
        kn2 = k2 * _seg_inv(k2, bdr_ref, HEAD_DIM)
        for pp in range(2):
            p = 2 * g + pp
            sl = slice(pp * LANES, (pp + 1) * LANES)
            qx = qn2[:, sl] * gdq_ref[...]
            qx = qx * tdc_ref[...] + pltpu.roll(qx, 64, 1) * tds_ref[...]
            kx = kn2[:, sl] * gdk_ref[...]
            kx = kx * tdc_ref[...] + pltpu.roll(kx, 64, 1) * tds_ref[...]
            put(qd_ref, p, qx, DIL_RATES[g])
            put(kd_ref, p, kx, DIL_RATES[g])
            put(vd_ref, p, z[:, 512 + pp * LANES: 512 + (pp + 1) * LANES], DIL_RATES[g])


def _const_spec(shape):
    nd = len(shape)
    return pl.BlockSpec(shape, lambda i, _nd=nd: (0,) * _nd)


def _mix_in(x, lw, tabs, seq):
    t = x.shape[0]
    nblk_seq = seq // TM
    tab_spec = pl.BlockSpec((TM, LANES), lambda i: (i % nblk_seq, 0))
    weights = [lw['g_mix'], lw['w_in'], lw['g_cq'], lw['g_ckv'], lw['w_uq'], lw['w_uk'], lw['w_uv'],
               lw['g_mq'], lw['g_mk'], lw['g_kr'], lw['g_nq'], lw['g_nk'], lw['g_dq'], lw['g_dk'],
               *_head_join_matrices()]
    tables = [tabs['q_c'], tabs['q_s1'], tabs['q_s2'], tabs['k_c'], tabs['k_s'], tabs['d_c'], tabs['d_s']]

    def out(n):
        return (jax.ShapeDtypeStruct((n, t, LANES), _BF), pl.BlockSpec((n, TM, LANES), lambda i: (0, i, 0)))

    vt_out = (jax.ShapeDtypeStruct((MLA_HEADS, t // TM, LANES, TM), _BF),
              pl.BlockSpec((MLA_HEADS, 1, LANES, TM), lambda i: (0, i, 0, 0)))
    outs = [out(6), out(6), vt_out, out(3), out(3), out(3), out(6), out(6), out(6)]
    return pl.pallas_call(
        _mix_in_kernel,
        grid=(t // TM,),
        in_specs=[pl.BlockSpec((TM, D_MODEL), lambda i: (i, 0))]
        + [_const_spec(w.shape) for w in weights] + [tab_spec] * len(tables),
        out_specs=[o[1] for o in outs],
        out_shape=[o[0] for o in outs],
        scratch_shapes=[pltpu.VMEM((TM, LANES), _F32)],
        compiler_params=pltpu.CompilerParams(dimension_semantics=("parallel",), vmem_limit_bytes=VMEM_LIMIT),
        name="mix_in",
    )(x, *weights, *tables)


def _mla_kernel(q_ref, k_ref, vt_ref, o_ref, sa_sc, sb_sc, m_sc, acc_sc, *, tk):
    tq = q_ref.shape[1]
    nk = k_ref.shape[1] // tk
    sub = tk // TM
    m_sc[...] = jnp.full(m_sc.shape, NEG, _F32)
    acc_sc[...] = jnp.zeros(acc_sc.shape, _F32)

    def scores(j, dst):
        rows = pl.ds(pl.multiple_of(j * tk, tk), tk)
        for hh in range(2):
            dst[hh] = _dot_t(k_ref[hh, rows, :], q_ref[hh])

    def consume(j, src):
        for hh in range(2):
            s = src[hh]
            m_prev = m_sc[hh]
            m_new = jnp.maximum(m_prev, jnp.max(s, axis=0, keepdims=True))
            p = jnp.exp2(s - m_new).astype(_BF)
            vt = jnp.concatenate([vt_ref[hh, j * sub + c] for c in range(sub)], axis=1)
            acc_sc[hh] = jnp.exp2(m_prev - m_new) * acc_sc[hh] + _dot(vt, p)
            m_sc[hh] = m_new

    def pair(i, carry):
        j = 2 * i
        scores(j + 1, sb_sc)
        consume(j, sa_sc)
        scores(j + 2, sa_sc)
        consume(j + 1, sb_sc)
        return carry

    scores(0, sa_sc)
    lax.fori_loop(0, nk // 2 - 1, pair, 0)
    scores(nk - 1, sb_sc)
    consume(nk - 2, sa_sc)
    consume(nk - 1, sb_sc)

    lane = _lane((tq, LANES))
    outs = []
    for hh in range(2):
        acc = acc_sc[hh]
        outs.append((acc / acc[MLA_V:MLA_V + 1, :]).T)
    o_ref[0] = jnp.where(lane < MLA_V, outs[0], pltpu.roll(outs[1], MLA_V, 1)).astype(o_ref.dtype)


def _mla(q, k, vt, batch, seq, tq=1024, tk=512):
    t = q.shape[1]
    nq = seq // tq
    nblk = seq // TM
    assert seq % (2 * tk) == 0 and tk % TM == 0
    return pl.pallas_call(
        functools.partial(_mla_kernel, tk=tk),
        grid=(batch, MLA_HEADS // 2, nq),
        in_specs=[pl.BlockSpec((2, tq, LANES), lambda b, p, i: (p, b * nq + i, 0)),
                  pl.BlockSpec((2, seq, LANES), lambda b, p, i: (p, b, 0)),
                  pl.BlockSpec((2, nblk, LANES, TM), lambda b, p, i: (p, b, 0, 0))],
        out_specs=pl.BlockSpec((1, tq, LANES), lambda b, p, i: (p, b * nq + i, 0)),
        out_shape=jax.ShapeDtypeStruct((MLA_HEADS // 2, t, LANES), _BF),
        scratch_shapes=[pltpu.VMEM((2, tk, tq), _F32), pltpu.VMEM((2, tk, tq), _F32),
                        pltpu.VMEM((2, 1, tq), _F32), pltpu.VMEM((2, LANES, tq), _F32)],
        compiler_params=pltpu.CompilerParams(
            dimension_semantics=("parallel", "parallel", "arbitrary"), vmem_limit_bytes=VMEM_LIMIT),
        name="mla_attention",
    )(q, k, vt)


def _na_kernel(q_ref, k_ref, v_ref, bias_ref, o_ref, *, rows):
    i = pl.program_id(2)
    lane = _lane((GRID_W, LANES))
    first = lane < HEAD_DIM
    nkeys = NA_WIN_R * GRID_W

    scores, windows = [], []
    for rr in range(NA_RB):
        r = i * NA_RB + rr
        rs = jnp.clip(r - NA_WIN_R // 2, 0, rows - NA_WIN_R)
        start = pl.multiple_of(rs * GRID_W, GRID_W)
        q = q_ref[0, rr * GRID_W:(rr + 1) * GRID_W, :]
        zero = jnp.zeros_like(q)
        q2 = jnp.concatenate([jnp.where(first, q, zero), jnp.where(first, zero, q)], axis=0)
        scores.append(_dot_t(q2, k_ref[0, pl.ds(start, nkeys), :]) + bias_ref[0, r - rs])
        windows.append(start)
    probs = []
    for s in scores:
        p = jnp.exp(s - jnp.max(s, axis=1, keepdims=True))
        probs.append((p.astype(_BF), jnp.sum(p, axis=1, keepdims=True)))
    for rr, ((p, l), start) in enumerate(zip(probs, windows)):
        o2 = _dot(p, v_ref[0, pl.ds(start, nkeys), :]) / l
        o = jnp.where(first, o2[:GRID_W], o2[GRID_W:])
        o_ref[0, rr * GRID_W:(rr + 1) * GRID_W, :] = o.astype(o_ref.dtype)


def _na(q, k, v, bias, batch, seq):
    t = q.shape[1]
    rows = seq // GRID_W
    assert rows >= NA_WIN_R and rows % NA_RB == 0
    nrb = rows // NA_RB
    qrows = NA_RB * GRID_W
    return pl.pallas_call(
        functools.partial(_na_kernel, rows=rows),
        grid=(batch, NA_HEADS // 2, nrb),
        in_specs=[pl.BlockSpec((1, qrows, LANES), lambda b, p, i: (p, b * nrb + i, 0)),
                  pl.BlockSpec((1, seq, LANES), lambda b, p, i: (p, b, 0)),
                  pl.BlockSpec((1, seq, LANES), lambda b, p, i: (p, b, 0)),
                  pl.BlockSpec((1, NA_WIN_R, 2 * GRID_W, NA_WIN_R * GRID_W), lambda b, p, i: (p, 0, 0, 0))],
        out_specs=pl.BlockSpec((1, qrows, LANES), lambda b, p, i: (p, b * nrb + i, 0)),
        out_shape=jax.ShapeDtypeStruct((NA_HEADS // 2, t, LANES), _BF),
        compiler_params=pltpu.CompilerParams(
            dimension_semantics=("parallel", "parallel", "arbitrary"), vmem_limit_bytes=VMEM_LIMIT),
        name="na_attention",
    )(q, k, v, bias)


def _dil_kernel(q0_ref, q1_ref, q2_ref, k0_ref, k1_ref, k2_ref, v0_ref, v1_ref, v2_ref, band_ref,
                o_ref, og_sc, lse_sc, *, seq):
    sb = pl.program_id(2)
    q_refs = (q0_ref, q1_ref, q2_ref)
    k_refs = (k0_ref, k1_ref, k2_ref)
    v_refs = (v0_ref, v1_ref, v2_ref)
    lane = _lane((DIL_TQ, LANES))
    first_qk = (lane % HEAD_DIM) < (HEAD_DIM // 2)
    first_v = lane < HEAD_DIM
    n_tiles = DIL_SB // DIL_TQ
    half = DIL_TQ // 2

    for g in range(DIL_GROUPS):
        rate = DIL_RATES[g]
        length = seq // rate
        per_blk = TM // rate
        chunk = min(half, per_blk)
        n_chunks = (2 * DIL_TQ) // chunk
        tiles_per_res = n_tiles // rate
        q_ref, k_ref, v_ref = q_refs[g], k_refs[g], v_refs[g]

        def row_of(m, rho, _per_blk=per_blk):
            return (m // _per_blk) * TM + rho * _per_blk + (m % _per_blk)

        def body(tg, carry, _g=g, _rate=rate, _length=length, _per_blk=per_blk, _chunk=chunk,
                 _n_chunks=n_chunks, _tpr=tiles_per_res, _q=q_ref, _k=k_ref, _v=v_ref):
            scores, metas = [], []
            for u in range(DIL_UNROLL):
                tt = tg * DIL_UNROLL + u
                rho = tt // _tpr
                mt = tt % _tpr
                m0 = sb * (DIL_SB // _rate) + mt * DIL_TQ
                q = jnp.concatenate(
                    [_q[0, pl.ds(pl.multiple_of(row_of(m0 + c * _chunk, rho), _chunk), _chunk), :]
                     for c in range(DIL_TQ // _chunk)], axis=0)
                zero = jnp.zeros_like(q)
                q2 = jnp.concatenate([jnp.where(first_qk, q, zero), jnp.where(first_qk, zero, q)], axis=0)
                win = []
                for c in range(_n_chunks):
                    ms = jnp.clip(m0 - half + c * _chunk, 0, _length - _chunk)
                    win.append(pl.multiple_of(row_of(ms, rho), _chunk))
                kw = jnp.concatenate([_k[0, pl.ds(row, _chunk), :] for row in win], axis=0)
                edge = (m0 == 0).astype(jnp.int32) + 2 * (m0 == _length - DIL_TQ).astype(jnp.int32)
                scores.append(_dot_t(q2, kw) + band_ref[edge])
                metas.append((rho + _rate * mt * DIL_TQ, win))
            probs = []
            for s in scores:
                m = jnp.max(s, axis=1, keepdims=True)
                p = jnp.exp(s - m)
                probs.append((p.astype(_BF), m, jnp.sum(p, axis=1, keepdims=True)))
            for (p, m, l), (base, win) in zip(probs, metas):
                vw = jnp.concatenate([_v[0, pl.ds(row, _chunk), :] for row in win], axis=0)
                o2 = _dot(p, vw) / l
                lse2 = jnp.broadcast_to(m + jnp.log(l), (2 * DIL_TQ, LANES))
                if _rate == 1:
                    dst = pl.ds(pl.multiple_of(base, DIL_TQ), DIL_TQ)
                else:
                    dst = pl.ds(base, DIL_TQ, stride=_rate)
                og_sc[_g, dst, :] = jnp.where(first_v, o2[:DIL_TQ], o2[DIL_TQ:])
                lse_sc[_g, dst, :] = jnp.where(first_v, lse2[:DIL_TQ], lse2[DIL_TQ:])
            return carry

        lax.fori_loop(0, n_tiles // DIL_UNROLL, body, 0)

    m = jnp.maximum(jnp.maximum(lse_sc[0], lse_sc[1]), lse_sc[2])
    w0 = jnp.exp(lse_sc[0] - m)
    w1 = jnp.exp(lse_sc[1] - m)
    w2 = jnp.exp(lse_sc[2] - m)
    o = (w0 * og_sc[0] + w1 * og_sc[1] + w2 * og_sc[2]) / (w0 + w1 + w2)
    o_ref[0] = o.astype(o_ref.dtype)


def _dil(q, k, v, band, batch, seq):
    t = q.shape[1]
    assert seq % DIL_SB == 0
    nsb = seq // DIL_SB

    def q_spec(g):
        return pl.BlockSpec((1, DIL_SB, LANES), lambda b, sp, s, _g=g: (2 * _g + sp, b * nsb + s, 0))

    def kv_spec(g):
        return pl.BlockSpec((1, seq, LANES), lambda b, sp, s, _g=g: (2 * _g + sp, b, 0))

    return pl.pallas_call(
        functools.partial(_dil_kernel, seq=seq),
        grid=(batch, 2, nsb),
        in_specs=[kv_spec(0), kv_spec(1), kv_spec(2), kv_spec(0), kv_spec(1), kv_spec(2),
                  kv_spec(0), kv_spec(1), kv_spec(2),
                  pl.BlockSpec((4, 2 * DIL_TQ, 2 * DIL_TQ), lambda b, sp, s: (0, 0, 0))],
        out_specs=pl.BlockSpec((1, DIL_SB, LANES), lambda b, sp, s: (sp, b * nsb + s, 0)),
        out_shape=jax.ShapeDtypeStruct((2, t, LANES), _BF),
        scratch_shapes=[pltpu.VMEM((DIL_GROUPS, DIL_SB, LANES), _F32),
                        pltpu.VMEM((DIL_GROUPS, DIL_SB, LANES), _F32)],
        compiler_params=pltpu.CompilerParams(
            dimension_semantics=("parallel", "parallel", "arbitrary"), vmem_limit_bytes=VMEM_LIMIT),
        name="dil_attention",
    )(q, q, q, k, k, k, v, v, v, band)


def _mem_kv_kernel(mem_ref, gmem_ref, wckv_ref, gxk_ref, k_ref, v_ref):
    hm = _rms(mem_ref[0], gmem_ref[...]).astype(_BF)
    kv = _dot(hm, wckv_ref[...])
    for hd in range(X_HEADS):
        kh = kv[:, hd * X_HEAD_DIM:(hd + 1) * X_HEAD_DIM]
        k_ref[0, :, hd * X_HEAD_DIM:(hd + 1) * X_HEAD_DIM] = _rms(kh, gxk_ref[...]).astype(_BF)
    v_ref[0] = kv[:, D_MODEL:].astype(_BF)


def _mem_kv(mem, lw):
    b, m, _ = mem.shape
    return pl.pallas_call(
        _mem_kv_kernel,
        grid=(b,),
        in_specs=[pl.BlockSpec((1, m, D_MODEL), lambda i: (i, 0, 0)),
                  _const_spec(lw['g_mem'].shape), _const_spec(lw['w_ckv'].shape), _const_spec(lw['g_xk'].shape)],
        out_specs=[pl.BlockSpec((1, m, D_MODEL), lambda i: (i, 0, 0))] * 2,
        out_shape=[jax.ShapeDtypeStruct((b, m, D_MODEL), _BF)] * 2,
        compiler_params=pltpu.CompilerParams(dimension_semantics=("parallel",), vmem_limit_bytes=VMEM_LIMIT),
        name="mem_kv",
    )(mem, lw['g_mem'], lw['w_ckv'], lw['g_xk'])


def _mix_out_kernel(x_ref, oa_ref, ob_ref, oc_ref, wo_ref, gcross_ref, wcq_ref, gxq_ref,
                    mk_ref, mv_ref, wco_ref, y_ref):
    mix = jnp.concatenate([oa_ref[p] for p in range(3)] + [ob_ref[p] for p in range(3)]
                          + [oc_ref[p] for p in range(2)], axis=1)
    x1 = x_ref[...] + _dot(mix, wo_ref[...])
    hc = _rms(x1, gcross_ref[...]).astype(_BF)
    q = _dot(hc, wcq_ref[...])
    outs = []
    for hd in range(X_HEADS):
        sl = slice(hd * X_HEAD_DIM, (hd + 1) * X_HEAD_DIM)
        qh = (_rms(q[:, sl], gxq_ref[...]) * X_HEAD_DIM ** -0.5).astype(_BF)
        s = _dot_t(qh, mk_ref[0, :, sl])
        m = jnp.max(s, axis=1, keepdims=True)
        p = jnp.exp(s - m)
        l = jnp.sum(p, axis=1, keepdims=True)
        outs.append((_dot(p.astype(_BF), mv_ref[0, :, sl]) / l).astype(_BF))
    o = jnp.concatenate(outs, axis=1)
    y_ref[...] = x1 + _dot(o, wco_ref[...])


def _mix_out(x, oa, ob, oc, mk, mv, lw, seq):
    t = x.shape[0]
    nblk_seq = seq // TM
    mem_len = mk.shape[1]

    def grp(n):
        return pl.BlockSpec((n, TM, LANES), lambda i: (0, i, 0))

    mem_spec = pl.BlockSpec((1, mem_len, D_MODEL), lambda i: (i // nblk_seq, 0, 0))
    return pl.pallas_call(
        _mix_out_kernel,
        grid=(t // TM,),
        in_specs=[pl.BlockSpec((TM, D_MODEL), lambda i: (i, 0)), grp(3), grp(3), grp(2),
                  _const_spec(lw['w_o'].shape), _const_spec(lw['g_cross'].shape),
                  _const_spec(lw['w_cq'].shape), _const_spec(lw['g_xq'].shape),
                  mem_spec, mem_spec, _const_spec(lw['w_co'].shape)],
        out_specs=pl.BlockSpec((TM, D_MODEL), lambda i: (i, 0)),
        out_shape=jax.ShapeDtypeStruct((t, D_MODEL), _F32),
        compiler_params=pltpu.CompilerParams(dimension_semantics=("parallel",), vmem_limit_bytes=VMEM_LIMIT),
        name="mix_out",
    )(x, oa, ob, oc, lw['w_o'], lw['g_cross'], lw['w_cq'], lw['g_xq'], mk, mv, lw['w_co'])


def _ffn_kernel(x_ref, xp_ref, xn_ref, gffn_ref, wup_ref, cw_ref, cb_ref, wdn_ref, y_ref, *, nblk_seq):
    i = pl.program_id(0)
    tm = x_ref.shape[0]
    x = x_ref[...]
    g = gffn_ref[...]
    is_first = (i % nblk_seq) == 0
    is_last = (i % nblk_seq) == nblk_seq - 1
    hp = jnp.where(is_first, 0.0, _rms(xp_ref[...], g))
    hn = jnp.where(is_last, 0.0, _rms(xn_ref[...], g))
    h = jnp.concatenate([hp, _rms(x, g), hn], axis=0).astype(_BF)
    acc = x
    pending = []
    u_next = _dot(h, wup_ref[0])
    for c in range(N_FF_CH):
        u = u_next
        if c + 1 < N_FF_CH:
            u_next = _dot(h, wup_ref[c + 1])
        cw = cw_ref[c]
        up = pltpu.roll(u, 1, 0)[8:8 + tm]
        un = pltpu.roll(u, tm + 15, 0)[8:8 + tm]
        uc = up * cw[0:1] + u[8:8 + tm] * cw[1:2] + un * cw[2:3] + cb_ref[c]
        a = uc[:, :FF_CH]
        half_gate = 0.5 * uc[:, FF_CH:]
        pending.append((half_gate * (1.0 + jnp.tanh(half_gate)) * a).astype(_BF))
        rest = N_FF_CH - 1 - c
        if len(pending) == 2 and rest != 1 or rest == 0:
            lo = (c + 1 - len(pending)) * FF_CH
            act = pending[0] if len(pending) == 1 else jnp.concatenate(pending, axis=1)
            acc = acc + _dot(act, wdn_ref[lo:(c + 1) * FF_CH, :])
            pending = []
    y_ref[...] = acc


def _ffn(x, lw, seq):
    t = x.shape[0]
    nblk_seq = seq // TM
    nhalo = t // 8
    per = TM // 8
    return pl.pallas_call(
        functools.partial(_ffn_kernel, nblk_seq=nblk_seq),
        grid=(t // TM,),
        in_specs=[pl.BlockSpec((TM, D_MODEL), lambda i: (i, 0)),
                  pl.BlockSpec((8, D_MODEL), lambda i: (jnp.maximum(i * per - 1, 0), 0)),
                  pl.BlockSpec((8, D_MODEL), lambda i: (jnp.minimum((i + 1) * per, nhalo - 1), 0)),
                  _const_spec(lw['g_ffn'].shape), _const_spec(lw['w_up'].shape),
                  _const_spec(lw['conv_w'].shape), _const_spec(lw['conv_b'].shape),
                  _const_spec(lw['w_down'].shape)],
        out_specs=pl.BlockSpec((TM, D_MODEL), lambda i: (i, 0)),
        out_shape=jax.ShapeDtypeStruct((t, D_MODEL), _F32),
        compiler_params=pltpu.CompilerParams(dimension_semantics=("parallel",), vmem_limit_bytes=VMEM_LIMIT),
        name="conv_ffn",
    )(x, x, x, lw['g_ffn'], lw['w_up'], lw['conv_w'], lw['conv_b'], lw['w_down'])


def _w_in_columns():
    cols = np.full((D_IN_P,), -1, np.int64)
    cols[0:SPLIT_CKV] = np.arange(SPLIT_CKV)
    a = SPLIT_CKV + np.arange(16)
    b = SPLIT_CKV + 16 + np.arange(16)
    cols[384:400], cols[400:416] = b, a
    cols[448:464], cols[464:480] = a, b
    d = np.arange(HEAD_DIM)

    def na_col(kind, head):
        return SPLIT_KR + (kind * NA_HEADS + head) * HEAD_DIM + d

    def dil_col(kind, head):
        return SPLIT_NA + (kind * DIL_ALL + head) * HEAD_DIM + d

    pos = COL_NA
    for pairs in ((0, 1), (2,)):
        for kind in range(3):
            for p in pairs:
                cols[pos:pos + 128] = np.concatenate([na_col(kind, 2 * p), na_col(kind, 2 * p + 1)])
                pos += 128
    assert pos == COL_DIL
    for g in range(DIL_GROUPS):
        for kind in range(3):
            for pp in range(2):
                h0, h1 = dil_col(kind, 4 * g + 2 * pp), dil_col(kind, 4 * g + 2 * pp + 1)
                if kind < 2:
                    grp = np.concatenate([h0[:32], h1[:32], h0[32:], h1[32:]])
                else:
                    grp = np.concatenate([h0, h1])
                cols[pos:pos + 128] = grp
                pos += 128
    assert pos == D_IN_P
    return cols


_W_IN_COLS = _w_in_columns()


def _pad_heads(w, real, n_heads):
    k = w.shape[0]
    w = w.reshape(k, n_heads, real)
    return jnp.pad(w, ((0, 0), (0, 0), (0, LANES - real))).reshape(k, n_heads * LANES)


def _row(v):
    return v.reshape(1, -1).astype(_F32)


def _prep_layer(p):
    src = jnp.asarray(np.maximum(_W_IN_COLS, 0))
    keep = jnp.asarray(_W_IN_COLS >= 0)
    w_in = jnp.where(keep[None, :], p['w_in'][:, src], 0.0).astype(_BF)
    kn, qn = p['mla_kn'], p['mla_qn']
    zeros32 = jnp.zeros((32,), _F32)
    g_kr = jnp.concatenate([kn[80:96], kn[64:80], zeros32, kn[64:80], kn[80:96], zeros32])
    dq, dk = p['dil_qn'], p['dil_kn']
    w_up = p['w_up'].reshape(D_MODEL, 2, N_FF_CH, FF_CH).transpose(2, 0, 1, 3).reshape(N_FF_CH, D_MODEL, 2 * FF_CH)
    conv_w = p['conv_w'].reshape(CONV_W, 2, N_FF_CH, FF_CH).transpose(2, 0, 1, 3).reshape(N_FF_CH, CONV_W, 2 * FF_CH)
    conv_b = p['conv_b'].reshape(2, N_FF_CH, FF_CH).transpose(1, 0, 2).reshape(N_FF_CH, 1, 2 * FF_CH)
    return {
        'g_mix': _row(p['norm_mix']), 'w_in': w_in,
        'g_cq': _row(p['mla_q_norm']), 'g_ckv': _row(p['mla_kv_norm']),
        'w_uq': _pad_heads(p['w_uq'], MLA_QK, MLA_HEADS).astype(_BF),
        'w_uk': _pad_heads(p['w_uk'], MLA_NOPE, MLA_HEADS).astype(_BF),
        'w_uv': _pad_heads(p['w_uv'], MLA_V, MLA_HEADS).astype(_BF),
        'g_mq': _row(jnp.concatenate([qn, zeros32]) * (MLA_QK ** -0.5 * LOG2E)),
        'g_mk': _row(jnp.concatenate([kn[:MLA_NOPE], jnp.zeros((64,), _F32)])),
        'g_kr': _row(g_kr),
        'g_nq': _row(jnp.tile(p['na_qn'], 2) * HEAD_DIM ** -0.5), 'g_nk': _row(jnp.tile(p['na_kn'], 2)),
        'g_dq': _row(jnp.concatenate([dq[:32], dq[:32], dq[32:], dq[32:]]) * HEAD_DIM ** -0.5),
        'g_dk': _row(jnp.concatenate([dk[:32], dk[:32], dk[32:], dk[32:]])),
        'w_o': p['w_o'].astype(_BF),
        'g_cross': _row(p['norm_cross']), 'w_cq': p['w_cq'].astype(_BF),
        'g_xq': _row(p['x_qn']), 'g_xk': _row(p['x_kn']), 'g_mem': _row(p['norm_mem']),
        'w_ckv': p['w_ckv'].astype(_BF), 'w_co': p['w_co'].astype(_BF),
        'g_ffn': _row(p['norm_ffn']), 'w_up': w_up.astype(_BF),
        'conv_w': conv_w.astype(_F32), 'conv_b': conv_b.astype(_F32),
        'w_down': p['w_down'].astype(_BF),
        'na_bias': _na_bias(p['na_rpb']),
    }


def _na_bias(rpb):
    dl = np.arange(NA_WIN_R)
    j = np.arange(NA_WIN_R)
    r_off = j[None, :] - dl[:, None] + (NA_WIN_R - 1)
    qc = np.arange(GRID_W)
    kc = np.arange(GRID_W)
    c_start = np.clip(qc - NA_WIN_C // 2, 0, GRID_W - NA_WIN_C)
    valid = (kc[None, :] >= c_start[:, None]) & (kc[None, :] < c_start[:, None] + NA_WIN_C)
    c_off = np.clip(kc[None, :] - qc[:, None] + (NA_WIN_C - 1), 0, 2 * NA_WIN_C - 2)
    b = rpb[:, r_off]
    b = b[:, :, :, c_off]
    b = jnp.where(jnp.asarray(valid)[None, None, None], b, NEG)
    b = b.transpose(0, 1, 3, 2, 4).reshape(NA_HEADS // 2, 2, NA_WIN_R, GRID_W, NA_WIN_R * GRID_W)
    b = b.transpose(0, 2, 1, 3, 4).astype(_F32)
    return b.reshape(NA_HEADS // 2, NA_WIN_R, 2 * GRID_W, NA_WIN_R * GRID_W)


def _band_bias():
    i = np.arange(DIL_TQ)[:, None]
    c = np.arange(2 * DIL_TQ)[None, :]
    band = (c >= i) & (c <= i + DIL_TQ)
    out = []
    for edge in range(4):
        ok = band
        if edge & 1:
            ok = ok & (c >= DIL_TQ // 2)
        if edge & 2:
            ok = ok & (c < 2 * DIL_TQ - DIL_TQ // 2)
        out.append(np.tile(np.where(ok, 0.0, NEG), (2, 1)))
    return jnp.asarray(np.stack(out), _F32)


def _rope_tables(seq):
    pos = jnp.arange(seq, dtype=_F32)[:, None]

    def cs(half):
        inv = ROPE_THETA ** (-jnp.arange(half, dtype=_F32) / half)
        ang = pos * inv[None, :]
        return jnp.cos(ang), jnp.sin(ang)

    c16, s16 = cs(MLA_ROPE // 2)
    c32, s32 = cs(HEAD_DIM // 2)
    one = jnp.ones((seq, 64), _F32)
    z16 = jnp.zeros((seq, 16), _F32)
    z32 = jnp.zeros((seq, 32), _F32)
    z64 = jnp.zeros((seq, 64), _F32)
    return {
        'q_c': jnp.concatenate([one, c16, c16, z32], axis=1),
        'q_s1': jnp.concatenate([z64, z16, s16, z32], axis=1),
        'q_s2': jnp.concatenate([z64, -s16, z16, z32], axis=1),
        'k_c': jnp.concatenate([z64, c16, c16, z32], axis=1),
        'k_s': jnp.concatenate([z64, -s16, s16, z32], axis=1),
        'd_c': jnp.concatenate([c32, c32, c32, c32], axis=1),
        'd_s': jnp.concatenate([-s32, -s32, s32, s32], axis=1),
    }


def _trunk_layer(x, mem, lw, tabs, band, batch, seq):
    qm, km, vm, qn, kn, vn, qd, kd, vd = _mix_in(x, lw, tabs, seq)
    oa = _mla(qm, km, vm, batch, seq)
    ob = _na(qn, kn, vn, lw['na_bias'], batch, seq)
    oc = _dil(qd, kd, vd, band, batch, seq)
    mk, mv = _mem_kv(mem, lw)
    x = _mix_out(x, oa, ob, oc, mk, mv, lw, seq)
    return _ffn(x, lw, seq)


def kernel(x_prompt, x_sample, mem_prompt, mem_sample, norm_mix, w_in, mla_q_norm, mla_kv_norm, w_uq, w_uk, w_uv, mla_qn, mla_kn, na_qn, na_kn, na_rpb, dil_qn, dil_kn, w_o, norm_cross, norm_mem, w_cq, w_ckv, x_qn, x_kn, w_co, norm_ffn, w_up, conv_w, conv_b, w_down):
    stacked = {
        'norm_mix': norm_mix, 'w_in': w_in, 'mla_q_norm': mla_q_norm, 'mla_kv_norm': mla_kv_norm,
        'w_uq': w_uq, 'w_uk': w_uk, 'w_uv': w_uv, 'mla_qn': mla_qn, 'mla_kn': mla_kn,
        'na_qn': na_qn, 'na_kn': na_kn, 'na_rpb': na_rpb, 'dil_qn': dil_qn, 'dil_kn': dil_kn,
        'w_o': w_o, 'norm_cross': norm_cross, 'norm_mem': norm_mem, 'w_cq': w_cq, 'w_ckv': w_ckv,
        'x_qn': x_qn, 'x_kn': x_kn, 'w_co': w_co, 'norm_ffn': norm_ffn, 'w_up': w_up,
        'conv_w': conv_w, 'conv_b': conv_b, 'w_down': w_down,
    }
    layers = [_prep_layer({k: v[l] for k, v in stacked.items()}) for l in range(DEPTH)]
    band = _band_bias()
    outs = []
    for x, mem in ((x_prompt, mem_prompt), (x_sample, mem_sample)):
        batch, seq, _ = x.shape
        tabs = _rope_tables(seq)
        xt = x.reshape(batch * seq, D_MODEL)
        for lw in layers:
            xt = _trunk_layer(xt, mem, lw, tabs, band, batch, seq)
        outs.append(xt.reshape(batch, seq, D_MODEL))
    return tuple(outs)
```

```python
import functools

import jax
import jax.numpy as jnp
import numpy as np
from jax import lax
from jax.experimental import pallas as pl
from jax.experimental.pallas import tpu as pltpu

D_MODEL = 1024
DEPTH = 4
GRID_W = 64
HEAD_DIM = 64
ROPE_THETA = 10000.0
EPS = 1e-6
MLA_HEADS = 6
MLA_Q_RANK = 256
MLA_KV_RANK = 128
MLA_NOPE = 64
MLA_ROPE = 32
MLA_V = 64
MLA_QK = MLA_NOPE + MLA_ROPE
NA_HEADS = 6
NA_WIN_R = 8
NA_WIN_C = 16
DIL_WINDOWS = (128, 512, 2048)
DIL_RATES = (1, 4, 16)
DIL_GROUPS = 3
DIL_HEADS = 4
DIL_ALL = DIL_GROUPS * DIL_HEADS
X_HEADS = 4
X_HEAD_DIM = D_MODEL // X_HEADS
D_FF = 2816
CONV_W = 3
SPLIT_CQ = MLA_Q_RANK
SPLIT_CKV = SPLIT_CQ + MLA_KV_RANK
SPLIT_KR = SPLIT_CKV + MLA_ROPE
SPLIT_NA = SPLIT_KR + 3 * NA_HEADS * HEAD_DIM
D_IN = SPLIT_NA + 3 * DIL_ALL * HEAD_DIM

LANES = 128
NEG = -1e30
LOG2E = 1.4426950408889634
TM = 512
DIL_TQ = 128
DIL_SB = DIL_TQ * max(DIL_RATES)
DIL_UNROLL = 4
NA_RB = 8
FF_CH = 256
N_FF_CH = D_FF // FF_CH
VMEM_LIMIT = 56 * 1024 * 1024

COL_MLA = 0
COL_NA = 512
COL_DIL = COL_NA + 3 * NA_HEADS * HEAD_DIM
D_IN_P = COL_DIL + 3 * DIL_ALL * HEAD_DIM

_BF = jnp.bfloat16
_F32 = jnp.float32


def _dot(a, b):
    return jnp.dot(a, b, preferred_element_type=_F32)


def _dot_t(a, b):
    return lax.dot_general(a, b, (((1,), (1,)), ((), ())), preferred_element_type=_F32)


def _rms(x, g):
    ms = jnp.mean(x * x, axis=-1, keepdims=True)
    return x * lax.rsqrt(ms + EPS) * g


def _lane(shape):
    return lax.broadcasted_iota(jnp.int32, shape, len(shape) - 1)


def _head_join_matrices():
    i = np.arange(2 * LANES)
    same_group = (i[:, None] // LANES) == (i[None, :] // LANES)
    natural = (i[:, None] // HEAD_DIM) == (i[None, :] // HEAD_DIM)
    half = ((i % HEAD_DIM) < HEAD_DIM // 2)
    rotary = same_group & (half[:, None] == half[None, :])
    return [jnp.asarray(m, _BF) for m in (same_group, natural, rotary)]


def _seg_inv(x2, bd_ref, dim):
    ss = _dot((x2 * x2).astype(_BF), bd_ref[...])
    return lax.rsqrt(ss * (1.0 / dim) + EPS)


def _mix_in_kernel(x_ref, gmix_ref, win_ref, gcq_ref, gckv_ref, wuq_ref, wuk_ref, wuv_ref,
                   gmq_ref, gmk_ref, gkr_ref, gnq_ref, gnk_ref, gdq_ref, gdk_ref,
                   bd128_ref, bd64_ref, bdr_ref,
                   tqc_ref, tqs1_ref, tqs2_ref, tkc_ref, tks_ref, tdc_ref, tds_ref,
                   qm_ref, km_ref, vm_ref, qn_ref, kn_ref, vn_ref, qd_ref, kd_ref, vd_ref,
                   perm_ref):
    tm = x_ref.shape[0]
    h = _rms(x_ref[...], gmix_ref[...]).astype(_BF)
    lane = _lane((tm, LANES))

    z = _dot(h, win_ref[:, COL_MLA:COL_NA])
    cq = _rms(z[:, 0:MLA_Q_RANK], gcq_ref[...]).astype(_BF)
    ckv = _rms(z[:, MLA_Q_RANK:MLA_Q_RANK + MLA_KV_RANK], gckv_ref[...]).astype(_BF)
    kr = z[:, 384:512]
    kr_own = jnp.where((lane >= MLA_NOPE) & (lane < MLA_QK), kr, 0.0)
    krg = kr * gkr_ref[...]
    kr_rot = krg * tkc_ref[...] + pltpu.roll(krg, 64, 1) * tks_ref[...]
    qpre = _dot(cq, wuq_ref[...])
    kpre = _dot(ckv, wuk_ref[...])
    vpre = _dot(ckv, wuv_ref[...])
    row_t = lax.broadcasted_iota(jnp.int32, (LANES, tm), 0)
    for c in range(MLA_HEADS // 2):
        q2 = qpre[:, 2 * c * LANES:(2 * c + 2) * LANES]
        k2 = kpre[:, 2 * c * LANES:(2 * c + 2) * LANES]
        inv_q = _seg_inv(q2, bd128_ref, MLA_QK)
        inv_k = _seg_inv(k2 + jnp.concatenate([kr_own, kr_own], axis=1), bd128_ref, MLA_QK)
        for hh in range(2):
            hd = 2 * c + hh
            sl = slice(hh * LANES, (hh + 1) * LANES)
            qg = q2[:, sl] * gmq_ref[...]
            qr = qg * tqc_ref[...] + pltpu.roll(qg, 16, 1) * tqs1_ref[...] + pltpu.roll(qg, 112, 1) * tqs2_ref[...]
            qm_ref[hd] = (qr * inv_q[:, sl]).astype(_BF)
            km_ref[hd] = ((k2[:, sl] * gmk_ref[...] + kr_rot) * inv_k[:, sl]).astype(_BF)
            vt = vpre[:, hd * LANES:(hd + 1) * LANES].T
            vm_ref[hd, 0] = jnp.where(row_t == MLA_V, 1.0, vt).astype(_BF)

    z = _dot(h, win_ref[:, COL_NA:COL_DIL])
    q2 = z[:, 0:256]
    k2 = z[:, 256:512]
    qk2 = z[:, 768:1024]
    qn2 = q2 * _seg_inv(q2, bd64_ref, HEAD_DIM)
    kn2 = k2 * _seg_inv(k2, bd64_ref, HEAD_DIM)
    qkn2 = qk2 * _seg_inv(qk2, bd64_ref, HEAD_DIM)
    for p in range(2):
        sl = slice(p * LANES, (p + 1) * LANES)
        qn_ref[p] = (qn2[:, sl] * gnq_ref[...]).astype(_BF)
        kn_ref[p] = (kn2[:, sl] * gnk_ref[...]).astype(_BF)
        vn_ref[p] = z[:, 512 + p * LANES: 512 + (p + 1) * LANES].astype(_BF)
    qn_ref[2] = (qkn2[:, :LANES] * gnq_ref[...]).astype(_BF)
    kn_ref[2] = (qkn2[:, LANES:] * gnk_ref[...]).astype(_BF)
    vn_ref[2] = z[:, 1024:1152].astype(_BF)

    def put(out_ref, p, val, rate):
        if rate == 1:
            out_ref[p] = val.astype(_BF)
        else:
            rows = tm // rate
            perm_ref[...] = val
            for rho in range(rate):
                out_ref[p, rho * rows:(rho + 1) * rows, :] = perm_ref[pl.ds(rho, rows, stride=rate), :].astype(_BF)

    for g in range(DIL_GROUPS):
        z = _dot(h, win_ref[:, COL_DIL + g * 768: COL_DIL + (g + 1) * 768])
        q2 = z[:, 0:256]
        k2 = z[:, 256:512]
        qn2 = q2 * _seg_inv(q2, bdr_ref, HEAD_DIM)
        kn2 = k2 * _seg_inv(k2, bdr_ref, HEAD_DIM)
        for pp in range(2):
            p = 2 * g + pp
            sl = slice(pp * LANES, (pp + 1) * LANES)
            qx = qn2[:, sl] * gdq_ref[...]
            qx = qx * tdc_ref[...] + pltpu.roll(qx, 64, 1) * tds_ref[...]
            kx = kn2[:, sl] * gdk_ref[...]
            kx = kx * tdc_ref[...] + pltpu.roll(kx, 64, 1) * tds_ref[...]
            put(qd_ref, p, qx, DIL_RATES[g])
            put(kd_ref, p, kx, DIL_RATES[g])
            put(vd_ref, p, z[:, 512 + pp * LANES: 512 + (pp + 1) * LANES], DIL_RATES[g])


def _const_spec(shape):
    nd = len(shape)
    return pl.BlockSpec(shape, lambda *_, _nd=nd: (0,) * _nd)


def _layer_spec(arr, layer):
    nd = arr.ndim
    return pl.BlockSpec((None,) + arr.shape[1:], lambda *_, _l=layer, _nd=nd: (_l,) + (0,) * (_nd - 1))


def _mix_in(x, lw, layer, tabs, seq):
    t = x.shape[0]
    nblk_seq = seq // TM
    tab_spec = pl.BlockSpec((TM, LANES), lambda i: (i % nblk_seq, 0))
    weights = [lw[n] for n in ('g_mix', 'w_in', 'g_cq', 'g_ckv', 'w_uq', 'w_uk', 'w_uv',
                               'g_mq', 'g_mk', 'g_kr', 'g_nq', 'g_nk', 'g_dq', 'g_dk')]
    joins = _head_join_matrices()
    tables = [tabs['q_c'], tabs['q_s1'], tabs['q_s2'], tabs['k_c'], tabs['k_s'], tabs['d_c'], tabs['d_s']]

    def out(n):
        return (jax.ShapeDtypeStruct((n, t, LANES), _BF), pl.BlockSpec((n, TM, LANES), lambda i: (0, i, 0)))

    vt_out = (jax.ShapeDtypeStruct((MLA_HEADS, t // TM, LANES, TM), _BF),
              pl.BlockSpec((MLA_HEADS, 1, LANES, TM), lambda i: (0, i, 0, 0)))
    outs = [out(6), out(6), vt_out, out(3), out(3), out(3), out(6), out(6), out(6)]
    return pl.pallas_call(
        _mix_in_kernel,
        grid=(t // TM,),
        in_specs=[pl.BlockSpec((TM, D_MODEL), lambda i: (i, 0))]
        + [_layer_spec(w, layer) for w in weights] + [_const_spec(m.shape) for m in joins]
        + [tab_spec] * len(tables),
        out_specs=[o[1] for o in outs],
        out_shape=[o[0] for o in outs],
        scratch_shapes=[pltpu.VMEM((TM, LANES), _F32)],
        compiler_params=pltpu.CompilerParams(dimension_semantics=("parallel",), vmem_limit_bytes=VMEM_LIMIT),
        name="mix_in",
    )(x, *weights, *joins, *tables)


def _mla_kernel(q_ref, k_ref, vt_ref, o_ref, sa_sc, sb_sc, m_sc, acc_sc, *, tk):
    tq = q_ref.shape[1]
    nk = k_ref.shape[1] // tk
    sub = tk // TM
    m_sc[...] = jnp.full(m_sc.shape, NEG, _F32)
    acc_sc[...] = jnp.zeros(acc_sc.shape, _F32)

    def scores(j, dst):
        rows = pl.ds(pl.multiple_of(j * tk, tk), tk)
        for hh in range(2):
            dst[hh] = _dot_t(k_ref[hh, rows, :], q_ref[hh])

    def consume(j, src):
        for hh in range(2):
            s = src[hh]
            m_prev = m_sc[hh]
            m_new = jnp.maximum(m_prev, jnp.max(s, axis=0, keepdims=True))
            p = jnp.exp2(s - m_new).astype(_BF)
            vt = jnp.concatenate([vt_ref[hh, j * sub + c] for c in range(sub)], axis=1)
            acc_sc[hh] = jnp.exp2(m_prev - m_new) * acc_sc[hh] + _dot(vt, p)
            m_sc[hh] = m_new

    def pair(i, carry):
        j = 2 * i
        scores(j + 1, sb_sc)
        consume(j, sa_sc)
        scores(j + 2, sa_sc)
        consume(j + 1, sb_sc)
        return carry

    scores(0, sa_sc)
    lax.fori_loop(0, nk // 2 - 1, pair, 0)
    scores(nk - 1, sb_sc)
    consume(nk - 2, sa_sc)
    consume(nk - 1, sb_sc)

    lane = _lane((tq, LANES))
    outs = []
    for hh in range(2):
        acc = acc_sc[hh]
        outs.append((acc / acc[MLA_V:MLA_V + 1, :]).T)
    o_ref[0] = jnp.where(lane < MLA_V, outs[0], pltpu.roll(outs[1], MLA_V, 1)).astype(o_ref.dtype)


def _mla(q, k, vt, batch, seq, tq=1024, tk=512):
    t = q.shape[1]
    nq = seq // tq
    nblk = seq // TM
    assert seq % (2 * tk) == 0 and tk % TM == 0
    return pl.pallas_call(
        functools.partial(_mla_kernel, tk=tk),
        grid=(batch, MLA_HEADS // 2, nq),
        in_specs=[pl.BlockSpec((2, tq, LANES), lambda b, p, i: (p, b * nq + i, 0)),
                  pl.BlockSpec((2, seq, LANES), lambda b, p, i: (p, b, 0)),
                  pl.BlockSpec((2, nblk, LANES, TM), lambda b, p, i: (p, b, 0, 0))],
        out_specs=pl.BlockSpec((1, tq, LANES), lambda b, p, i: (p, b * nq + i, 0)),
        out_shape=jax.ShapeDtypeStruct((MLA_HEADS // 2, t, LANES), _BF),
        scratch_shapes=[pltpu.VMEM((2, tk, tq), _F32), pltpu.VMEM((2, tk, tq), _F32),
                        pltpu.VMEM((2, 1, tq), _F32), pltpu.VMEM((2, LANES, tq), _F32)],
        compiler_params=pltpu.CompilerParams(
            dimension_semantics=("parallel", "parallel", "arbitrary"), vmem_limit_bytes=VMEM_LIMIT),
        name="mla_attention",
    )(q, k, vt)


def _na_kernel(q_ref, k_ref, v_ref, bias_ref, o_ref, *, rows):
    i = pl.program_id(2)
    lane = _lane((GRID_W, LANES))
    first = lane < HEAD_DIM
    nkeys = NA_WIN_R * GRID_W

    scores, windows = [], []
    for rr in range(NA_RB):
        r = i * NA_RB + rr
        rs = jnp.clip(r - NA_WIN_R // 2, 0, rows - NA_WIN_R)
        start = pl.multiple_of(rs * GRID_W, GRID_W)
        q = q_ref[0, rr * GRID_W:(rr + 1) * GRID_W, :]
        zero = jnp.zeros_like(q)
        q2 = jnp.concatenate([jnp.where(first, q, zero), jnp.where(first, zero, q)], axis=0)
        scores.append(_dot_t(q2, k_ref[0, pl.ds(start, nkeys), :]) + bias_ref[0, r - rs])
        windows.append(start)
    probs = []
    for s in scores:
        p = jnp.exp(s - jnp.max(s, axis=1, keepdims=True))
        probs.append((p.astype(_BF), jnp.sum(p, axis=1, keepdims=True)))
    for rr, ((p, l), start) in enumerate(zip(probs, windows)):
        o2 = _dot(p, v_ref[0, pl.ds(start, nkeys), :]) / l
        o = jnp.where(first, o2[:GRID_W], o2[GRID_W:])
        o_ref[0, rr * GRID_W:(rr + 1) * GRID_W, :] = o.astype(o_ref.dtype)


def _na(q, k, v, bias, layer, batch, seq):
    t = q.shape[1]
    rows = seq // GRID_W
    assert rows >= NA_WIN_R and rows % NA_RB == 0
    nrb = rows // NA_RB
    qrows = NA_RB * GRID_W
    return pl.pallas_call(
        functools.partial(_na_kernel, rows=rows),
        grid=(batch, NA_HEADS // 2, nrb),
        in_specs=[pl.BlockSpec((1, qrows, LANES), lambda b, p, i: (p, b * nrb + i, 0)),
                  pl.BlockSpec((1, seq, LANES), lambda b, p, i: (p, b, 0)),
                  pl.BlockSpec((1, seq, LANES), lambda b, p, i: (p, b, 0)),
                  pl.BlockSpec((None, 1, NA_WIN_R, 2 * GRID_W, NA_WIN_R * GRID_W),
                               lambda b, p, i: (layer, p, 0, 0, 0))],
        out_specs=pl.BlockSpec((1, qrows, LANES), lambda b, p, i: (p, b * nrb + i, 0)),
        out_shape=jax.ShapeDtypeStruct((NA_HEADS // 2, t, LANES), _BF),
        compiler_params=pltpu.CompilerParams(
            dimension_semantics=("parallel", "parallel", "arbitrary"), vmem_limit_bytes=VMEM_LIMIT),
        name="na_attention",
    )(q, k, v, bias)


def _dil_kernel(q0_ref, q1_ref, q2_ref, k0_ref, k1_ref, k2_ref, v0_ref, v1_ref, v2_ref, band_ref,
                o_ref, og_sc, lse_sc, *, seq):
    sb = pl.program_id(2)
    q_refs = (q0_ref, q1_ref, q2_ref)
    k_refs = (k0_ref, k1_ref, k2_ref)
    v_refs = (v0_ref, v1_ref, v2_ref)
    lane = _lane((DIL_TQ, LANES))
    first_qk = (lane % HEAD_DIM) < (HEAD_DIM // 2)
    first_v = lane < HEAD_DIM
    n_tiles = DIL_SB // DIL_TQ
    half = DIL_TQ // 2

    for g in range(DIL_GROUPS):
        rate = DIL_RATES[g]
        length = seq // rate
        per_blk = TM // rate
        chunk = min(half, per_blk)
        n_chunks = (2 * DIL_TQ) // chunk
        tiles_per_res = n_tiles // rate
        q_ref, k_ref, v_ref = q_refs[g], k_refs[g], v_refs[g]

        def row_of(m, rho, _per_blk=per_blk):
            return (m // _per_blk) * TM + rho * _per_blk + (m % _per_blk)

        def body(tg, carry, _g=g, _rate=rate, _length=length, _per_blk=per_blk, _chunk=chunk,
                 _n_chunks=n_chunks, _tpr=tiles_per_res, _q=q_ref, _k=k_ref, _v=v_ref):
            scores, metas = [], []
            for u in range(DIL_UNROLL):
                tt = tg * DIL_UNROLL + u
                rho = tt // _tpr
                mt = tt % _tpr
                m0 = sb * (DIL_SB // _rate) + mt * DIL_TQ
                q = jnp.concatenate(
                    [_q[0, pl.ds(pl.multiple_of(row_of(m0 + c * _chunk, rho), _chunk), _chunk), :]
                     for c in range(DIL_TQ // _chunk)], axis=0)
                zero = jnp.zeros_like(q)
                q2 = jnp.concatenate([jnp.where(first_qk, q, zero), jnp.where(first_qk, zero, q)], axis=0)
                win = []
                for c in range(_n_chunks):
                    ms = jnp.clip(m0 - half + c * _chunk, 0, _length - _chunk)
                    win.append(pl.multiple_of(row_of(ms, rho), _chunk))
                kw = jnp.concatenate([_k[0, pl.ds(row, _chunk), :] for row in win], axis=0)
                edge = (m0 == 0).astype(jnp.int32) + 2 * (m0 == _length - DIL_TQ).astype(jnp.int32)
                scores.append(_dot_t(q2, kw) + band_ref[edge])
                metas.append((rho + _rate * mt * DIL_TQ, win))
            probs = []
            for s in scores:
                m = jnp.max(s, axis=1, keepdims=True)
                p = jnp.exp(s - m)
                probs.append((p.astype(_BF), m, jnp.sum(p, axis=1, keepdims=True)))
            for (p, m, l), (base, win) in zip(probs, metas):
                vw = jnp.concatenate([_v[0, pl.ds(row, _chunk), :] for row in win], axis=0)
                o2 = _dot(p, vw) / l
                lse2 = jnp.broadcast_to(m + jnp.log(l), (2 * DIL_TQ, LANES))
                if _rate == 1:
                    dst = pl.ds(pl.multiple_of(base, DIL_TQ), DIL_TQ)
                else:
                    dst = pl.ds(base, DIL_TQ, stride=_rate)
                og_sc[_g, dst, :] = jnp.where(first_v, o2[:DIL_TQ], o2[DIL_TQ:])
                lse_sc[_g, dst, :] = jnp.where(first_v, lse2[:DIL_TQ], lse2[DIL_TQ:])
            return carry

        lax.fori_loop(0, n_tiles // DIL_UNROLL, body, 0)

    m = jnp.maximum(jnp.maximum(lse_sc[0], lse_sc[1]), lse_sc[2])
    w0 = jnp.exp(lse_sc[0] - m)
    w1 = jnp.exp(lse_sc[1] - m)
    w2 = jnp.exp(lse_sc[2] - m)
    o = (w0 * og_sc[0] + w1 * og_sc[1] + w2 * og_sc[2]) / (w0 + w1 + w2)
    o_ref[0] = o.astype(o_ref.dtype)


def _dil(q, k, v, band, batch, seq):
    t = q.shape[1]
    assert seq % DIL_SB == 0
    nsb = seq // DIL_SB

    def q_spec(g):
        return pl.BlockSpec((1, DIL_SB, LANES), lambda b, sp, s, _g=g: (2 * _g + sp, b * nsb + s, 0))

    def kv_spec(g):
        return pl.BlockSpec((1, seq, LANES), lambda b, sp, s, _g=g: (2 * _g + sp, b, 0))

    return pl.pallas_call(
        functools.partial(_dil_kernel, seq=seq),
        grid=(batch, 2, nsb),
        in_specs=[kv_spec(0), kv_spec(1), kv_spec(2), kv_spec(0), kv_spec(1), kv_spec(2),
                  kv_spec(0), kv_spec(1), kv_spec(2),
                  pl.BlockSpec((4, 2 * DIL_TQ, 2 * DIL_TQ), lambda b, sp, s: (0, 0, 0))],
        out_specs=pl.BlockSpec((1, DIL_SB, LANES), lambda b, sp, s: (sp, b * nsb + s, 0)),
        out_shape=jax.ShapeDtypeStruct((2, t, LANES), _BF),
        scratch_shapes=[pltpu.VMEM((DIL_GROUPS, DIL_SB, LANES), _F32),
                        pltpu.VMEM((DIL_GROUPS, DIL_SB, LANES), _F32)],
        compiler_params=pltpu.CompilerParams(
            dimension_semantics=("parallel", "parallel", "arbitrary"), vmem_limit_bytes=VMEM_LIMIT),
        name="dil_attention",
    )(q, q, q, k, k, k, v, v, v, band)


def _mem_kv_kernel(mem_ref, gmem_ref, wckv_ref, gxk_ref, k_ref, v_ref):
    hm = _rms(mem_ref[0], gmem_ref[...]).astype(_BF)
    kv = _dot(hm, wckv_ref[...])
    for hd in range(X_HEADS):
        kh = kv[:, hd * X_HEAD_DIM:(hd + 1) * X_HEAD_DIM]
        k_ref[0, :, hd * X_HEAD_DIM:(hd + 1) * X_HEAD_DIM] = _rms(kh, gxk_ref[...]).astype(_BF)
    v_ref[0] = kv[:, D_MODEL:].astype(_BF)


def _mem_kv(mem, lw, layer):
    b, m, _ = mem.shape
    return pl.pallas_call(
        _mem_kv_kernel,
        grid=(b,),
        in_specs=[pl.BlockSpec((1, m, D_MODEL), lambda i: (i, 0, 0))]
        + [_layer_spec(lw[n], layer) for n in ('g_mem', 'w_ckv', 'g_xk')],
        out_specs=[pl.BlockSpec((1, m, D_MODEL), lambda i: (i, 0, 0))] * 2,
        out_shape=[jax.ShapeDtypeStruct((b, m, D_MODEL), _BF)] * 2,
        compiler_params=pltpu.CompilerParams(dimension_semantics=("parallel",), vmem_limit_bytes=VMEM_LIMIT),
        name="mem_kv",
    )(mem, lw['g_mem'], lw['w_ckv'], lw['g_xk'])


def _mix_out_kernel(x_ref, oa_ref, ob_ref, oc_ref, wo_ref, gcross_ref, wcq_ref, gxq_ref,
                    mk_ref, mv_ref, wco_ref, y_ref):
    mix = jnp.concatenate([oa_ref[p] for p in range(3)] + [ob_ref[p] for p in range(3)]
                          + [oc_ref[p] for p in range(2)], axis=1)
    x1 = x_ref[...] + _dot(mix, wo_ref[...])
    hc = _rms(x1, gcross_ref[...]).astype(_BF)
    q = _dot(hc, wcq_ref[...])
    outs = []
    for hd in range(X_HEADS):
        sl = slice(hd * X_HEAD_DIM, (hd + 1) * X_HEAD_DIM)
        qh = (_rms(q[:, sl], gxq_ref[...]) * X_HEAD_DIM ** -0.5).astype(_BF)
        s = _dot_t(qh, mk_ref[0, :, sl])
        m = jnp.max(s, axis=1, keepdims=True)
        p = jnp.exp(s - m)
        l = jnp.sum(p, axis=1, keepdims=True)
        outs.append((_dot(p.astype(_BF), mv_ref[0, :, sl]) / l).astype(_BF))
    o = jnp.concatenate(outs, axis=1)
    y_ref[...] = x1 + _dot(o, wco_ref[...])


def _mix_out(x, oa, ob, oc, mk, mv, lw, layer, seq):
    t = x.shape[0]
    nblk_seq = seq // TM
    mem_len = mk.shape[1]

    def grp(n):
        return pl.BlockSpec((n, TM, LANES), lambda i: (0, i, 0))

    mem_spec = pl.BlockSpec((1, mem_len, D_MODEL), lambda i: (i // nblk_seq, 0, 0))
    return pl.pallas_call(
        _mix_out_kernel,
        grid=(t // TM,),
        in_specs=[pl.BlockSpec((TM, D_MODEL), lambda i: (i, 0)), grp(3), grp(3), grp(2),
                  _layer_spec(lw['w_o'], layer), _layer_spec(lw['g_cross'], layer),
                  _layer_spec(lw['w_cq'], layer), _layer_spec(lw['g_xq'], layer),
                  mem_spec, mem_spec, _layer_spec(lw['w_co'], layer)],
        out_specs=pl.BlockSpec((TM, D_MODEL), lambda i: (i, 0)),
        out_shape=jax.ShapeDtypeStruct((t, D_MODEL), _F32),
        compiler_params=pltpu.CompilerParams(dimension_semantics=("parallel",), vmem_limit_bytes=VMEM_LIMIT),
        name="mix_out",
    )(x, oa, ob, oc, lw['w_o'], lw['g_cross'], lw['w_cq'], lw['g_xq'], mk, mv, lw['w_co'])


def _ffn_kernel(x_ref, xp_ref, xn_ref, gffn_ref, wup_ref, cw_ref, cb_ref, wdn_ref, y_ref, *, nblk_seq):
    i = pl.program_id(0)
    tm = x_ref.shape[0]
    x = x_ref[...]
    g = gffn_ref[...]
    is_first = (i % nblk_seq) == 0
    is_last = (i % nblk_seq) == nblk_seq - 1
    hp = jnp.where(is_first, 0.0, _rms(xp_ref[...], g))
    hn = jnp.where(is_last, 0.0, _rms(xn_ref[...], g))
    h = jnp.concatenate([hp, _rms(x, g), hn], axis=0).astype(_BF)
    acc = x
    pending = []
    u_next = _dot(h, wup_ref[0])
    for c in range(N_FF_CH):
        u = u_next
        if c + 1 < N_FF_CH:
            u_next = _dot(h, wup_ref[c + 1])
        cw = cw_ref[c]
        up = pltpu.roll(u, 1, 0)[8:8 + tm]
        un = pltpu.roll(u, tm + 15, 0)[8:8 + tm]
        uc = up * cw[0:1] + u[8:8 + tm] * cw[1:2] + un * cw[2:3] + cb_ref[c]
        a = uc[:, :FF_CH]
        half_gate = 0.5 * uc[:, FF_CH:]
        pending.append((half_gate * (1.0 + jnp.tanh(half_gate)) * a).astype(_BF))
        rest = N_FF_CH - 1 - c
        if len(pending) == 2 and rest != 1 or rest == 0:
            lo = (c + 1 - len(pending)) * FF_CH
            act = pending[0] if len(pending) == 1 else jnp.concatenate(pending, axis=1)
            acc = acc + _dot(act, wdn_ref[lo:(c + 1) * FF_CH, :])
            pending = []
    y_ref[...] = acc


def _ffn(x, lw, layer, seq):
    t = x.shape[0]
    nblk_seq = seq // TM
    nhalo = t // 8
    per = TM // 8
    return pl.pallas_call(
        functools.partial(_ffn_kernel, nblk_seq=nblk_seq),
        grid=(t // TM,),
        in_specs=[pl.BlockSpec((TM, D_MODEL), lambda i: (i, 0)),
                  pl.BlockSpec((8, D_MODEL), lambda i: (jnp.maximum(i * per - 1, 0), 0)),
                  pl.BlockSpec((8, D_MODEL), lambda i: (jnp.minimum((i + 1) * per, nhalo - 1), 0))]
        + [_layer_spec(lw[n], layer) for n in ('g_ffn', 'w_up', 'conv_w', 'conv_b', 'w_down')],
        out_specs=pl.BlockSpec((TM, D_MODEL), lambda i: (i, 0)),
        out_shape=jax.ShapeDtypeStruct((t, D_MODEL), _F32),
        compiler_params=pltpu.CompilerParams(dimension_semantics=("parallel",), vmem_limit_bytes=VMEM_LIMIT),
        name="conv_ffn",
    )(x, x, x, lw['g_ffn'], lw['w_up'], lw['conv_w'], lw['conv_b'], lw['w_down'])


def _relayout_w_in(w):
    nl = w.shape[0]
    lead = w[..., :SPLIT_CKV]
    a = w[..., SPLIT_CKV:SPLIT_CKV + 16]
    b = w[..., SPLIT_CKV + 16:SPLIT_KR]
    z32 = jnp.zeros(a.shape[:-1] + (32,), w.dtype)
    kr = jnp.concatenate([b, a, z32, a, b, z32], axis=-1)
    na = w[..., SPLIT_KR:SPLIT_NA].reshape(nl, D_MODEL, 3, NA_HEADS // 2, LANES)
    na_ab = na[:, :, :, 0:2].reshape(nl, D_MODEL, 3 * 2 * LANES)
    na_c = na[:, :, :, 2].reshape(nl, D_MODEL, 3 * LANES)
    dl = w[..., SPLIT_NA:].reshape(nl, D_MODEL, 3, DIL_GROUPS, 2, 2, 2, HEAD_DIM // 2)
    qk = dl[:, :, 0:2].transpose(0, 1, 3, 2, 4, 6, 5, 7).reshape(nl, D_MODEL, DIL_GROUPS, 4 * LANES)
    v = dl[:, :, 2].reshape(nl, D_MODEL, DIL_GROUPS, 2 * LANES)
    dil = jnp.concatenate([qk, v], axis=-1).reshape(nl, D_MODEL, DIL_GROUPS * 6 * LANES)
    out = jnp.concatenate([lead, kr, na_ab, na_c, dil], axis=-1)
    assert out.shape[-1] == D_IN_P
    return out


def _pad_heads(w, real, n_heads):
    nl, k, _ = w.shape
    w = w.reshape(nl, k, n_heads, real)
    return jnp.pad(w, ((0, 0), (0, 0), (0, 0), (0, LANES - real))).reshape(nl, k, n_heads * LANES)


def _rows(v):
    return v[:, None, :].astype(_F32)


def _prep_layers(p):
    nl = p['w_in'].shape[0]
    kn, qn = p['mla_kn'], p['mla_qn']
    z32 = jnp.zeros((nl, 32), _F32)
    z64 = jnp.zeros((nl, 64), _F32)
    dq, dk = p['dil_qn'], p['dil_kn']

    def rot(gain):
        return jnp.concatenate([gain[:, :32], gain[:, :32], gain[:, 32:], gain[:, 32:]], axis=-1)

    w_up = p['w_up'].astype(_BF).reshape(nl, D_MODEL, 2, N_FF_CH, FF_CH).transpose(0, 3, 1, 2, 4)
    conv_w = p['conv_w'].reshape(nl, CONV_W, 2, N_FF_CH, FF_CH).transpose(0, 3, 1, 2, 4)
    conv_b = p['conv_b'].reshape(nl, 2, N_FF_CH, FF_CH).transpose(0, 2, 1, 3)
    return {
        'g_mix': _rows(p['norm_mix']), 'w_in': _relayout_w_in(p['w_in'].astype(_BF)),
        'g_cq': _rows(p['mla_q_norm']), 'g_ckv': _rows(p['mla_kv_norm']),
        'w_uq': _pad_heads(p['w_uq'].astype(_BF), MLA_QK, MLA_HEADS),
        'w_uk': _pad_heads(p['w_uk'].astype(_BF), MLA_NOPE, MLA_HEADS),
        'w_uv': _pad_heads(p['w_uv'].astype(_BF), MLA_V, MLA_HEADS),
        'g_mq': _rows(jnp.concatenate([qn, z32], axis=-1) * (MLA_QK ** -0.5 * LOG2E)),
        'g_mk': _rows(jnp.concatenate([kn[:, :MLA_NOPE], z64], axis=-1)),
        'g_kr': _rows(jnp.concatenate([kn[:, 80:96], kn[:, 64:80], z32, kn[:, 64:80], kn[:, 80:96], z32], axis=-1)),
        'g_nq': _rows(jnp.tile(p['na_qn'], (1, 2)) * HEAD_DIM ** -0.5), 'g_nk': _rows(jnp.tile(p['na_kn'], (1, 2))),
        'g_dq': _rows(rot(dq) * HEAD_DIM ** -0.5), 'g_dk': _rows(rot(dk)),
        'w_o': p['w_o'].astype(_BF),
        'g_cross': _rows(p['norm_cross']), 'w_cq': p['w_cq'].astype(_BF),
        'g_xq': _rows(p['x_qn']), 'g_xk': _rows(p['x_kn']), 'g_mem': _rows(p['norm_mem']),
        'w_ckv': p['w_ckv'].astype(_BF), 'w_co': p['w_co'].astype(_BF),
        'g_ffn': _rows(p['norm_ffn']), 'w_up': w_up.reshape(nl, N_FF_CH, D_MODEL, 2 * FF_CH),
        'conv_w': conv_w.reshape(nl, N_FF_CH, CONV_W, 2 * FF_CH).astype(_F32),
        'conv_b': conv_b.reshape(nl, N_FF_CH, 1, 2 * FF_CH).astype(_F32),
        'w_down': p['w_down'].astype(_BF),
        'na_bias': _na_bias(p['na_rpb']),
    }


def _na_bias(rpb):
    nl = rpb.shape[0]
    dl = np.arange(NA_WIN_R)
    j = np.arange(NA_WIN_R)
    r_off = j[None, :] - dl[:, None] + (NA_WIN_R - 1)
    qc = np.arange(GRID_W)
    kc = np.arange(GRID_W)
    c_start = np.clip(qc - NA_WIN_C // 2, 0, GRID_W - NA_WIN_C)
    valid = (kc[None, :] >= c_start[:, None]) & (kc[None, :] < c_start[:, None] + NA_WIN_C)
    c_off = np.clip(kc[None, :] - qc[:, None] + (NA_WIN_C - 1), 0, 2 * NA_WIN_C - 2)
    b = rpb[:, :, r_off]
    b = b[..., c_off]
    b = jnp.where(jnp.asarray(valid), b, NEG)
    b = b.transpose(0, 1, 2, 4, 3, 5).reshape(nl, NA_HEADS // 2, 2, NA_WIN_R, GRID_W, NA_WIN_R * GRID_W)
    b = b.transpose(0, 1, 3, 2, 4, 5).astype(_F32)
    return b.reshape(nl, NA_HEADS // 2, NA_WIN_R, 2 * GRID_W, NA_WIN_R * GRID_W)


def _band_bias():
    i = np.arange(DIL_TQ)[:, None]
    c = np.arange(2 * DIL_TQ)[None, :]
    band = (c >= i) & (c <= i + DIL_TQ)
    out = []
    for edge in range(4):
        ok = band
        if edge & 1:
            ok = ok & (c >= DIL_TQ // 2)
        if edge & 2:
            ok = ok & (c < 2 * DIL_TQ - DIL_TQ // 2)
        out.append(np.tile(np.where(ok, 0.0, NEG), (2, 1)))
    return jnp.asarray(np.stack(out), _F32)


def _rope_tables(seq):
    pos = jnp.arange(seq, dtype=_F32)[:, None]

    def cs(half):
        inv = ROPE_THETA ** (-jnp.arange(half, dtype=_F32) / half)
        ang = pos * inv[None, :]
        return jnp.cos(ang), jnp.sin(ang)

    c16, s16 = cs(MLA_ROPE // 2)
    c32, s32 = cs(HEAD_DIM // 2)
    one = jnp.ones((seq, 64), _F32)
    z16 = jnp.zeros((seq, 16), _F32)
    z32 = jnp.zeros((seq, 32), _F32)
    z64 = jnp.zeros((seq, 64), _F32)
    return {
        'q_c': jnp.concatenate([one, c16, c16, z32], axis=1),
        'q_s1': jnp.concatenate([z64, z16, s16, z32], axis=1),
        'q_s2': jnp.concatenate([z64, -s16, z16, z32], axis=1),
        'k_c': jnp.concatenate([z64, c16, c16, z32], axis=1),
        'k_s': jnp.concatenate([z64, -s16, s16, z32], axis=1),
        'd_c': jnp.concatenate([c32, c32, c32, c32], axis=1),
        'd_s': jnp.concatenate([-s32, -s32, s32, s32], axis=1),
    }


def _trunk_layer(x, mem, lw, layer, tabs, band, batch, seq):
    qm, km, vm, qn, kn, vn, qd, kd, vd = _mix_in(x, lw, layer, tabs, seq)
    oa = _mla(qm, km, vm, batch, seq)
    ob = _na(qn, kn, vn, lw['na_bias'], layer, batch, seq)
    oc = _dil(qd, kd, vd, band, batch, seq)
    mk, mv = _mem_kv(mem, lw, layer)
    x = _mix_out(x, oa, ob, oc, mk, mv, lw, layer, seq)
    return _ffn(x, lw, layer, seq)


def kernel(x_prompt, x_sample, mem_prompt, mem_sample, norm_mix, w_in, mla_q_norm, mla_kv_norm, w_uq, w_uk, w_uv, mla_qn, mla_kn, na_qn, na_kn, na_rpb, dil_qn, dil_kn, w_o, norm_cross, norm_mem, w_cq, w_ckv, x_qn, x_kn, w_co, norm_ffn, w_up, conv_w, conv_b, w_down):
    stacked = {
        'norm_mix': norm_mix, 'w_in': w_in, 'mla_q_norm': mla_q_norm, 'mla_kv_norm': mla_kv_norm,
        'w_uq': w_uq, 'w_uk': w_uk, 'w_uv': w_uv, 'mla_qn': mla_qn, 'mla_kn': mla_kn,
        'na_qn': na_qn, 'na_kn': na_kn, 'na_rpb': na_rpb, 'dil_qn': dil_qn, 'dil_kn': dil_kn,
        'w_o': w_o, 'norm_cross': norm_cross, 'norm_mem': norm_mem, 'w_cq': w_cq, 'w_ckv': w_ckv,
        'x_qn': x_qn, 'x_kn': x_kn, 'w_co': w_co, 'norm_ffn': norm_ffn, 'w_up': w_up,
        'conv_w': conv_w, 'conv_b': conv_b, 'w_down': w_down,
    }
    lw = _prep_layers(stacked)
    band = _band_bias()
    outs = []
    for x, mem in ((x_prompt, mem_prompt), (x_sample, mem_sample)):
        batch, seq, _ = x.shape
        tabs = _rope_tables(seq)
        xt = x.reshape(batch * seq, D_MODEL)
        for layer in range(DEPTH):
            xt = _trunk_layer(xt, mem, lw, layer, tabs, band, batch, seq)
        outs.append(xt.reshape(batch, seq, D_MODEL))
    return tuple(outs)
```

```python
import functools

import jax
import jax.numpy as jnp
import numpy as np
from jax import lax
from jax.experimental import pallas as pl
from jax.experimental.pallas import tpu as pltpu

D_MODEL = 1024
DEPTH = 4
GRID_W = 64
HEAD_DIM = 64
ROPE_THETA = 10000.0
EPS = 1e-6
MLA_HEADS = 6
MLA_Q_RANK = 256
MLA_KV_RANK = 128
MLA_NOPE = 64
MLA_ROPE = 32
MLA_V = 64
MLA_QK = MLA_NOPE + MLA_ROPE
NA_HEADS = 6
NA_WIN_R = 8
NA_WIN_C = 16
DIL_WINDOWS = (128, 512, 2048)
DIL_RATES = (1, 4, 16)
DIL_GROUPS = 3
DIL_HEADS = 4
DIL_ALL = DIL_GROUPS * DIL_HEADS
X_HEADS = 4
X_HEAD_DIM = D_MODEL // X_HEADS
D_FF = 2816
CONV_W = 3
SPLIT_CQ = MLA_Q_RANK
SPLIT_CKV = SPLIT_CQ + MLA_KV_RANK
SPLIT_KR = SPLIT_CKV + MLA_ROPE
SPLIT_NA = SPLIT_KR + 3 * NA_HEADS * HEAD_DIM
D_IN = SPLIT_NA + 3 * DIL_ALL * HEAD_DIM

LANES = 128
NEG = -1e30
LOG2E = 1.4426950408889634
TM = 512
DIL_TQ = 128
DIL_SB = DIL_TQ * max(DIL_RATES)
DIL_UNROLL = 8
NA_RB = 16
FF_CH = 256
N_FF_CH = D_FF // FF_CH
VMEM_LIMIT = 56 * 1024 * 1024

COL_MLA = 0
COL_NA = 512
COL_DIL = COL_NA + 3 * NA_HEADS * HEAD_DIM
D_IN_P = COL_DIL + 3 * DIL_ALL * HEAD_DIM

_BF = jnp.bfloat16
_F32 = jnp.float32


def _dot(a, b):
    return jnp.dot(a, b, preferred_element_type=_F32)


def _dot_t(a, b):
    return lax.dot_general(a, b, (((1,), (1,)), ((), ())), preferred_element_type=_F32)


def _rms(x, g):
    ms = jnp.mean(x * x, axis=-1, keepdims=True)
    return x * lax.rsqrt(ms + EPS) * g


def _lane(shape):
    return lax.broadcasted_iota(jnp.int32, shape, len(shape) - 1)


def _head_join_matrices():
    i = np.arange(2 * LANES)
    same_group = (i[:, None] // LANES) == (i[None, :] // LANES)
    natural = (i[:, None] // HEAD_DIM) == (i[None, :] // HEAD_DIM)
    half = ((i % HEAD_DIM) < HEAD_DIM // 2)
    rotary = same_group & (half[:, None] == half[None, :])
    return [jnp.asarray(m, _BF) for m in (same_group, natural, rotary)]


def _seg_inv(x2, bd_ref, dim):
    ss = _dot((x2 * x2).astype(_BF), bd_ref[...])
    return lax.rsqrt(ss * (1.0 / dim) + EPS)


def _mix_in_kernel(x_ref, gmix_ref, win_ref, gcq_ref, gckv_ref, wuq_ref, wuk_ref, wuv_ref,
                   gmq_ref, gmk_ref, gkr_ref, gnq_ref, gnk_ref, gdq_ref, gdk_ref,
                   bd128_ref, bd64_ref, bdr_ref,
                   tqc_ref, tqs1_ref, tqs2_ref, tkc_ref, tks_ref, tdc_ref, tds_ref,
                   qm_ref, km_ref, vm_ref, qn_ref, kn_ref, vn_ref, qd_ref, kd_ref, vd_ref,
                   perm_ref):
    tm = x_ref.shape[0]
    x = x_ref[...]
    xg = (x * gmix_ref[...]).astype(_BF)
    inv_x = lax.rsqrt(jnp.mean(x * x, axis=-1, keepdims=True) + EPS)
    lane = _lane((tm, LANES))

    def project(lo, hi):
        return _dot(xg, win_ref[:, lo:hi]) * inv_x

    def latent(z):
        cq = _rms(z[:, 0:MLA_Q_RANK], gcq_ref[...]).astype(_BF)
        ckv = _rms(z[:, MLA_Q_RANK:MLA_Q_RANK + MLA_KV_RANK], gckv_ref[...]).astype(_BF)
        kr = z[:, 384:512]
        kr_own = jnp.where((lane >= MLA_NOPE) & (lane < MLA_QK), kr, 0.0)
        krg = kr * gkr_ref[...]
        kr_rot = krg * tkc_ref[...] + pltpu.roll(krg, 64, 1) * tks_ref[...]
        qpre = _dot(cq, wuq_ref[...])
        kpre = _dot(ckv, wuk_ref[...])
        vpre = _dot(ckv, wuv_ref[...])
        row_t = lax.broadcasted_iota(jnp.int32, (LANES, tm), 0)
        for c in range(MLA_HEADS // 2):
            q2 = qpre[:, 2 * c * LANES:(2 * c + 2) * LANES]
            k2 = kpre[:, 2 * c * LANES:(2 * c + 2) * LANES]
            inv_q = _seg_inv(q2, bd128_ref, MLA_QK)
            inv_k = _seg_inv(k2 + jnp.concatenate([kr_own, kr_own], axis=1), bd128_ref, MLA_QK)
            for hh in range(2):
                hd = 2 * c + hh
                sl = slice(hh * LANES, (hh + 1) * LANES)
                qg = q2[:, sl] * gmq_ref[...]
                qr = (qg * tqc_ref[...] + pltpu.roll(qg, 16, 1) * tqs1_ref[...]
                      + pltpu.roll(qg, 112, 1) * tqs2_ref[...])
                qm_ref[hd] = (qr * inv_q[:, sl]).astype(_BF)
                km_ref[hd] = ((k2[:, sl] * gmk_ref[...] + kr_rot) * inv_k[:, sl]).astype(_BF)
                vt = vpre[:, hd * LANES:(hd + 1) * LANES].T
                vm_ref[hd, 0] = jnp.where(row_t == MLA_V, 1.0, vt).astype(_BF)

    def neighbourhood(z):
        q2 = z[:, 0:256]
        k2 = z[:, 256:512]
        qk2 = z[:, 768:1024]
        qn2 = q2 * _seg_inv(q2, bd64_ref, HEAD_DIM)
        kn2 = k2 * _seg_inv(k2, bd64_ref, HEAD_DIM)
        qkn2 = qk2 * _seg_inv(qk2, bd64_ref, HEAD_DIM)
        for p in range(2):
            sl = slice(p * LANES, (p + 1) * LANES)
            qn_ref[p] = (qn2[:, sl] * gnq_ref[...]).astype(_BF)
            kn_ref[p] = (kn2[:, sl] * gnk_ref[...]).astype(_BF)
            vn_ref[p] = z[:, 512 + p * LANES: 512 + (p + 1) * LANES].astype(_BF)
        qn_ref[2] = (qkn2[:, :LANES] * gnq_ref[...]).astype(_BF)
        kn_ref[2] = (qkn2[:, LANES:] * gnk_ref[...]).astype(_BF)
        vn_ref[2] = z[:, 1024:1152].astype(_BF)

    def put(out_ref, p, val, rate, slot):
        if rate == 1:
            out_ref[p] = val.astype(_BF)
        else:
            rows = tm // rate
            perm_ref[slot] = val
            for rho in range(rate):
                out_ref[p, rho * rows:(rho + 1) * rows, :] = (
                    perm_ref[slot, pl.ds(rho, rows, stride=rate), :].astype(_BF))

    def dilated(g, z):
        q2 = z[:, 0:256]
        k2 = z[:, 256:512]
        qn2 = q2 * _seg_inv(q2, bdr_ref, HEAD_DIM)
        kn2 = k2 * _seg_inv(k2, bdr_ref, HEAD_DIM)
        for pp in range(2):
            p = 2 * g + pp
            sl = slice(pp * LANES, (pp + 1) * LANES)
            qx = qn2[:, sl] * gdq_ref[...]
            qx = qx * tdc_ref[...] + pltpu.roll(qx, 64, 1) * tds_ref[...]
            kx = kn2[:, sl] * gdk_ref[...]
            kx = kx * tdc_ref[...] + pltpu.roll(kx, 64, 1) * tds_ref[...]
            put(qd_ref, p, qx, DIL_RATES[g], 3 * pp)
            put(kd_ref, p, kx, DIL_RATES[g], 3 * pp + 1)
            put(vd_ref, p, z[:, 512 + pp * LANES: 512 + (pp + 1) * LANES], DIL_RATES[g], 3 * pp + 2)

    dil_lo = [COL_DIL + g * 768 for g in range(DIL_GROUPS)]
    stages = [(dil_lo[2], dil_lo[2] + 768, functools.partial(dilated, 2)),
              (dil_lo[1], dil_lo[1] + 768, functools.partial(dilated, 1)),
              (dil_lo[0], dil_lo[0] + 768, functools.partial(dilated, 0)),
              (COL_MLA, COL_NA, latent),
              (COL_NA, COL_DIL, neighbourhood)]
    z_next = project(stages[0][0], stages[0][1])
    for idx, (_, _, finish) in enumerate(stages):
        z = z_next
        if idx + 1 < len(stages):
            z_next = project(stages[idx + 1][0], stages[idx + 1][1])
        finish(z)


def _const_spec(shape):
    nd = len(shape)
    return pl.BlockSpec(shape, lambda *_, _nd=nd: (0,) * _nd)


def _layer_spec(arr, layer):
    nd = arr.ndim
    return pl.BlockSpec((None,) + arr.shape[1:], lambda *_, _l=layer, _nd=nd: (_l,) + (0,) * (_nd - 1))


def _mix_in(x, lw, layer, tabs, seq):
    t = x.shape[0]
    nblk_seq = seq // TM
    tab_spec = pl.BlockSpec((TM, LANES), lambda i: (i % nblk_seq, 0))
    weights = [lw[n] for n in ('g_mix', 'w_in', 'g_cq', 'g_ckv', 'w_uq', 'w_uk', 'w_uv',
                               'g_mq', 'g_mk', 'g_kr', 'g_nq', 'g_nk', 'g_dq', 'g_dk')]
    joins = _head_join_matrices()
    tables = [tabs['q_c'], tabs['q_s1'], tabs['q_s2'], tabs['k_c'], tabs['k_s'], tabs['d_c'], tabs['d_s']]

    def out(n):
        return (jax.ShapeDtypeStruct((n, t, LANES), _BF), pl.BlockSpec((n, TM, LANES), lambda i: (0, i, 0)))

    vt_out = (jax.ShapeDtypeStruct((MLA_HEADS, t // TM, LANES, TM), _BF),
              pl.BlockSpec((MLA_HEADS, 1, LANES, TM), lambda i: (0, i, 0, 0)))
    outs = [out(6), out(6), vt_out, out(3), out(3), out(3), out(6), out(6), out(6)]
    return pl.pallas_call(
        _mix_in_kernel,
        grid=(t // TM,),
        in_specs=[pl.BlockSpec((TM, D_MODEL), lambda i: (i, 0))]
        + [_layer_spec(w, layer) for w in weights] + [_const_spec(m.shape) for m in joins]
        + [tab_spec] * len(tables),
        out_specs=[o[1] for o in outs],
        out_shape=[o[0] for o in outs],
        scratch_shapes=[pltpu.VMEM((6, TM, LANES), _F32)],
        compiler_params=pltpu.CompilerParams(dimension_semantics=("parallel",), vmem_limit_bytes=VMEM_LIMIT),
        name="mix_in",
    )(x, *weights, *joins, *tables)


def _mla_kernel(q_ref, k_ref, vt_ref, o_ref, sa_sc, sb_sc, m_sc, acc_sc, *, tk):
    tq = q_ref.shape[1]
    nk = k_ref.shape[1] // tk
    sub = tk // TM
    m_sc[...] = jnp.full(m_sc.shape, NEG, _F32)
    acc_sc[...] = jnp.zeros(acc_sc.shape, _F32)

    def scores(j, dst):
        rows = pl.ds(pl.multiple_of(j * tk, tk), tk)
        for hh in range(2):
            dst[hh] = _dot_t(k_ref[hh, rows, :], q_ref[hh])

    def consume(j, src):
        for hh in range(2):
            s = src[hh]
            m_prev = m_sc[hh]
            m_new = jnp.maximum(m_prev, jnp.max(s, axis=0, keepdims=True))
            p = jnp.exp2(s - m_new).astype(_BF)
            vt = jnp.concatenate([vt_ref[hh, j * sub + c] for c in range(sub)], axis=1)
            acc_sc[hh] = jnp.exp2(m_prev - m_new) * acc_sc[hh] + _dot(vt, p)
            m_sc[hh] = m_new

    def pair(i, carry):
        j = 2 * i
        scores(j + 1, sb_sc)
        consume(j, sa_sc)
        scores(j + 2, sa_sc)
        consume(j + 1, sb_sc)
        return carry

    scores(0, sa_sc)
    lax.fori_loop(0, nk // 2 - 1, pair, 0)
    scores(nk - 1, sb_sc)
    consume(nk - 2, sa_sc)
    consume(nk - 1, sb_sc)

    lane = _lane((tq, LANES))
    outs = []
    for hh in range(2):
        acc = acc_sc[hh]
        outs.append((acc / acc[MLA_V:MLA_V + 1, :]).T)
    o_ref[0] = jnp.where(lane < MLA_V, outs[0], pltpu.roll(outs[1], MLA_V, 1)).astype(o_ref.dtype)


def _mla(q, k, vt, batch, seq, tq=1024, tk=512):
    t = q.shape[1]
    nq = seq // tq
    nblk = seq // TM
    assert seq % (2 * tk) == 0 and tk % TM == 0
    return pl.pallas_call(
        functools.partial(_mla_kernel, tk=tk),
        grid=(batch, MLA_HEADS // 2, nq),
        in_specs=[pl.BlockSpec((2, tq, LANES), lambda b, p, i: (p, b * nq + i, 0)),
                  pl.BlockSpec((2, seq, LANES), lambda b, p, i: (p, b, 0)),
                  pl.BlockSpec((2, nblk, LANES, TM), lambda b, p, i: (p, b, 0, 0))],
        out_specs=pl.BlockSpec((1, tq, LANES), lambda b, p, i: (p, b * nq + i, 0)),
        out_shape=jax.ShapeDtypeStruct((MLA_HEADS // 2, t, LANES), _BF),
        scratch_shapes=[pltpu.VMEM((2, tk, tq), _F32), pltpu.VMEM((2, tk, tq), _F32),
                        pltpu.VMEM((2, 1, tq), _F32), pltpu.VMEM((2, LANES, tq), _F32)],
        compiler_params=pltpu.CompilerParams(
            dimension_semantics=("parallel", "parallel", "arbitrary"), vmem_limit_bytes=VMEM_LIMIT),
        name="mla_attention",
    )(q, k, vt)


def _na_kernel(q_ref, k_ref, v_ref, bias_ref, o_ref, *, rows):
    i = pl.program_id(2)
    lane = _lane((GRID_W, LANES))
    first = lane < HEAD_DIM
    nkeys = NA_WIN_R * GRID_W

    scores, windows = [], []
    for rr in range(NA_RB):
        r = i * NA_RB + rr
        rs = jnp.clip(r - NA_WIN_R // 2, 0, rows - NA_WIN_R)
        start = pl.multiple_of(rs * GRID_W, GRID_W)
        q = q_ref[0, rr * GRID_W:(rr + 1) * GRID_W, :]
        zero = jnp.zeros_like(q)
        q2 = jnp.concatenate([jnp.where(first, q, zero), jnp.where(first, zero, q)], axis=0)
        scores.append(_dot_t(q2, k_ref[0, pl.ds(start, nkeys), :]) + bias_ref[0, r - rs])
        windows.append(start)
    probs = []
    for s in scores:
        p = jnp.exp(s - jnp.max(s, axis=1, keepdims=True))
        probs.append((p.astype(_BF), jnp.sum(p, axis=1, keepdims=True)))
    for rr, ((p, l), start) in enumerate(zip(probs, windows)):
        o2 = _dot(p, v_ref[0, pl.ds(start, nkeys), :]) / l
        o = jnp.where(first, o2[:GRID_W], o2[GRID_W:])
        o_ref[0, rr * GRID_W:(rr + 1) * GRID_W, :] = o.astype(o_ref.dtype)


def _na(q, k, v, bias, layer, batch, seq):
    t = q.shape[1]
    rows = seq // GRID_W
    assert rows >= NA_WIN_R and rows % NA_RB == 0
    nrb = rows // NA_RB
    qrows = NA_RB * GRID_W
    return pl.pallas_call(
        functools.partial(_na_kernel, rows=rows),
        grid=(batch, NA_HEADS // 2, nrb),
        in_specs=[pl.BlockSpec((1, qrows, LANES), lambda b, p, i: (p, b * nrb + i, 0)),
                  pl.BlockSpec((1, seq, LANES), lambda b, p, i: (p, b, 0)),
                  pl.BlockSpec((1, seq, LANES), lambda b, p, i: (p, b, 0)),
                  pl.BlockSpec((None, 1, NA_WIN_R, 2 * GRID_W, NA_WIN_R * GRID_W),
                               lambda b, p, i: (layer, p, 0, 0, 0))],
        out_specs=pl.BlockSpec((1, qrows, LANES), lambda b, p, i: (p, b * nrb + i, 0)),
        out_shape=jax.ShapeDtypeStruct((NA_HEADS // 2, t, LANES), _BF),
        compiler_params=pltpu.CompilerParams(
            dimension_semantics=("parallel", "parallel", "arbitrary"), vmem_limit_bytes=VMEM_LIMIT),
        name="na_attention",
    )(q, k, v, bias)


def _dil_kernel(q0_ref, q1_ref, q2_ref, k0_ref, k1_ref, k2_ref, v0_ref, v1_ref, v2_ref, band_ref,
                o_ref, og_sc, lse_sc, *, seq):
    sb = pl.program_id(2)
    q_refs = (q0_ref, q1_ref, q2_ref)
    k_refs = (k0_ref, k1_ref, k2_ref)
    v_refs = (v0_ref, v1_ref, v2_ref)
    lane = _lane((DIL_TQ, LANES))
    first_qk = (lane % HEAD_DIM) < (HEAD_DIM // 2)
    first_v = lane < HEAD_DIM
    n_tiles = DIL_SB // DIL_TQ
    half = DIL_TQ // 2

    for g in range(DIL_GROUPS):
        rate = DIL_RATES[g]
        length = seq // rate
        per_blk = TM // rate
        chunk = min(half, per_blk)
        n_chunks = (2 * DIL_TQ) // chunk
        tiles_per_res = n_tiles // rate
        q_ref, k_ref, v_ref = q_refs[g], k_refs[g], v_refs[g]

        def row_of(m, rho, _per_blk=per_blk):
            return (m // _per_blk) * TM + rho * _per_blk + (m % _per_blk)

        def body(tg, carry, _g=g, _rate=rate, _length=length, _per_blk=per_blk, _chunk=chunk,
                 _n_chunks=n_chunks, _tpr=tiles_per_res, _q=q_ref, _k=k_ref, _v=v_ref):
            scores, metas = [], []
            for u in range(DIL_UNROLL):
                tt = tg * DIL_UNROLL + u
                rho = tt // _tpr
                mt = tt % _tpr
                m0 = sb * (DIL_SB // _rate) + mt * DIL_TQ
                q = jnp.concatenate(
                    [_q[0, pl.ds(pl.multiple_of(row_of(m0 + c * _chunk, rho), _chunk), _chunk), :]
                     for c in range(DIL_TQ // _chunk)], axis=0)
                zero = jnp.zeros_like(q)
                q2 = jnp.concatenate([jnp.where(first_qk, q, zero), jnp.where(first_qk, zero, q)], axis=0)
                win = []
                for c in range(_n_chunks):
                    ms = jnp.clip(m0 - half + c * _chunk, 0, _length - _chunk)
                    win.append(pl.multiple_of(row_of(ms, rho), _chunk))
                kw = jnp.concatenate([_k[0, pl.ds(row, _chunk), :] for row in win], axis=0)
                edge = (m0 == 0).astype(jnp.int32) + 2 * (m0 == _length - DIL_TQ).astype(jnp.int32)
                scores.append(_dot_t(q2, kw) + band_ref[edge])
                metas.append((rho + _rate * mt * DIL_TQ, win))
            probs = []
            for s in scores:
                m = jnp.max(s, axis=1, keepdims=True)
                p = jnp.exp(s - m)
                probs.append((p.astype(_BF), m, jnp.sum(p, axis=1, keepdims=True)))
            for (p, m, l), (base, win) in zip(probs, metas):
                vw = jnp.concatenate([_v[0, pl.ds(row, _chunk), :] for row in win], axis=0)
                o2 = _dot(p, vw) / l
                lse2 = jnp.broadcast_to(m + jnp.log(l), (2 * DIL_TQ, LANES))
                if _rate == 1:
                    dst = pl.ds(pl.multiple_of(base, DIL_TQ), DIL_TQ)
                else:
                    dst = pl.ds(base, DIL_TQ, stride=_rate)
                og_sc[_g, dst, :] = jnp.where(first_v, o2[:DIL_TQ], o2[DIL_TQ:])
                lse_sc[_g, dst, :] = jnp.where(first_v, lse2[:DIL_TQ], lse2[DIL_TQ:])
            return carry

        lax.fori_loop(0, n_tiles // DIL_UNROLL, body, 0)

    m = jnp.maximum(jnp.maximum(lse_sc[0], lse_sc[1]), lse_sc[2])
    w0 = jnp.exp(lse_sc[0] - m)
    w1 = jnp.exp(lse_sc[1] - m)
    w2 = jnp.exp(lse_sc[2] - m)
    o = (w0 * og_sc[0] + w1 * og_sc[1] + w2 * og_sc[2]) / (w0 + w1 + w2)
    o_ref[0] = o.astype(o_ref.dtype)


def _dil(q, k, v, band, batch, seq):
    t = q.shape[1]
    assert seq % DIL_SB == 0
    nsb = seq // DIL_SB

    def q_spec(g):
        return pl.BlockSpec((1, DIL_SB, LANES), lambda b, sp, s, _g=g: (2 * _g + sp, b * nsb + s, 0))

    def kv_spec(g):
        return pl.BlockSpec((1, seq, LANES), lambda b, sp, s, _g=g: (2 * _g + sp, b, 0))

    return pl.pallas_call(
        functools.partial(_dil_kernel, seq=seq),
        grid=(batch, 2, nsb),
        in_specs=[kv_spec(0), kv_spec(1), kv_spec(2), kv_spec(0), kv_spec(1), kv_spec(2),
                  kv_spec(0), kv_spec(1), kv_spec(2),
                  pl.BlockSpec((4, 2 * DIL_TQ, 2 * DIL_TQ), lambda b, sp, s: (0, 0, 0))],
        out_specs=pl.BlockSpec((1, DIL_SB, LANES), lambda b, sp, s: (sp, b * nsb + s, 0)),
        out_shape=jax.ShapeDtypeStruct((2, t, LANES), _BF),
        scratch_shapes=[pltpu.VMEM((DIL_GROUPS, DIL_SB, LANES), _F32),
                        pltpu.VMEM((DIL_GROUPS, DIL_SB, LANES), _F32)],
        compiler_params=pltpu.CompilerParams(
            dimension_semantics=("parallel", "parallel", "arbitrary"), vmem_limit_bytes=VMEM_LIMIT),
        name="dil_attention",
    )(q, q, q, k, k, k, v, v, v, band)


def _mem_kv_kernel(mem_ref, gmem_ref, wckv_ref, gxk_ref, k_ref, v_ref):
    hm = _rms(mem_ref[0], gmem_ref[...]).astype(_BF)
    kv = _dot(hm, wckv_ref[...])
    for hd in range(X_HEADS):
        kh = kv[:, hd * X_HEAD_DIM:(hd + 1) * X_HEAD_DIM]
        k_ref[0, :, hd * X_HEAD_DIM:(hd + 1) * X_HEAD_DIM] = _rms(kh, gxk_ref[...]).astype(_BF)
    v_ref[0] = kv[:, D_MODEL:].astype(_BF)


def _mem_kv(mem, lw, layer):
    b, m, _ = mem.shape
    return pl.pallas_call(
        _mem_kv_kernel,
        grid=(b,),
        in_specs=[pl.BlockSpec((1, m, D_MODEL), lambda i: (i, 0, 0))]
        + [_layer_spec(lw[n], layer) for n in ('g_mem', 'w_ckv', 'g_xk')],
        out_specs=[pl.BlockSpec((1, m, D_MODEL), lambda i: (i, 0, 0))] * 2,
        out_shape=[jax.ShapeDtypeStruct((b, m, D_MODEL), _BF)] * 2,
        compiler_params=pltpu.CompilerParams(dimension_semantics=("parallel",), vmem_limit_bytes=VMEM_LIMIT),
        name="mem_kv",
    )(mem, lw['g_mem'], lw['w_ckv'], lw['g_xk'])


def _mix_out_kernel(x_ref, oa_ref, ob_ref, oc_ref, wo_ref, gcross_ref, wcq_ref, gxq_ref,
                    mk_ref, mv_ref, wco_ref, y_ref):
    mix = jnp.concatenate([oa_ref[p] for p in range(3)] + [ob_ref[p] for p in range(3)]
                          + [oc_ref[p] for p in range(2)], axis=1)
    x1 = x_ref[...] + _dot(mix, wo_ref[...])
    hc = _rms(x1, gcross_ref[...]).astype(_BF)
    q = _dot(hc, wcq_ref[...])
    outs = []
    for hd in range(X_HEADS):
        sl = slice(hd * X_HEAD_DIM, (hd + 1) * X_HEAD_DIM)
        qh = (_rms(q[:, sl], gxq_ref[...]) * X_HEAD_DIM ** -0.5).astype(_BF)
        s = _dot_t(qh, mk_ref[0, :, sl])
        m = jnp.max(s, axis=1, keepdims=True)
        p = jnp.exp(s - m)
        l = jnp.sum(p, axis=1, keepdims=True)
        outs.append((_dot(p.astype(_BF), mv_ref[0, :, sl]) / l).astype(_BF))
    o = jnp.concatenate(outs, axis=1)
    y_ref[...] = x1 + _dot(o, wco_ref[...])


def _mix_out(x, oa, ob, oc, mk, mv, lw, layer, seq):
    t = x.shape[0]
    nblk_seq = seq // TM
    mem_len = mk.shape[1]

    def grp(n):
        return pl.BlockSpec((n, TM, LANES), lambda i: (0, i, 0))

    mem_spec = pl.BlockSpec((1, mem_len, D_MODEL), lambda i: (i // nblk_seq, 0, 0))
    return pl.pallas_call(
        _mix_out_kernel,
        grid=(t // TM,),
        in_specs=[pl.BlockSpec((TM, D_MODEL), lambda i: (i, 0)), grp(3), grp(3), grp(2),
                  _layer_spec(lw['w_o'], layer), _layer_spec(lw['g_cross'], layer),
                  _layer_spec(lw['w_cq'], layer), _layer_spec(lw['g_xq'], layer),
                  mem_spec, mem_spec, _layer_spec(lw['w_co'], layer)],
        out_specs=pl.BlockSpec((TM, D_MODEL), lambda i: (i, 0)),
        out_shape=jax.ShapeDtypeStruct((t, D_MODEL), _F32),
        compiler_params=pltpu.CompilerParams(dimension_semantics=("parallel",), vmem_limit_bytes=VMEM_LIMIT),
        name="mix_out",
    )(x, oa, ob, oc, lw['w_o'], lw['g_cross'], lw['w_cq'], lw['g_xq'], mk, mv, lw['w_co'])


def _ffn_kernel(x_ref, xp_ref, xn_ref, gffn_ref, wup_ref, cw_ref, cb_ref, wdn_ref, y_ref, *, nblk_seq):
    i = pl.program_id(0)
    tm = x_ref.shape[0]
    x = x_ref[...]
    g = gffn_ref[...]
    is_first = (i % nblk_seq) == 0
    is_last = (i % nblk_seq) == nblk_seq - 1
    hp = jnp.where(is_first, 0.0, _rms(xp_ref[...], g))
    hn = jnp.where(is_last, 0.0, _rms(xn_ref[...], g))
    h = jnp.concatenate([hp, _rms(x, g), hn], axis=0).astype(_BF)
    acc = x

    def up_proj(c):
        return [_dot(h, wup_ref[:, part * D_FF + c * FF_CH: part * D_FF + (c + 1) * FF_CH]) for part in range(2)]

    def conv(u, lo):
        cw = cw_ref[:, lo:lo + FF_CH]
        up = pltpu.roll(u, 1, 0)[8:8 + tm]
        un = pltpu.roll(u, tm + 15, 0)[8:8 + tm]
        return up * cw[0:1] + u[8:8 + tm] * cw[1:2] + un * cw[2:3] + cb_ref[:, lo:lo + FF_CH]

    pending = []
    u_next = up_proj(0)
    for c in range(N_FF_CH):
        u_val, u_gate = u_next
        if c + 1 < N_FF_CH:
            u_next = up_proj(c + 1)
        a = conv(u_val, c * FF_CH)
        half_gate = 0.5 * conv(u_gate, D_FF + c * FF_CH)
        pending.append((half_gate * (1.0 + jnp.tanh(half_gate)) * a).astype(_BF))
        rest = N_FF_CH - 1 - c
        if len(pending) == 2 and rest != 1 or rest == 0:
            lo = (c + 1 - len(pending)) * FF_CH
            act = pending[0] if len(pending) == 1 else jnp.concatenate(pending, axis=1)
            acc = acc + _dot(act, wdn_ref[lo:(c + 1) * FF_CH, :])
            pending = []
    y_ref[...] = acc


def _ffn(x, lw, layer, seq):
    t = x.shape[0]
    nblk_seq = seq // TM
    nhalo = t // 8
    per = TM // 8
    return pl.pallas_call(
        functools.partial(_ffn_kernel, nblk_seq=nblk_seq),
        grid=(t // TM,),
        in_specs=[pl.BlockSpec((TM, D_MODEL), lambda i: (i, 0)),
                  pl.BlockSpec((8, D_MODEL), lambda i: (jnp.maximum(i * per - 1, 0), 0)),
                  pl.BlockSpec((8, D_MODEL), lambda i: (jnp.minimum((i + 1) * per, nhalo - 1), 0))]
        + [_layer_spec(lw[n], layer) for n in ('g_ffn', 'w_up', 'conv_w', 'conv_b', 'w_down')],
        out_specs=pl.BlockSpec((TM, D_MODEL), lambda i: (i, 0)),
        out_shape=jax.ShapeDtypeStruct((t, D_MODEL), _F32),
        compiler_params=pltpu.CompilerParams(dimension_semantics=("parallel",), vmem_limit_bytes=VMEM_LIMIT),
        name="conv_ffn",
    )(x, x, x, lw['g_ffn'], lw['w_up'], lw['conv_w'], lw['conv_b'], lw['w_down'])


def _relayout_w_in(w):
    nl = w.shape[0]
    lead = w[..., :SPLIT_CKV]
    a = w[..., SPLIT_CKV:SPLIT_CKV + 16]
    b = w[..., SPLIT_CKV + 16:SPLIT_KR]
    z32 = jnp.zeros(a.shape[:-1] + (32,), w.dtype)
    kr = jnp.concatenate([b, a, z32, a, b, z32], axis=-1)
    na = w[..., SPLIT_KR:SPLIT_NA].reshape(nl, D_MODEL, 3, NA_HEADS // 2, LANES)
    na_ab = na[:, :, :, 0:2].reshape(nl, D_MODEL, 3 * 2 * LANES)
    na_c = na[:, :, :, 2].reshape(nl, D_MODEL, 3 * LANES)
    dl = w[..., SPLIT_NA:].reshape(nl, D_MODEL, 3, DIL_GROUPS, 2, 2, 2, HEAD_DIM // 2)
    qk = dl[:, :, 0:2].transpose(0, 1, 3, 2, 4, 6, 5, 7).reshape(nl, D_MODEL, DIL_GROUPS, 4 * LANES)
    v = dl[:, :, 2].reshape(nl, D_MODEL, DIL_GROUPS, 2 * LANES)
    dil = jnp.concatenate([qk, v], axis=-1).reshape(nl, D_MODEL, DIL_GROUPS * 6 * LANES)
    out = jnp.concatenate([lead, kr, na_ab, na_c, dil], axis=-1)
    assert out.shape[-1] == D_IN_P
    return out


def _pad_heads(w, real, n_heads):
    nl, k, _ = w.shape
    w = w.reshape(nl, k, n_heads, real)
    return jnp.pad(w, ((0, 0), (0, 0), (0, 0), (0, LANES - real))).reshape(nl, k, n_heads * LANES)


def _rows(v):
    return v[:, None, :].astype(_F32)


def _prep_layers(p):
    nl = p['w_in'].shape[0]
    kn, qn = p['mla_kn'], p['mla_qn']
    z32 = jnp.zeros((nl, 32), _F32)
    z64 = jnp.zeros((nl, 64), _F32)
    dq, dk = p['dil_qn'], p['dil_kn']

    def rot(gain):
        return jnp.concatenate([gain[:, :32], gain[:, :32], gain[:, 32:], gain[:, 32:]], axis=-1)

    return {
        'g_mix': _rows(p['norm_mix']), 'w_in': _relayout_w_in(p['w_in'].astype(_BF)),
        'g_cq': _rows(p['mla_q_norm']), 'g_ckv': _rows(p['mla_kv_norm']),
        'w_uq': _pad_heads(p['w_uq'].astype(_BF), MLA_QK, MLA_HEADS),
        'w_uk': _pad_heads(p['w_uk'].astype(_BF), MLA_NOPE, MLA_HEADS),
        'w_uv': _pad_heads(p['w_uv'].astype(_BF), MLA_V, MLA_HEADS),
        'g_mq': _rows(jnp.concatenate([qn, z32], axis=-1) * (MLA_QK ** -0.5 * LOG2E)),
        'g_mk': _rows(jnp.concatenate([kn[:, :MLA_NOPE], z64], axis=-1)),
        'g_kr': _rows(jnp.concatenate([kn[:, 80:96], kn[:, 64:80], z32, kn[:, 64:80], kn[:, 80:96], z32], axis=-1)),
        'g_nq': _rows(jnp.tile(p['na_qn'], (1, 2)) * HEAD_DIM ** -0.5), 'g_nk': _rows(jnp.tile(p['na_kn'], (1, 2))),
        'g_dq': _rows(rot(dq) * HEAD_DIM ** -0.5), 'g_dk': _rows(rot(dk)),
        'w_o': p['w_o'].astype(_BF),
        'g_cross': _rows(p['norm_cross']), 'w_cq': p['w_cq'].astype(_BF),
        'g_xq': _rows(p['x_qn']), 'g_xk': _rows(p['x_kn']), 'g_mem': _rows(p['norm_mem']),
        'w_ckv': p['w_ckv'].astype(_BF), 'w_co': p['w_co'].astype(_BF),
        'g_ffn': _rows(p['norm_ffn']), 'w_up': p['w_up'].astype(_BF),
        'conv_w': p['conv_w'].astype(_F32), 'conv_b': _rows(p['conv_b']),
        'w_down': p['w_down'].astype(_BF),
        'na_bias': _na_bias(p['na_rpb']),
    }


def _na_bias(rpb):
    nl = rpb.shape[0]
    dl = np.arange(NA_WIN_R)
    j = np.arange(NA_WIN_R)
    r_off = j[None, :] - dl[:, None] + (NA_WIN_R - 1)
    qc = np.arange(GRID_W)
    kc = np.arange(GRID_W)
    c_start = np.clip(qc - NA_WIN_C // 2, 0, GRID_W - NA_WIN_C)
    valid = (kc[None, :] >= c_start[:, None]) & (kc[None, :] < c_start[:, None] + NA_WIN_C)
    c_off = kc[None, :] - qc[:, None] + (NA_WIN_C - 1)
    pick = (np.arange(2 * NA_WIN_C - 1)[:, None, None] == c_off[None]) & valid[None]
    rows = rpb[:, :, r_off].reshape(nl, NA_HEADS // 2, 2, NA_WIN_R, NA_WIN_R, 2 * NA_WIN_C - 1)
    b = jnp.einsum('lpedjc,cqk->lpdeqjk', rows, jnp.asarray(pick, _F32), precision=lax.Precision.HIGHEST)
    b = b + jnp.asarray(np.where(valid, 0.0, NEG), _F32)[:, None, :]
    return b.reshape(nl, NA_HEADS // 2, NA_WIN_R, 2 * GRID_W, NA_WIN_R * GRID_W)


def _band_bias():
    i = np.arange(DIL_TQ)[:, None]
    c = np.arange(2 * DIL_TQ)[None, :]
    band = (c >= i) & (c <= i + DIL_TQ)
    out = []
    for edge in range(4):
        ok = band
        if edge & 1:
            ok = ok & (c >= DIL_TQ // 2)
        if edge & 2:
            ok = ok & (c < 2 * DIL_TQ - DIL_TQ // 2)
        out.append(np.tile(np.where(ok, 0.0, NEG), (2, 1)))
    return jnp.asarray(np.stack(out), _F32)


def _rope_tables(seq):
    pos = jnp.arange(seq, dtype=_F32)[:, None]

    def cs(half):
        inv = ROPE_THETA ** (-jnp.arange(half, dtype=_F32) / half)
        ang = pos * inv[None, :]
        return jnp.cos(ang), jnp.sin(ang)

    c16, s16 = cs(MLA_ROPE // 2)
    c32, s32 = cs(HEAD_DIM // 2)
    one = jnp.ones((seq, 64), _F32)
    z16 = jnp.zeros((seq, 16), _F32)
    z32 = jnp.zeros((seq, 32), _F32)
    z64 = jnp.zeros((seq, 64), _F32)
    return {
        'q_c': jnp.concatenate([one, c16, c16, z32], axis=1),
        'q_s1': jnp.concatenate([z64, z16, s16, z32], axis=1),
        'q_s2': jnp.concatenate([z64, -s16, z16, z32], axis=1),
        'k_c': jnp.concatenate([z64, c16, c16, z32], axis=1),
        'k_s': jnp.concatenate([z64, -s16, s16, z32], axis=1),
        'd_c': jnp.concatenate([c32, c32, c32, c32], axis=1),
        'd_s': jnp.concatenate([-s32, -s32, s32, s32], axis=1),
    }


def _trunk_layer(x, mem, lw, layer, tabs, band, batch, seq):
    qm, km, vm, qn, kn, vn, qd, kd, vd = _mix_in(x, lw, layer, tabs, seq)
    oa = _mla(qm, km, vm, batch, seq)
    ob = _na(qn, kn, vn, lw['na_bias'], layer, batch, seq)
    oc = _dil(qd, kd, vd, band, batch, seq)
    mk, mv = _mem_kv(mem, lw, layer)
    x = _mix_out(x, oa, ob, oc, mk, mv, lw, layer, seq)
    return _ffn(x, lw, layer, seq)


def kernel(x_prompt, x_sample, mem_prompt, mem_sample, norm_mix, w_in, mla_q_norm, mla_kv_norm, w_uq, w_uk, w_uv, mla_qn, mla_kn, na_qn, na_kn, na_rpb, dil_qn, dil_kn, w_o, norm_cross, norm_mem, w_cq, w_ckv, x_qn, x_kn, w_co, norm_ffn, w_up, conv_w, conv_b, w_down):
    stacked = {
        'norm_mix': norm_mix, 'w_in': w_in, 'mla_q_norm': mla_q_norm, 'mla_kv_norm': mla_kv_norm,
        'w_uq': w_uq, 'w_uk': w_uk, 'w_uv': w_uv, 'mla_qn': mla_qn, 'mla_kn': mla_kn,
        'na_qn': na_qn, 'na_kn': na_kn, 'na_rpb': na_rpb, 'dil_qn': dil_qn, 'dil_kn': dil_kn,
        'w_o': w_o, 'norm_cross': norm_cross, 'norm_mem': norm_mem, 'w_cq': w_cq, 'w_ckv': w_ckv,
        'x_qn': x_qn, 'x_kn': x_kn, 'w_co': w_co, 'norm_ffn': norm_ffn, 'w_up': w_up,
        'conv_w': conv_w, 'conv_b': conv_b, 'w_down': w_down,
    }
    lw = _prep_layers(stacked)
    band = _band_bias()
    outs = []
    for x, mem in ((x_prompt, mem_prompt), (x_sample, mem_sample)):
        batch, seq, _ = x.shape
        tabs = _rope_tables(seq)
        xt = x.reshape(batch * seq, D_MODEL)
        for layer in range(DEPTH):
            xt = _trunk_layer(xt, mem, lw, layer, tabs, band, batch, seq)
        outs.append(xt.reshape(batch, seq, D_MODEL))
    return tuple(outs)
```

```python
import functools

import jax
import jax.numpy as jnp
import numpy as np
from jax import lax
from jax.experimental import pallas as pl
from jax.experimental.pallas import tpu as pltpu

D_MODEL = 1024
DEPTH = 4
GRID_W = 64
HEAD_DIM = 64
ROPE_THETA = 10000.0
EPS = 1e-6
MLA_HEADS = 6
MLA_Q_RANK = 256
MLA_KV_RANK = 128
MLA_NOPE = 64
MLA_ROPE = 32
MLA_V = 64
MLA_QK = MLA_NOPE + MLA_ROPE
NA_HEADS = 6
NA_WIN_R = 8
NA_WIN_C = 16
DIL_WINDOWS = (128, 512, 2048)
DIL_RATES = (1, 4, 16)
DIL_GROUPS = 3
DIL_HEADS = 4
DIL_ALL = DIL_GROUPS * DIL_HEADS
X_HEADS = 4
X_HEAD_DIM = D_MODEL // X_HEADS
D_FF = 2816
CONV_W = 3
SPLIT_CQ = MLA_Q_RANK
SPLIT_CKV = SPLIT_CQ + MLA_KV_RANK
SPLIT_KR = SPLIT_CKV + MLA_ROPE
SPLIT_NA = SPLIT_KR + 3 * NA_HEADS * HEAD_DIM
D_IN = SPLIT_NA + 3 * DIL_ALL * HEAD_DIM

LANES = 128
NEG = -1e30
LOG2E = 1.4426950408889634
TM = 512
DIL_TQ = 128
DIL_SB = DIL_TQ * max(DIL_RATES)
DIL_UNROLL = 8
NA_RB = 16
FF_CH = 256
N_FF_CH = D_FF // FF_CH
VMEM_LIMIT = 56 * 1024 * 1024

COL_MLA = 0
COL_NA = 512
COL_DIL = COL_NA + 3 * NA_HEADS * HEAD_DIM
D_IN_P = COL_DIL + 3 * DIL_ALL * HEAD_DIM

_BF = jnp.bfloat16
_F32 = jnp.float32


def _dot(a, b):
    return jnp.dot(a, b, preferred_element_type=_F32)


def _dot_t(a, b):
    return lax.dot_general(a, b, (((1,), (1,)), ((), ())), preferred_element_type=_F32)


def _rms(x, g):
    ms = jnp.mean(x * x, axis=-1, keepdims=True)
    return x * lax.rsqrt(ms + EPS) * g


def _lane(shape):
    return lax.broadcasted_iota(jnp.int32, shape, len(shape) - 1)


def _head_join_matrices():
    i = np.arange(2 * LANES)
    same_group = (i[:, None] // LANES) == (i[None, :] // LANES)
    natural = (i[:, None] // HEAD_DIM) == (i[None, :] // HEAD_DIM)
    half = ((i % HEAD_DIM) < HEAD_DIM // 2)
    rotary = same_group & (half[:, None] == half[None, :])
    return [jnp.asarray(m, _BF) for m in (same_group, natural, rotary)]


def _seg_inv(x2, bd_ref, dim):
    ss = _dot((x2 * x2).astype(_BF), bd_ref[...])
    return lax.rsqrt(ss * (1.0 / dim) + EPS)


def _mix_in_kernel(x_ref, gmix_ref, win_ref, gcq_ref, gckv_ref, wuq_ref, wuk_ref, wuv_ref,
                   gmq_ref, gmk_ref, gkr_ref, gnq_ref, gnk_ref, gdq_ref, gdk_ref,
                   bd128_ref, bd64_ref, bdr_ref,
                   tqc_ref, tqs1_ref, tqs2_ref, tkc_ref, tks_ref, tdc_ref, tds_ref,
                   qm_ref, km_ref, vm_ref, qn_ref, kn_ref, vn_ref, qd_ref, kd_ref, vd_ref,
                   perm_ref):
    tm = x_ref.shape[0]
    x = x_ref[...]
    xg = (x * gmix_ref[...]).astype(_BF)
    inv_x = lax.rsqrt(jnp.mean(x * x, axis=-1, keepdims=True) + EPS)
    lane = _lane((tm, LANES))

    def project(lo, hi):
        return _dot(xg, win_ref[:, lo:hi]) * inv_x

    def latent(z):
        cq = _rms(z[:, 0:MLA_Q_RANK], gcq_ref[...]).astype(_BF)
        ckv = _rms(z[:, MLA_Q_RANK:MLA_Q_RANK + MLA_KV_RANK], gckv_ref[...]).astype(_BF)
        kr = z[:, 384:512]
        kr_own = jnp.where((lane >= MLA_NOPE) & (lane < MLA_QK), kr, 0.0)
        krg = kr * gkr_ref[...]
        kr_rot = krg * tkc_ref[...] + pltpu.roll(krg, 64, 1) * tks_ref[...]
        qpre = _dot(cq, wuq_ref[...])
        kpre = _dot(ckv, wuk_ref[...])
        vpre = _dot(ckv, wuv_ref[...])
        row_t = lax.broadcasted_iota(jnp.int32, (LANES, tm), 0)
        for c in range(MLA_HEADS // 2):
            q2 = qpre[:, 2 * c * LANES:(2 * c + 2) * LANES]
            k2 = kpre[:, 2 * c * LANES:(2 * c + 2) * LANES]
            inv_q = _seg_inv(q2, bd128_ref, MLA_QK)
            inv_k = _seg_inv(k2 + jnp.concatenate([kr_own, kr_own], axis=1), bd128_ref, MLA_QK)
            for hh in range(2):
                hd = 2 * c + hh
                sl = slice(hh * LANES, (hh + 1) * LANES)
                qg = q2[:, sl] * gmq_ref[...]
                qr = (qg * tqc_ref[...] + pltpu.roll(qg, 16, 1) * tqs1_ref[...]
                      + pltpu.roll(qg, 112, 1) * tqs2_ref[...])
                qm_ref[hd] = (qr * inv_q[:, sl]).astype(_BF)
                km_ref[hd] = ((k2[:, sl] * gmk_ref[...] + kr_rot) * inv_k[:, sl]).astype(_BF)
                vt = vpre[:, hd * LANES:(hd + 1) * LANES].T
                vm_ref[hd, 0] = jnp.where(row_t == MLA_V, 1.0, vt).astype(_BF)

    def neighbourhood(z):
        q2 = z[:, 0:256]
        k2 = z[:, 256:512]
        qk2 = z[:, 768:1024]
        qn2 = q2 * _seg_inv(q2, bd64_ref, HEAD_DIM)
        kn2 = k2 * _seg_inv(k2, bd64_ref, HEAD_DIM)
        qkn2 = qk2 * _seg_inv(qk2, bd64_ref, HEAD_DIM)
        for p in range(2):
            sl = slice(p * LANES, (p + 1) * LANES)
            qn_ref[p] = (qn2[:, sl] * gnq_ref[...]).astype(_BF)
            kn_ref[p] = (kn2[:, sl] * gnk_ref[...]).astype(_BF)
            vn_ref[p] = z[:, 512 + p * LANES: 512 + (p + 1) * LANES].astype(_BF)
        qn_ref[2] = (qkn2[:, :LANES] * gnq_ref[...]).astype(_BF)
        kn_ref[2] = (qkn2[:, LANES:] * gnk_ref[...]).astype(_BF)
        vn_ref[2] = z[:, 1024:1152].astype(_BF)

    def put(out_ref, p, val, rate, slot):
        if rate == 1:
            out_ref[p] = val.astype(_BF)
        else:
            rows = tm // rate
            perm_ref[slot] = val
            for rho in range(rate):
                out_ref[p, rho * rows:(rho + 1) * rows, :] = (
                    perm_ref[slot, pl.ds(rho, rows, stride=rate), :].astype(_BF))

    def dilated(g, z):
        q2 = z[:, 0:256]
        k2 = z[:, 256:512]
        qn2 = q2 * _seg_inv(q2, bdr_ref, HEAD_DIM)
        kn2 = k2 * _seg_inv(k2, bdr_ref, HEAD_DIM)
        for pp in range(2):
            p = 2 * g + pp
            sl = slice(pp * LANES, (pp + 1) * LANES)
            qx = qn2[:, sl] * gdq_ref[...]
            qx = qx * tdc_ref[...] + pltpu.roll(qx, 64, 1) * tds_ref[...]
            kx = kn2[:, sl] * gdk_ref[...]
            kx = kx * tdc_ref[...] + pltpu.roll(kx, 64, 1) * tds_ref[...]
            put(qd_ref, p, qx, DIL_RATES[g], 3 * pp)
            put(kd_ref, p, kx, DIL_RATES[g], 3 * pp + 1)
            put(vd_ref, p, z[:, 512 + pp * LANES: 512 + (pp + 1) * LANES], DIL_RATES[g], 3 * pp + 2)

    dil_lo = [COL_DIL + g * 768 for g in range(DIL_GROUPS)]
    stages = [(dil_lo[2], dil_lo[2] + 768, functools.partial(dilated, 2)),
              (dil_lo[1], dil_lo[1] + 768, functools.partial(dilated, 1)),
              (dil_lo[0], dil_lo[0] + 768, functools.partial(dilated, 0)),
              (COL_MLA, COL_NA, latent),
              (COL_NA, COL_DIL, neighbourhood)]
    z_next = project(stages[0][0], stages[0][1])
    for idx, (_, _, finish) in enumerate(stages):
        z = z_next
        if idx + 1 < len(stages):
            z_next = project(stages[idx + 1][0], stages[idx + 1][1])
        finish(z)


def _const_spec(shape):
    nd = len(shape)
    return pl.BlockSpec(shape, lambda *_, _nd=nd: (0,) * _nd)


def _layer_spec(arr, layer):
    nd = arr.ndim
    return pl.BlockSpec((None,) + arr.shape[1:], lambda *_, _l=layer, _nd=nd: (_l,) + (0,) * (_nd - 1))


def _mix_in(x, lw, layer, tabs, seq):
    t = x.shape[0]
    nblk_seq = seq // TM
    tab_spec = pl.BlockSpec((TM, LANES), lambda i: (i % nblk_seq, 0))
    weights = [lw[n] for n in ('g_mix', 'w_in', 'g_cq', 'g_ckv', 'w_uq', 'w_uk', 'w_uv',
                               'g_mq', 'g_mk', 'g_kr', 'g_nq', 'g_nk', 'g_dq', 'g_dk')]
    joins = _head_join_matrices()
    tables = [tabs['q_c'], tabs['q_s1'], tabs['q_s2'], tabs['k_c'], tabs['k_s'], tabs['d_c'], tabs['d_s']]

    def out(n):
        return (jax.ShapeDtypeStruct((n, t, LANES), _BF), pl.BlockSpec((n, TM, LANES), lambda i: (0, i, 0)))

    vt_out = (jax.ShapeDtypeStruct((MLA_HEADS, t // TM, LANES, TM), _BF),
              pl.BlockSpec((MLA_HEADS, 1, LANES, TM), lambda i: (0, i, 0, 0)))
    outs = [out(6), out(6), vt_out, out(3), out(3), out(3), out(6), out(6), out(6)]
    return pl.pallas_call(
        _mix_in_kernel,
        grid=(t // TM,),
        in_specs=[pl.BlockSpec((TM, D_MODEL), lambda i: (i, 0))]
        + [_layer_spec(w, layer) for w in weights] + [_const_spec(m.shape) for m in joins]
        + [tab_spec] * len(tables),
        out_specs=[o[1] for o in outs],
        out_shape=[o[0] for o in outs],
        scratch_shapes=[pltpu.VMEM((6, TM, LANES), _F32)],
        compiler_params=pltpu.CompilerParams(dimension_semantics=("parallel",), vmem_limit_bytes=VMEM_LIMIT),
        name="mix_in",
    )(x, *weights, *joins, *tables)


def _mla_kernel(q_ref, k_ref, vt_ref, o_ref, sa_sc, sb_sc, m_sc, acc_sc, *, tk):
    tq = q_ref.shape[1]
    nk = k_ref.shape[1] // tk
    sub = tk // TM
    m_sc[...] = jnp.full(m_sc.shape, NEG, _F32)
    acc_sc[...] = jnp.zeros(acc_sc.shape, _F32)

    def scores(j, dst):
        rows = pl.ds(pl.multiple_of(j * tk, tk), tk)
        for hh in range(2):
            dst[hh] = _dot_t(k_ref[hh, rows, :], q_ref[hh])

    def consume(j, src):
        for hh in range(2):
            s = src[hh]
            m_prev = m_sc[hh]
            m_new = jnp.maximum(m_prev, jnp.max(s, axis=0, keepdims=True))
            p = jnp.exp2(s - m_new).astype(_BF)
            vt = jnp.concatenate([vt_ref[hh, j * sub + c] for c in range(sub)], axis=1)
            acc_sc[hh] = jnp.exp2(m_prev - m_new) * acc_sc[hh] + _dot(vt, p)
            m_sc[hh] = m_new

    def pair(i, carry):
        j = 2 * i
        scores(j + 1, sb_sc)
        consume(j, sa_sc)
        scores(j + 2, sa_sc)
        consume(j + 1, sb_sc)
        return carry

    scores(0, sa_sc)
    lax.fori_loop(0, nk // 2 - 1, pair, 0)
    scores(nk - 1, sb_sc)
    consume(nk - 2, sa_sc)
    consume(nk - 1, sb_sc)

    lane = _lane((tq, LANES))
    outs = []
    for hh in range(2):
        acc = acc_sc[hh]
        outs.append((acc / acc[MLA_V:MLA_V + 1, :]).T)
    o_ref[0] = jnp.where(lane < MLA_V, outs[0], pltpu.roll(outs[1], MLA_V, 1)).astype(o_ref.dtype)


def _mla(q, k, vt, batch, seq, tq=2048, tk=512):
    t = q.shape[1]
    nq = seq // tq
    nblk = seq // TM
    assert seq % (2 * tk) == 0 and tk % TM == 0
    return pl.pallas_call(
        functools.partial(_mla_kernel, tk=tk),
        grid=(batch, MLA_HEADS // 2, nq),
        in_specs=[pl.BlockSpec((2, tq, LANES), lambda b, p, i: (p, b * nq + i, 0)),
                  pl.BlockSpec((2, seq, LANES), lambda b, p, i: (p, b, 0)),
                  pl.BlockSpec((2, nblk, LANES, TM), lambda b, p, i: (p, b, 0, 0))],
        out_specs=pl.BlockSpec((1, tq, LANES), lambda b, p, i: (p, b * nq + i, 0)),
        out_shape=jax.ShapeDtypeStruct((MLA_HEADS // 2, t, LANES), _BF),
        scratch_shapes=[pltpu.VMEM((2, tk, tq), _F32), pltpu.VMEM((2, tk, tq), _F32),
                        pltpu.VMEM((2, 1, tq), _F32), pltpu.VMEM((2, LANES, tq), _F32)],
        compiler_params=pltpu.CompilerParams(
            dimension_semantics=("parallel", "parallel", "arbitrary"), vmem_limit_bytes=VMEM_LIMIT),
        name="mla_attention",
    )(q, k, vt)


def _na_kernel(q_ref, k_ref, v_ref, bias_ref, o_ref, *, rows):
    i = pl.program_id(2)
    lane = _lane((GRID_W, LANES))
    first = lane < HEAD_DIM
    nkeys = NA_WIN_R * GRID_W

    scores, windows = [], []
    for rr in range(NA_RB):
        r = i * NA_RB + rr
        rs = jnp.clip(r - NA_WIN_R // 2, 0, rows - NA_WIN_R)
        start = pl.multiple_of(rs * GRID_W, GRID_W)
        q = q_ref[0, rr * GRID_W:(rr + 1) * GRID_W, :]
        zero = jnp.zeros_like(q)
        q2 = jnp.concatenate([jnp.where(first, q, zero), jnp.where(first, zero, q)], axis=0)
        scores.append(_dot_t(q2, k_ref[0, pl.ds(start, nkeys), :]) + bias_ref[0, r - rs])
        windows.append(start)
    probs = []
    for s in scores:
        p = jnp.exp(s - jnp.max(s, axis=1, keepdims=True))
        probs.append((p.astype(_BF), jnp.sum(p, axis=1, keepdims=True)))
    for rr, ((p, l), start) in enumerate(zip(probs, windows)):
        o2 = _dot(p, v_ref[0, pl.ds(start, nkeys), :]) / l
        o = jnp.where(first, o2[:GRID_W], o2[GRID_W:])
        o_ref[0, rr * GRID_W:(rr + 1) * GRID_W, :] = o.astype(o_ref.dtype)


def _na(q, k, v, bias, layer, batch, seq):
    t = q.shape[1]
    rows = seq // GRID_W
    assert rows >= NA_WIN_R and rows % NA_RB == 0
    nrb = rows // NA_RB
    qrows = NA_RB * GRID_W
    return pl.pallas_call(
        functools.partial(_na_kernel, rows=rows),
        grid=(batch, NA_HEADS // 2, nrb),
        in_specs=[pl.BlockSpec((1, qrows, LANES), lambda b, p, i: (p, b * nrb + i, 0)),
                  pl.BlockSpec((1, seq, LANES), lambda b, p, i: (p, b, 0)),
                  pl.BlockSpec((1, seq, LANES), lambda b, p, i: (p, b, 0)),
                  pl.BlockSpec((None, 1, NA_WIN_R, 2 * GRID_W, NA_WIN_R * GRID_W),
                               lambda b, p, i: (layer, p, 0, 0, 0))],
        out_specs=pl.BlockSpec((1, qrows, LANES), lambda b, p, i: (p, b * nrb + i, 0)),
        out_shape=jax.ShapeDtypeStruct((NA_HEADS // 2, t, LANES), _BF),
        compiler_params=pltpu.CompilerParams(
            dimension_semantics=("parallel", "parallel", "arbitrary"), vmem_limit_bytes=VMEM_LIMIT),
        name="na_attention",
    )(q, k, v, bias)


def _dil_kernel(q0_ref, q1_ref, q2_ref, k0_ref, k1_ref, k2_ref, v0_ref, v1_ref, v2_ref, band_ref,
                o_ref, og_sc, lse_sc, *, seq):
    sb = pl.program_id(2)
    q_refs = (q0_ref, q1_ref, q2_ref)
    k_refs = (k0_ref, k1_ref, k2_ref)
    v_refs = (v0_ref, v1_ref, v2_ref)
    lane = _lane((DIL_TQ, LANES))
    first_qk = (lane % HEAD_DIM) < (HEAD_DIM // 2)
    first_v = lane < HEAD_DIM
    n_tiles = DIL_SB // DIL_TQ
    half = DIL_TQ // 2

    for g in range(DIL_GROUPS):
        rate = DIL_RATES[g]
        length = seq // rate
        per_blk = TM // rate
        chunk = min(half, per_blk)
        n_chunks = (2 * DIL_TQ) // chunk
        tiles_per_res = n_tiles // rate
        q_ref, k_ref, v_ref = q_refs[g], k_refs[g], v_refs[g]

        def row_of(m, rho, _per_blk=per_blk):
            return (m // _per_blk) * TM + rho * _per_blk + (m % _per_blk)

        def body(tg, carry, _g=g, _rate=rate, _length=length, _per_blk=per_blk, _chunk=chunk,
                 _n_chunks=n_chunks, _tpr=tiles_per_res, _q=q_ref, _k=k_ref, _v=v_ref):
            scores, metas = [], []
            for u in range(DIL_UNROLL):
                tt = tg * DIL_UNROLL + u
                rho = tt // _tpr
                mt = tt % _tpr
                m0 = sb * (DIL_SB // _rate) + mt * DIL_TQ
                q = jnp.concatenate(
                    [_q[0, pl.ds(pl.multiple_of(row_of(m0 + c * _chunk, rho), _chunk), _chunk), :]
                     for c in range(DIL_TQ // _chunk)], axis=0)
                zero = jnp.zeros_like(q)
                q2 = jnp.concatenate([jnp.where(first_qk, q, zero), jnp.where(first_qk, zero, q)], axis=0)
                win = []
                for c in range(_n_chunks):
                    ms = jnp.clip(m0 - half + c * _chunk, 0, _length - _chunk)
                    win.append(pl.multiple_of(row_of(ms, rho), _chunk))
                kw = jnp.concatenate([_k[0, pl.ds(row, _chunk), :] for row in win], axis=0)
                edge = (m0 == 0).astype(jnp.int32) + 2 * (m0 == _length - DIL_TQ).astype(jnp.int32)
                scores.append(_dot_t(q2, kw) + band_ref[edge])
                metas.append((rho + _rate * mt * DIL_TQ, win))
            probs = []
            for s in scores:
                m = jnp.max(s, axis=1, keepdims=True)
                p = jnp.exp(s - m)
                probs.append((p.astype(_BF), m, jnp.sum(p, axis=1, keepdims=True)))
            for (p, m, l), (base, win) in zip(probs, metas):
                vw = jnp.concatenate([_v[0, pl.ds(row, _chunk), :] for row in win], axis=0)
                o2 = _dot(p, vw) / l
                lse2 = jnp.broadcast_to(m + jnp.log(l), (2 * DIL_TQ, LANES))
                if _rate == 1:
                    dst = pl.ds(pl.multiple_of(base, DIL_TQ), DIL_TQ)
                else:
                    dst = pl.ds(base, DIL_TQ, stride=_rate)
                og_sc[_g, dst, :] = jnp.where(first_v, o2[:DIL_TQ], o2[DIL_TQ:])
                lse_sc[_g, dst, :] = jnp.where(first_v, lse2[:DIL_TQ], lse2[DIL_TQ:])
            return carry

        lax.fori_loop(0, n_tiles // DIL_UNROLL, body, 0)

    m = jnp.maximum(jnp.maximum(lse_sc[0], lse_sc[1]), lse_sc[2])
    w0 = jnp.exp(lse_sc[0] - m)
    w1 = jnp.exp(lse_sc[1] - m)
    w2 = jnp.exp(lse_sc[2] - m)
    o = (w0 * og_sc[0] + w1 * og_sc[1] + w2 * og_sc[2]) / (w0 + w1 + w2)
    o_ref[0] = o.astype(o_ref.dtype)


def _dil(q, k, v, band, batch, seq):
    t = q.shape[1]
    assert seq % DIL_SB == 0
    nsb = seq // DIL_SB

    def q_spec(g):
        return pl.BlockSpec((1, DIL_SB, LANES), lambda b, sp, s, _g=g: (2 * _g + sp, b * nsb + s, 0))

    def kv_spec(g):
        return pl.BlockSpec((1, seq, LANES), lambda b, sp, s, _g=g: (2 * _g + sp, b, 0))

    return pl.pallas_call(
        functools.partial(_dil_kernel, seq=seq),
        grid=(batch, 2, nsb),
        in_specs=[kv_spec(0), kv_spec(1), kv_spec(2), kv_spec(0), kv_spec(1), kv_spec(2),
                  kv_spec(0), kv_spec(1), kv_spec(2),
                  pl.BlockSpec((4, 2 * DIL_TQ, 2 * DIL_TQ), lambda b, sp, s: (0, 0, 0))],
        out_specs=pl.BlockSpec((1, DIL_SB, LANES), lambda b, sp, s: (sp, b * nsb + s, 0)),
        out_shape=jax.ShapeDtypeStruct((2, t, LANES), _BF),
        scratch_shapes=[pltpu.VMEM((DIL_GROUPS, DIL_SB, LANES), _F32),
                        pltpu.VMEM((DIL_GROUPS, DIL_SB, LANES), _F32)],
        compiler_params=pltpu.CompilerParams(
            dimension_semantics=("parallel", "parallel", "arbitrary"), vmem_limit_bytes=VMEM_LIMIT),
        name="dil_attention",
    )(q, q, q, k, k, k, v, v, v, band)


def _mem_kv_kernel(mem_ref, gmem_ref, wckv_ref, gxk_ref, k_ref, v_ref):
    hm = _rms(mem_ref[0], gmem_ref[...]).astype(_BF)
    kv = _dot(hm, wckv_ref[...])
    for hd in range(X_HEADS):
        kh = kv[:, hd * X_HEAD_DIM:(hd + 1) * X_HEAD_DIM]
        k_ref[0, :, hd * X_HEAD_DIM:(hd + 1) * X_HEAD_DIM] = _rms(kh, gxk_ref[...]).astype(_BF)
    v_ref[0] = kv[:, D_MODEL:].astype(_BF)


def _mem_kv(mem, lw, layer):
    b, m, _ = mem.shape
    return pl.pallas_call(
        _mem_kv_kernel,
        grid=(b,),
        in_specs=[pl.BlockSpec((1, m, D_MODEL), lambda i: (i, 0, 0))]
        + [_layer_spec(lw[n], layer) for n in ('g_mem', 'w_ckv', 'g_xk')],
        out_specs=[pl.BlockSpec((1, m, D_MODEL), lambda i: (i, 0, 0))] * 2,
        out_shape=[jax.ShapeDtypeStruct((b, m, D_MODEL), _BF)] * 2,
        compiler_params=pltpu.CompilerParams(dimension_semantics=("parallel",), vmem_limit_bytes=VMEM_LIMIT),
        name="mem_kv",
    )(mem, lw['g_mem'], lw['w_ckv'], lw['g_xk'])


def _mix_out_kernel(x_ref, oa_ref, ob_ref, oc_ref, wo_ref, gcross_ref, wcq_ref, gxq_ref,
                    mk_ref, mv_ref, wco_ref, y_ref):
    mix = jnp.concatenate([oa_ref[p] for p in range(3)] + [ob_ref[p] for p in range(3)]
                          + [oc_ref[p] for p in range(2)], axis=1)
    x1 = x_ref[...] + _dot(mix, wo_ref[...])
    hc = _rms(x1, gcross_ref[...]).astype(_BF)
    q = _dot(hc, wcq_ref[...])
    outs = []
    for hd in range(X_HEADS):
        sl = slice(hd * X_HEAD_DIM, (hd + 1) * X_HEAD_DIM)
        qh = (_rms(q[:, sl], gxq_ref[...]) * X_HEAD_DIM ** -0.5).astype(_BF)
        s = _dot_t(qh, mk_ref[0, :, sl])
        m = jnp.max(s, axis=1, keepdims=True)
        p = jnp.exp(s - m)
        l = jnp.sum(p, axis=1, keepdims=True)
        outs.append((_dot(p.astype(_BF), mv_ref[0, :, sl]) / l).astype(_BF))
    o = jnp.concatenate(outs, axis=1)
    y_ref[...] = x1 + _dot(o, wco_ref[...])


def _mix_out(x, oa, ob, oc, mk, mv, lw, layer, seq):
    t = x.shape[0]
    nblk_seq = seq // TM
    mem_len = mk.shape[1]

    def grp(n):
        return pl.BlockSpec((n, TM, LANES), lambda i: (0, i, 0))

    mem_spec = pl.BlockSpec((1, mem_len, D_MODEL), lambda i: (i // nblk_seq, 0, 0))
    return pl.pallas_call(
        _mix_out_kernel,
        grid=(t // TM,),
        in_specs=[pl.BlockSpec((TM, D_MODEL), lambda i: (i, 0)), grp(3), grp(3), grp(2),
                  _layer_spec(lw['w_o'], layer), _layer_spec(lw['g_cross'], layer),
                  _layer_spec(lw['w_cq'], layer), _layer_spec(lw['g_xq'], layer),
                  mem_spec, mem_spec, _layer_spec(lw['w_co'], layer)],
        out_specs=pl.BlockSpec((TM, D_MODEL), lambda i: (i, 0)),
        out_shape=jax.ShapeDtypeStruct((t, D_MODEL), _F32),
        compiler_params=pltpu.CompilerParams(dimension_semantics=("parallel",), vmem_limit_bytes=VMEM_LIMIT),
        name="mix_out",
    )(x, oa, ob, oc, lw['w_o'], lw['g_cross'], lw['w_cq'], lw['g_xq'], mk, mv, lw['w_co'])


def _ffn_kernel(x_ref, xp_ref, xn_ref, gffn_ref, wup_ref, cw_ref, cb_ref, wdn_ref, y_ref, *, nblk_seq):
    i = pl.program_id(0)
    tm = x_ref.shape[0]
    x = x_ref[...]
    g = gffn_ref[...]
    is_first = (i % nblk_seq) == 0
    is_last = (i % nblk_seq) == nblk_seq - 1
    hp = jnp.where(is_first, 0.0, _rms(xp_ref[...], g))
    hn = jnp.where(is_last, 0.0, _rms(xn_ref[...], g))
    h = jnp.concatenate([hp, _rms(x, g), hn], axis=0).astype(_BF)
    acc = x

    def up_proj(c):
        return [_dot(h, wup_ref[:, part * D_FF + c * FF_CH: part * D_FF + (c + 1) * FF_CH]) for part in range(2)]

    def conv(u, lo):
        cw = cw_ref[:, lo:lo + FF_CH]
        up = pltpu.roll(u, 1, 0)[8:8 + tm]
        un = pltpu.roll(u, tm + 15, 0)[8:8 + tm]
        return up * cw[0:1] + u[8:8 + tm] * cw[1:2] + un * cw[2:3] + cb_ref[:, lo:lo + FF_CH]

    pending = []
    u_next = up_proj(0)
    for c in range(N_FF_CH):
        u_val, u_gate = u_next
        if c + 1 < N_FF_CH:
            u_next = up_proj(c + 1)
        a = conv(u_val, c * FF_CH)
        half_gate = conv(u_gate, D_FF + c * FF_CH)
        pending.append((half_gate * (1.0 + jnp.tanh(half_gate)) * a).astype(_BF))
        rest = N_FF_CH - 1 - c
        if len(pending) == 2 and rest != 1 or rest == 0:
            lo = (c + 1 - len(pending)) * FF_CH
            act = pending[0] if len(pending) == 1 else jnp.concatenate(pending, axis=1)
            acc = acc + _dot(act, wdn_ref[lo:(c + 1) * FF_CH, :])
            pending = []
    y_ref[...] = acc


def _ffn(x, lw, layer, seq):
    t = x.shape[0]
    nblk_seq = seq // TM
    nhalo = t // 8
    per = TM // 8
    return pl.pallas_call(
        functools.partial(_ffn_kernel, nblk_seq=nblk_seq),
        grid=(t // TM,),
        in_specs=[pl.BlockSpec((TM, D_MODEL), lambda i: (i, 0)),
                  pl.BlockSpec((8, D_MODEL), lambda i: (jnp.maximum(i * per - 1, 0), 0)),
                  pl.BlockSpec((8, D_MODEL), lambda i: (jnp.minimum((i + 1) * per, nhalo - 1), 0))]
        + [_layer_spec(lw[n], layer) for n in ('g_ffn', 'w_up', 'conv_w', 'conv_b', 'w_down')],
        out_specs=pl.BlockSpec((TM, D_MODEL), lambda i: (i, 0)),
        out_shape=jax.ShapeDtypeStruct((t, D_MODEL), _F32),
        compiler_params=pltpu.CompilerParams(dimension_semantics=("parallel",), vmem_limit_bytes=VMEM_LIMIT),
        name="conv_ffn",
    )(x, x, x, lw['g_ffn'], lw['w_up'], lw['conv_w'], lw['conv_b'], lw['w_down'])


def _relayout_w_in(w):
    nl = w.shape[0]
    lead = w[..., :SPLIT_CKV]
    a = w[..., SPLIT_CKV:SPLIT_CKV + 16]
    b = w[..., SPLIT_CKV + 16:SPLIT_KR]
    z32 = jnp.zeros(a.shape[:-1] + (32,), w.dtype)
    kr = jnp.concatenate([b, a, z32, a, b, z32], axis=-1)
    na = w[..., SPLIT_KR:SPLIT_NA].reshape(nl, D_MODEL, 3, NA_HEADS // 2, LANES)
    na_ab = na[:, :, :, 0:2].reshape(nl, D_MODEL, 3 * 2 * LANES)
    na_c = na[:, :, :, 2].reshape(nl, D_MODEL, 3 * LANES)
    dl = w[..., SPLIT_NA:].reshape(nl, D_MODEL, 3, DIL_GROUPS, 2, 2, 2, HEAD_DIM // 2)
    qk = dl[:, :, 0:2].transpose(0, 1, 3, 2, 4, 6, 5, 7).reshape(nl, D_MODEL, DIL_GROUPS, 4 * LANES)
    v = dl[:, :, 2].reshape(nl, D_MODEL, DIL_GROUPS, 2 * LANES)
    dil = jnp.concatenate([qk, v], axis=-1).reshape(nl, D_MODEL, DIL_GROUPS * 6 * LANES)
    out = jnp.concatenate([lead, kr, na_ab, na_c, dil], axis=-1)
    assert out.shape[-1] == D_IN_P
    return out


def _pad_heads(w, real, n_heads):
    nl, k, _ = w.shape
    w = w.reshape(nl, k, n_heads, real)
    return jnp.pad(w, ((0, 0), (0, 0), (0, 0), (0, LANES - real))).reshape(nl, k, n_heads * LANES)


def _rows(v):
    return v[:, None, :].astype(_F32)


def _prep_layers(p):
    nl = p['w_in'].shape[0]
    kn, qn = p['mla_kn'], p['mla_qn']
    z32 = jnp.zeros((nl, 32), _F32)
    z64 = jnp.zeros((nl, 64), _F32)
    dq, dk = p['dil_qn'], p['dil_kn']

    def rot(gain):
        return jnp.concatenate([gain[:, :32], gain[:, :32], gain[:, 32:], gain[:, 32:]], axis=-1)

    gate_half = jnp.concatenate([jnp.ones((D_FF,), _F32), jnp.full((D_FF,), 0.5, _F32)])

    return {
        'g_mix': _rows(p['norm_mix']), 'w_in': _relayout_w_in(p['w_in'].astype(_BF)),
        'g_cq': _rows(p['mla_q_norm']), 'g_ckv': _rows(p['mla_kv_norm']),
        'w_uq': _pad_heads(p['w_uq'].astype(_BF), MLA_QK, MLA_HEADS),
        'w_uk': _pad_heads(p['w_uk'].astype(_BF), MLA_NOPE, MLA_HEADS),
        'w_uv': _pad_heads(p['w_uv'].astype(_BF), MLA_V, MLA_HEADS),
        'g_mq': _rows(jnp.concatenate([qn, z32], axis=-1) * (MLA_QK ** -0.5 * LOG2E)),
        'g_mk': _rows(jnp.concatenate([kn[:, :MLA_NOPE], z64], axis=-1)),
        'g_kr': _rows(jnp.concatenate([kn[:, 80:96], kn[:, 64:80], z32, kn[:, 64:80], kn[:, 80:96], z32], axis=-1)),
        'g_nq': _rows(jnp.tile(p['na_qn'], (1, 2)) * HEAD_DIM ** -0.5), 'g_nk': _rows(jnp.tile(p['na_kn'], (1, 2))),
        'g_dq': _rows(rot(dq) * HEAD_DIM ** -0.5), 'g_dk': _rows(rot(dk)),
        'w_o': p['w_o'].astype(_BF),
        'g_cross': _rows(p['norm_cross']), 'w_cq': p['w_cq'].astype(_BF),
        'g_xq': _rows(p['x_qn']), 'g_xk': _rows(p['x_kn']), 'g_mem': _rows(p['norm_mem']),
        'w_ckv': p['w_ckv'].astype(_BF), 'w_co': p['w_co'].astype(_BF),
        'g_ffn': _rows(p['norm_ffn']), 'w_up': p['w_up'].astype(_BF),
        'conv_w': p['conv_w'] * gate_half, 'conv_b': _rows(p['conv_b'] * gate_half),
        'w_down': p['w_down'].astype(_BF),
        'na_bias': _na_bias(p['na_rpb']),
    }


def _na_bias(rpb):
    nl = rpb.shape[0]
    dl = np.arange(NA_WIN_R)
    j = np.arange(NA_WIN_R)
    r_off = j[None, :] - dl[:, None] + (NA_WIN_R - 1)
    qc = np.arange(GRID_W)
    kc = np.arange(GRID_W)
    c_start = np.clip(qc - NA_WIN_C // 2, 0, GRID_W - NA_WIN_C)
    valid = (kc[None, :] >= c_start[:, None]) & (kc[None, :] < c_start[:, None] + NA_WIN_C)
    c_off = kc[None, :] - qc[:, None] + (NA_WIN_C - 1)
    pick = (np.arange(2 * NA_WIN_C - 1)[:, None, None] == c_off[None]) & valid[None]
    rows = rpb[:, :, r_off].reshape(nl, NA_HEADS // 2, 2, NA_WIN_R, NA_WIN_R, 2 * NA_WIN_C - 1)
    b = jnp.einsum('lpedjc,cqk->lpdeqjk', rows, jnp.asarray(pick, _F32), precision=lax.Precision.HIGHEST)
    b = b + jnp.asarray(np.where(valid, 0.0, NEG), _F32)[:, None, :]
    return b.reshape(nl, NA_HEADS // 2, NA_WIN_R, 2 * GRID_W, NA_WIN_R * GRID_W)


def _band_bias():
    i = np.arange(DIL_TQ)[:, None]
    c = np.arange(2 * DIL_TQ)[None, :]
    band = (c >= i) & (c <= i + DIL_TQ)
    out = []
    for edge in range(4):
        ok = band
        if edge & 1:
            ok = ok & (c >= DIL_TQ // 2)
        if edge & 2:
            ok = ok & (c < 2 * DIL_TQ - DIL_TQ // 2)
        out.append(np.tile(np.where(ok, 0.0, NEG), (2, 1)))
    return jnp.asarray(np.stack(out), _F32)


def _rope_tables(seq):
    pos = jnp.arange(seq, dtype=_F32)[:, None]

    def cs(half):
        inv = ROPE_THETA ** (-jnp.arange(half, dtype=_F32) / half)
        ang = pos * inv[None, :]
        return jnp.cos(ang), jnp.sin(ang)

    c16, s16 = cs(MLA_ROPE // 2)
    c32, s32 = cs(HEAD_DIM // 2)
    one = jnp.ones((seq, 64), _F32)
    z16 = jnp.zeros((seq, 16), _F32)
    z32 = jnp.zeros((seq, 32), _F32)
    z64 = jnp.zeros((seq, 64), _F32)
    return {
        'q_c': jnp.concatenate([one, c16, c16, z32], axis=1),
        'q_s1': jnp.concatenate([z64, z16, s16, z32], axis=1),
        'q_s2': jnp.concatenate([z64, -s16, z16, z32], axis=1),
        'k_c': jnp.concatenate([z64, c16, c16, z32], axis=1),
        'k_s': jnp.concatenate([z64, -s16, s16, z32], axis=1),
        'd_c': jnp.concatenate([c32, c32, c32, c32], axis=1),
        'd_s': jnp.concatenate([-s32, -s32, s32, s32], axis=1),
    }


def _trunk_layer(x, mem, lw, layer, tabs, band, batch, seq):
    qm, km, vm, qn, kn, vn, qd, kd, vd = _mix_in(x, lw, layer, tabs, seq)
    oa = _mla(qm, km, vm, batch, seq)
    ob = _na(qn, kn, vn, lw['na_bias'], layer, batch, seq)
    oc = _dil(qd, kd, vd, band, batch, seq)
    mk, mv = _mem_kv(mem, lw, layer)
    x = _mix_out(x, oa, ob, oc, mk, mv, lw, layer, seq)
    return _ffn(x, lw, layer, seq)


def kernel(x_prompt, x_sample, mem_prompt, mem_sample, norm_mix, w_in, mla_q_norm, mla_kv_norm, w_uq, w_uk, w_uv, mla_qn, mla_kn, na_qn, na_kn, na_rpb, dil_qn, dil_kn, w_o, norm_cross, norm_mem, w_cq, w_ckv, x_qn, x_kn, w_co, norm_ffn, w_up, conv_w, conv_b, w_down):
    stacked = {
        'norm_mix': norm_mix, 'w_in': w_in, 'mla_q_norm': mla_q_norm, 'mla_kv_norm': mla_kv_norm,
        'w_uq': w_uq, 'w_uk': w_uk, 'w_uv': w_uv, 'mla_qn': mla_qn, 'mla_kn': mla_kn,
        'na_qn': na_qn, 'na_kn': na_kn, 'na_rpb': na_rpb, 'dil_qn': dil_qn, 'dil_kn': dil_kn,
        'w_o': w_o, 'norm_cross': norm_cross, 'norm_mem': norm_mem, 'w_cq': w_cq, 'w_ckv': w_ckv,
        'x_qn': x_qn, 'x_kn': x_kn, 'w_co': w_co, 'norm_ffn': norm_ffn, 'w_up': w_up,
        'conv_w': conv_w, 'conv_b': conv_b, 'w_down': w_down,
    }
    lw = _prep_layers(stacked)
    band = _band_bias()
    outs = []
    for x, mem in ((x_prompt, mem_prompt), (x_sample, mem_sample)):
        batch, seq, _ = x.shape
        tabs = _rope_tables(seq)
        xt = x.reshape(batch * seq, D_MODEL)
        for layer in range(DEPTH):
            xt = _trunk_layer(xt, mem, lw, layer, tabs, band, batch, seq)
        outs.append(xt.reshape(batch, seq, D_MODEL))
    return tuple(outs)
```

```python
import functools

import jax
import jax.numpy as jnp
import numpy as np
from jax import lax
from jax.experimental import pallas as pl
from jax.experimental.pallas import tpu as pltpu

D_MODEL = 1024
DEPTH = 4
GRID_W = 64
HEAD_DIM = 64
ROPE_THETA = 10000.0
EPS = 1e-6
MLA_HEADS = 6
MLA_Q_RANK = 256
MLA_KV_RANK = 128
MLA_NOPE = 64
MLA_ROPE = 32
MLA_V = 64
MLA_QK = MLA_NOPE + MLA_ROPE
NA_HEADS = 6
NA_WIN_R = 8
NA_WIN_C = 16
DIL_WINDOWS = (128, 512, 2048)
DIL_RATES = (1, 4, 16)
DIL_GROUPS = 3
DIL_HEADS = 4
DIL_ALL = DIL_GROUPS * DIL_HEADS
X_HEADS = 4
X_HEAD_DIM = D_MODEL // X_HEADS
D_FF = 2816
CONV_W = 3
SPLIT_CQ = MLA_Q_RANK
SPLIT_CKV = SPLIT_CQ + MLA_KV_RANK
SPLIT_KR = SPLIT_CKV + MLA_ROPE
SPLIT_NA = SPLIT_KR + 3 * NA_HEADS * HEAD_DIM
D_IN = SPLIT_NA + 3 * DIL_ALL * HEAD_DIM

LANES = 128
NEG = -1e30
LOG2E = 1.4426950408889634
TM = 512
DIL_TQ = 128
DIL_SB = DIL_TQ * max(DIL_RATES)
DIL_UNROLL = 16
NA_RB = 32
FF_CH = 256
N_FF_CH = D_FF // FF_CH
VMEM_LIMIT = 56 * 1024 * 1024

COL_MLA = 0
COL_NA = 512
COL_DIL = COL_NA + 3 * NA_HEADS * HEAD_DIM
D_IN_P = COL_DIL + 3 * DIL_ALL * HEAD_DIM

_BF = jnp.bfloat16
_F32 = jnp.float32


def _dot(a, b):
    return jnp.dot(a, b, preferred_element_type=_F32)


def _dot_t(a, b):
    return lax.dot_general(a, b, (((1,), (1,)), ((), ())), preferred_element_type=_F32)


def _rms(x, g):
    ms = jnp.mean(x * x, axis=-1, keepdims=True)
    return x * lax.rsqrt(ms + EPS) * g


def _lane(shape):
    return lax.broadcasted_iota(jnp.int32, shape, len(shape) - 1)


def _head_join_matrices():
    i = np.arange(2 * LANES)
    same_group = (i[:, None] // LANES) == (i[None, :] // LANES)
    natural = (i[:, None] // HEAD_DIM) == (i[None, :] // HEAD_DIM)
    half = ((i % HEAD_DIM) < HEAD_DIM // 2)
    rotary = same_group & (half[:, None] == half[None, :])
    return [jnp.asarray(m, _BF) for m in (same_group, natural, rotary)]


def _seg_inv(x2, bd_ref, dim):
    ss = _dot((x2 * x2).astype(_BF), bd_ref[...])
    return lax.rsqrt(ss * (1.0 / dim) + EPS)


def _mix_in_kernel(x_ref, gmix_ref, win_ref, gcq_ref, gckv_ref, wuq_ref, wuk_ref, wuv_ref,
                   gmq_ref, gmk_ref, gkr_ref, gnq_ref, gnk_ref, gdq_ref, gdk_ref,
                   bd128_ref, bd64_ref, bdr_ref,
                   tqc_ref, tqs1_ref, tqs2_ref, tkc_ref, tks_ref, tdc_ref, tds_ref,
                   qm_ref, km_ref, vm_ref, qn_ref, kn_ref, vn_ref, qd_ref, kd_ref, vd_ref,
                   perm_ref):
    tm = x_ref.shape[0]
    x = x_ref[...]
    xg = (x * gmix_ref[...]).astype(_BF)
    inv_x = lax.rsqrt(jnp.mean(x * x, axis=-1, keepdims=True) + EPS)
    lane = _lane((tm, LANES))

    def project(lo, hi):
        return _dot(xg, win_ref[:, lo:hi]) * inv_x

    def latent(z):
        cq = _rms(z[:, 0:MLA_Q_RANK], gcq_ref[...]).astype(_BF)
        ckv = _rms(z[:, MLA_Q_RANK:MLA_Q_RANK + MLA_KV_RANK], gckv_ref[...]).astype(_BF)
        kr = z[:, 384:512]
        kr_own = jnp.where((lane >= MLA_NOPE) & (lane < MLA_QK), kr, 0.0)
        krg = kr * gkr_ref[...]
        kr_rot = krg * tkc_ref[...] + pltpu.roll(krg, 64, 1) * tks_ref[...]
        qpre = _dot(cq, wuq_ref[...])
        kpre = _dot(ckv, wuk_ref[...])
        vpre = _dot(ckv, wuv_ref[...])
        row_t = lax.broadcasted_iota(jnp.int32, (LANES, tm), 0)
        for c in range(MLA_HEADS // 2):
            q2 = qpre[:, 2 * c * LANES:(2 * c + 2) * LANES]
            k2 = kpre[:, 2 * c * LANES:(2 * c + 2) * LANES]
            inv_q = _seg_inv(q2, bd128_ref, MLA_QK)
            inv_k = _seg_inv(k2 + jnp.concatenate([kr_own, kr_own], axis=1), bd128_ref, MLA_QK)
            for hh in range(2):
                hd = 2 * c + hh
                sl = slice(hh * LANES, (hh + 1) * LANES)
                qg = q2[:, sl] * gmq_ref[...]
                qr = (qg * tqc_ref[...] + pltpu.roll(qg, 16, 1) * tqs1_ref[...]
                      + pltpu.roll(qg, 112, 1) * tqs2_ref[...])
                qm_ref[hd] = (qr * inv_q[:, sl]).astype(_BF)
                km_ref[hd] = ((k2[:, sl] * gmk_ref[...] + kr_rot) * inv_k[:, sl]).astype(_BF)
                vt = vpre[:, hd * LANES:(hd + 1) * LANES].T
                vm_ref[hd, 0] = jnp.where(row_t == MLA_V, 1.0, vt).astype(_BF)

    def neighbourhood(z):
        q2 = z[:, 0:256]
        k2 = z[:, 256:512]
        qk2 = z[:, 768:1024]
        qn2 = q2 * _seg_inv(q2, bd64_ref, HEAD_DIM)
        kn2 = k2 * _seg_inv(k2, bd64_ref, HEAD_DIM)
        qkn2 = qk2 * _seg_inv(qk2, bd64_ref, HEAD_DIM)
        for p in range(2):
            sl = slice(p * LANES, (p + 1) * LANES)
            qn_ref[p] = (qn2[:, sl] * gnq_ref[...]).astype(_BF)
            kn_ref[p] = (kn2[:, sl] * gnk_ref[...]).astype(_BF)
            vn_ref[p] = z[:, 512 + p * LANES: 512 + (p + 1) * LANES].astype(_BF)
        qn_ref[2] = (qkn2[:, :LANES] * gnq_ref[...]).astype(_BF)
        kn_ref[2] = (qkn2[:, LANES:] * gnk_ref[...]).astype(_BF)
        vn_ref[2] = z[:, 1024:1152].astype(_BF)

    def put(out_ref, p, val, rate, slot):
        if rate == 1:
            out_ref[p] = val.astype(_BF)
        else:
            rows = tm // rate
            perm_ref[slot] = val
            for rho in range(rate):
                out_ref[p, rho * rows:(rho + 1) * rows, :] = (
                    perm_ref[slot, pl.ds(rho, rows, stride=rate), :].astype(_BF))

    def dilated(g, z):
        q2 = z[:, 0:256]
        k2 = z[:, 256:512]
        qn2 = q2 * _seg_inv(q2, bdr_ref, HEAD_DIM)
        kn2 = k2 * _seg_inv(k2, bdr_ref, HEAD_DIM)
        for pp in range(2):
            p = 2 * g + pp
            sl = slice(pp * LANES, (pp + 1) * LANES)
            qx = qn2[:, sl] * gdq_ref[...]
            qx = qx * tdc_ref[...] + pltpu.roll(qx, 64, 1) * tds_ref[...]
            kx = kn2[:, sl] * gdk_ref[...]
            kx = kx * tdc_ref[...] + pltpu.roll(kx, 64, 1) * tds_ref[...]
            put(qd_ref, p, qx, DIL_RATES[g], 3 * pp)
            put(kd_ref, p, kx, DIL_RATES[g], 3 * pp + 1)
            put(vd_ref, p, z[:, 512 + pp * LANES: 512 + (pp + 1) * LANES], DIL_RATES[g], 3 * pp + 2)

    dil_lo = [COL_DIL + g * 768 for g in range(DIL_GROUPS)]
    stages = [(dil_lo[2], dil_lo[2] + 768, functools.partial(dilated, 2)),
              (dil_lo[1], dil_lo[1] + 768, functools.partial(dilated, 1)),
              (dil_lo[0], dil_lo[0] + 768, functools.partial(dilated, 0)),
              (COL_MLA, COL_NA, latent),
              (COL_NA, COL_DIL, neighbourhood)]
    z_next = project(stages[0][0], stages[0][1])
    for idx, (_, _, finish) in enumerate(stages):
        z = z_next
        if idx + 1 < len(stages):
            z_next = project(stages[idx + 1][0], stages[idx + 1][1])
        finish(z)


def _const_spec(shape):
    nd = len(shape)
    return pl.BlockSpec(shape, lambda *_, _nd=nd: (0,) * _nd)


def _layer_spec(arr, layer):
    nd = arr.ndim
    return pl.BlockSpec((None,) + arr.shape[1:], lambda *_, _l=layer, _nd=nd: (_l,) + (0,) * (_nd - 1))


def _mix_in(x, lw, layer, tabs, seq):
    t = x.shape[0]
    nblk_seq = seq // TM
    tab_spec = pl.BlockSpec((TM, LANES), lambda i: (i % nblk_seq, 0))
    weights = [lw[n] for n in ('g_mix', 'w_in', 'g_cq', 'g_ckv', 'w_uq', 'w_uk', 'w_uv',
                               'g_mq', 'g_mk', 'g_kr', 'g_nq', 'g_nk', 'g_dq', 'g_dk')]
    joins = _head_join_matrices()
    tables = [tabs['q_c'], tabs['q_s1'], tabs['q_s2'], tabs['k_c'], tabs['k_s'], tabs['d_c'], tabs['d_s']]

    def out(n):
        return (jax.ShapeDtypeStruct((n, t, LANES), _BF), pl.BlockSpec((n, TM, LANES), lambda i: (0, i, 0)))

    vt_out = (jax.ShapeDtypeStruct((MLA_HEADS, t // TM, LANES, TM), _BF),
              pl.BlockSpec((MLA_HEADS, 1, LANES, TM), lambda i: (0, i, 0, 0)))
    outs = [out(6), out(6), vt_out, out(3), out(3), out(3), out(6), out(6), out(6)]
    return pl.pallas_call(
        _mix_in_kernel,
        grid=(t // TM,),
        in_specs=[pl.BlockSpec((TM, D_MODEL), lambda i: (i, 0))]
        + [_layer_spec(w, layer) for w in weights] + [_const_spec(m.shape) for m in joins]
        + [tab_spec] * len(tables),
        out_specs=[o[1] for o in outs],
        out_shape=[o[0] for o in outs],
        scratch_shapes=[pltpu.VMEM((6, TM, LANES), _F32)],
        compiler_params=pltpu.CompilerParams(dimension_semantics=("parallel",), vmem_limit_bytes=VMEM_LIMIT),
        name="mix_in",
    )(x, *weights, *joins, *tables)


def _mla_kernel(q_ref, k_ref, vt_ref, o_ref, sa_sc, sb_sc, m_sc, acc_sc, *, tk):
    tq = q_ref.shape[1]
    nk = k_ref.shape[1] // tk
    sub = tk // TM
    m_sc[...] = jnp.full(m_sc.shape, NEG, _F32)
    acc_sc[...] = jnp.zeros(acc_sc.shape, _F32)

    def scores(j, dst):
        rows = pl.ds(pl.multiple_of(j * tk, tk), tk)
        for hh in range(2):
            dst[hh] = _dot_t(k_ref[hh, rows, :], q_ref[hh])

    def consume(j, src):
        for hh in range(2):
            s = src[hh]
            m_prev = m_sc[hh]
            m_new = jnp.maximum(m_prev, jnp.max(s, axis=0, keepdims=True))
            p = jnp.exp2(s - m_new).astype(_BF)
            vt = jnp.concatenate([vt_ref[hh, j * sub + c] for c in range(sub)], axis=1)
            acc_sc[hh] = jnp.exp2(m_prev - m_new) * acc_sc[hh] + _dot(vt, p)
            m_sc[hh] = m_new

    def pair(i, carry):
        j = 2 * i
        scores(j + 1, sb_sc)
        consume(j, sa_sc)
        scores(j + 2, sa_sc)
        consume(j + 1, sb_sc)
        return carry

    scores(0, sa_sc)
    lax.fori_loop(0, nk // 2 - 1, pair, 0)
    scores(nk - 1, sb_sc)
    consume(nk - 2, sa_sc)
    consume(nk - 1, sb_sc)

    lane = _lane((tq, LANES))
    outs = []
    for hh in range(2):
        acc = acc_sc[hh]
        outs.append((acc / acc[MLA_V:MLA_V + 1, :]).T)
    o_ref[0] = jnp.where(lane < MLA_V, outs[0], pltpu.roll(outs[1], MLA_V, 1)).astype(o_ref.dtype)


def _mla(q, k, vt, batch, seq, tq=2048, tk=512):
    t = q.shape[1]
    nq = seq // tq
    nblk = seq // TM
    assert seq % (2 * tk) == 0 and tk % TM == 0
    return pl.pallas_call(
        functools.partial(_mla_kernel, tk=tk),
        grid=(batch, MLA_HEADS // 2, nq),
        in_specs=[pl.BlockSpec((2, tq, LANES), lambda b, p, i: (p, b * nq + i, 0)),
                  pl.BlockSpec((2, seq, LANES), lambda b, p, i: (p, b, 0)),
                  pl.BlockSpec((2, nblk, LANES, TM), lambda b, p, i: (p, b, 0, 0))],
        out_specs=pl.BlockSpec((1, tq, LANES), lambda b, p, i: (p, b * nq + i, 0)),
        out_shape=jax.ShapeDtypeStruct((MLA_HEADS // 2, t, LANES), _BF),
        scratch_shapes=[pltpu.VMEM((2, tk, tq), _F32), pltpu.VMEM((2, tk, tq), _F32),
                        pltpu.VMEM((2, 1, tq), _F32), pltpu.VMEM((2, LANES, tq), _F32)],
        compiler_params=pltpu.CompilerParams(
            dimension_semantics=("parallel", "parallel", "arbitrary"), vmem_limit_bytes=VMEM_LIMIT),
        name="mla_attention",
    )(q, k, vt)


def _na_kernel(q_ref, k_ref, v_ref, bias_ref, o_ref, *, rows):
    i = pl.program_id(2)
    lane = _lane((GRID_W, LANES))
    first = lane < HEAD_DIM
    nkeys = NA_WIN_R * GRID_W

    scores, windows = [], []
    for rr in range(NA_RB):
        r = i * NA_RB + rr
        rs = jnp.clip(r - NA_WIN_R // 2, 0, rows - NA_WIN_R)
        start = pl.multiple_of(rs * GRID_W, GRID_W)
        q = q_ref[0, rr * GRID_W:(rr + 1) * GRID_W, :]
        zero = jnp.zeros_like(q)
        q2 = jnp.concatenate([jnp.where(first, q, zero), jnp.where(first, zero, q)], axis=0)
        scores.append(_dot_t(q2, k_ref[0, pl.ds(start, nkeys), :]) + bias_ref[0, r - rs])
        windows.append(start)
    probs = []
    for s in scores:
        p = jnp.exp(s - jnp.max(s, axis=1, keepdims=True))
        probs.append((p.astype(_BF), jnp.sum(p, axis=1, keepdims=True)))
    for rr, ((p, l), start) in enumerate(zip(probs, windows)):
        o2 = _dot(p, v_ref[0, pl.ds(start, nkeys), :]) / l
        o = jnp.where(first, o2[:GRID_W], o2[GRID_W:])
        o_ref[0, rr * GRID_W:(rr + 1) * GRID_W, :] = o.astype(o_ref.dtype)


def _na(q, k, v, bias, layer, batch, seq):
    t = q.shape[1]
    rows = seq // GRID_W
    assert rows >= NA_WIN_R and rows % NA_RB == 0
    nrb = rows // NA_RB
    qrows = NA_RB * GRID_W
    return pl.pallas_call(
        functools.partial(_na_kernel, rows=rows),
        grid=(batch, NA_HEADS // 2, nrb),
        in_specs=[pl.BlockSpec((1, qrows, LANES), lambda b, p, i: (p, b * nrb + i, 0)),
                  pl.BlockSpec((1, seq, LANES), lambda b, p, i: (p, b, 0)),
                  pl.BlockSpec((1, seq, LANES), lambda b, p, i: (p, b, 0)),
                  pl.BlockSpec((None, 1, NA_WIN_R, 2 * GRID_W, NA_WIN_R * GRID_W),
                               lambda b, p, i: (layer, p, 0, 0, 0))],
        out_specs=pl.BlockSpec((1, qrows, LANES), lambda b, p, i: (p, b * nrb + i, 0)),
        out_shape=jax.ShapeDtypeStruct((NA_HEADS // 2, t, LANES), _BF),
        compiler_params=pltpu.CompilerParams(
            dimension_semantics=("parallel", "parallel", "arbitrary"), vmem_limit_bytes=VMEM_LIMIT),
        name="na_attention",
    )(q, k, v, bias)


def _dil_kernel(q0_ref, q1_ref, q2_ref, k0_ref, k1_ref, k2_ref, v0_ref, v1_ref, v2_ref, band_ref,
                o_ref, og_sc, lse_sc, *, seq):
    sb = pl.program_id(2)
    q_refs = (q0_ref, q1_ref, q2_ref)
    k_refs = (k0_ref, k1_ref, k2_ref)
    v_refs = (v0_ref, v1_ref, v2_ref)
    lane = _lane((DIL_TQ, LANES))
    first_qk = (lane % HEAD_DIM) < (HEAD_DIM // 2)
    first_v = lane < HEAD_DIM
    n_tiles = DIL_SB // DIL_TQ
    half = DIL_TQ // 2

    for g in range(DIL_GROUPS):
        rate = DIL_RATES[g]
        length = seq // rate
        per_blk = TM // rate
        chunk = min(half, per_blk)
        n_chunks = (2 * DIL_TQ) // chunk
        tiles_per_res = n_tiles // rate
        q_ref, k_ref, v_ref = q_refs[g], k_refs[g], v_refs[g]

        def row_of(m, rho, _per_blk=per_blk):
            return (m // _per_blk) * TM + rho * _per_blk + (m % _per_blk)

        def body(tg, carry, _g=g, _rate=rate, _length=length, _per_blk=per_blk, _chunk=chunk,
                 _n_chunks=n_chunks, _tpr=tiles_per_res, _q=q_ref, _k=k_ref, _v=v_ref):
            scores, metas = [], []
            for u in range(DIL_UNROLL):
                tt = tg * DIL_UNROLL + u
                rho = tt // _tpr
                mt = tt % _tpr
                m0 = sb * (DIL_SB // _rate) + mt * DIL_TQ
                q = jnp.concatenate(
                    [_q[0, pl.ds(pl.multiple_of(row_of(m0 + c * _chunk, rho), _chunk), _chunk), :]
                     for c in range(DIL_TQ // _chunk)], axis=0)
                zero = jnp.zeros_like(q)
                q2 = jnp.concatenate([jnp.where(first_qk, q, zero), jnp.where(first_qk, zero, q)], axis=0)
                win = []
                for c in range(_n_chunks):
                    ms = jnp.clip(m0 - half + c * _chunk, 0, _length - _chunk)
                    win.append(pl.multiple_of(row_of(ms, rho), _chunk))
                kw = jnp.concatenate([_k[0, pl.ds(row, _chunk), :] for row in win], axis=0)
                edge = (m0 == 0).astype(jnp.int32) + 2 * (m0 == _length - DIL_TQ).astype(jnp.int32)
                scores.append(_dot_t(q2, kw) + band_ref[edge])
                metas.append((rho + _rate * mt * DIL_TQ, win))
            probs = []
            for s in scores:
                m = jnp.max(s, axis=1, keepdims=True)
                p = jnp.exp(s - m)
                probs.append((p.astype(_BF), m, jnp.sum(p, axis=1, keepdims=True)))
            for (p, m, l), (base, win) in zip(probs, metas):
                vw = jnp.concatenate([_v[0, pl.ds(row, _chunk), :] for row in win], axis=0)
                o2 = _dot(p, vw) / l
                lse2 = jnp.broadcast_to(m + jnp.log(l), (2 * DIL_TQ, LANES))
                if _rate == 1:
                    dst = pl.ds(pl.multiple_of(base, DIL_TQ), DIL_TQ)
                else:
                    dst = pl.ds(base, DIL_TQ, stride=_rate)
                og_sc[_g, dst, :] = jnp.where(first_v, o2[:DIL_TQ], o2[DIL_TQ:])
                lse_sc[_g, dst, :] = jnp.where(first_v, lse2[:DIL_TQ], lse2[DIL_TQ:])
            return carry

        lax.fori_loop(0, n_tiles // DIL_UNROLL, body, 0)

    m = jnp.maximum(jnp.maximum(lse_sc[0], lse_sc[1]), lse_sc[2])
    w0 = jnp.exp(lse_sc[0] - m)
    w1 = jnp.exp(lse_sc[1] - m)
    w2 = jnp.exp(lse_sc[2] - m)
    o = (w0 * og_sc[0] + w1 * og_sc[1] + w2 * og_sc[2]) / (w0 + w1 + w2)
    o_ref[0] = o.astype(o_ref.dtype)


def _dil(q, k, v, band, batch, seq):
    t = q.shape[1]
    assert seq % DIL_SB == 0
    nsb = seq // DIL_SB

    def q_spec(g):
        return pl.BlockSpec((1, DIL_SB, LANES), lambda b, sp, s, _g=g: (2 * _g + sp, b * nsb + s, 0))

    def kv_spec(g):
        return pl.BlockSpec((1, seq, LANES), lambda b, sp, s, _g=g: (2 * _g + sp, b, 0))

    return pl.pallas_call(
        functools.partial(_dil_kernel, seq=seq),
        grid=(batch, 2, nsb),
        in_specs=[kv_spec(0), kv_spec(1), kv_spec(2), kv_spec(0), kv_spec(1), kv_spec(2),
                  kv_spec(0), kv_spec(1), kv_spec(2),
                  pl.BlockSpec((4, 2 * DIL_TQ, 2 * DIL_TQ), lambda b, sp, s: (0, 0, 0))],
        out_specs=pl.BlockSpec((1, DIL_SB, LANES), lambda b, sp, s: (sp, b * nsb + s, 0)),
        out_shape=jax.ShapeDtypeStruct((2, t, LANES), _BF),
        scratch_shapes=[pltpu.VMEM((DIL_GROUPS, DIL_SB, LANES), _F32),
                        pltpu.VMEM((DIL_GROUPS, DIL_SB, LANES), _F32)],
        compiler_params=pltpu.CompilerParams(
            dimension_semantics=("parallel", "parallel", "arbitrary"), vmem_limit_bytes=VMEM_LIMIT),
        name="dil_attention",
    )(q, q, q, k, k, k, v, v, v, band)


def _mem_kv_kernel(mem_ref, gmem_ref, wckv_ref, gxk_ref, k_ref, v_ref):
    hm = _rms(mem_ref[0], gmem_ref[...]).astype(_BF)
    kv = _dot(hm, wckv_ref[...])
    for hd in range(X_HEADS):
        kh = kv[:, hd * X_HEAD_DIM:(hd + 1) * X_HEAD_DIM]
        k_ref[0, :, hd * X_HEAD_DIM:(hd + 1) * X_HEAD_DIM] = _rms(kh, gxk_ref[...]).astype(_BF)
    v_ref[0] = kv[:, D_MODEL:].astype(_BF)


def _mem_kv(mem, lw, layer):
    b, m, _ = mem.shape
    return pl.pallas_call(
        _mem_kv_kernel,
        grid=(b,),
        in_specs=[pl.BlockSpec((1, m, D_MODEL), lambda i: (i, 0, 0))]
        + [_layer_spec(lw[n], layer) for n in ('g_mem', 'w_ckv', 'g_xk')],
        out_specs=[pl.BlockSpec((1, m, D_MODEL), lambda i: (i, 0, 0))] * 2,
        out_shape=[jax.ShapeDtypeStruct((b, m, D_MODEL), _BF)] * 2,
        compiler_params=pltpu.CompilerParams(dimension_semantics=("parallel",), vmem_limit_bytes=VMEM_LIMIT),
        name="mem_kv",
    )(mem, lw['g_mem'], lw['w_ckv'], lw['g_xk'])


def _mix_out_kernel(x_ref, oa_ref, ob_ref, oc_ref, wo_ref, gcross_ref, wcq_ref, gxq_ref,
                    mk_ref, mv_ref, wco_ref, y_ref):
    mix = jnp.concatenate([oa_ref[p] for p in range(3)] + [ob_ref[p] for p in range(3)]
                          + [oc_ref[p] for p in range(2)], axis=1)
    x1 = x_ref[...] + _dot(mix, wo_ref[...])
    hc = _rms(x1, gcross_ref[...]).astype(_BF)
    q = _dot(hc, wcq_ref[...])
    outs = []
    for hd in range(X_HEADS):
        sl = slice(hd * X_HEAD_DIM, (hd + 1) * X_HEAD_DIM)
        qh = (_rms(q[:, sl], gxq_ref[...]) * X_HEAD_DIM ** -0.5).astype(_BF)
        s = _dot_t(qh, mk_ref[0, :, sl])
        m = jnp.max(s, axis=1, keepdims=True)
        p = jnp.exp(s - m)
        l = jnp.sum(p, axis=1, keepdims=True)
        outs.append((_dot(p.astype(_BF), mv_ref[0, :, sl]) / l).astype(_BF))
    o = jnp.concatenate(outs, axis=1)
    y_ref[...] = x1 + _dot(o, wco_ref[...])


def _mix_out(x, oa, ob, oc, mk, mv, lw, layer, seq):
    t = x.shape[0]
    nblk_seq = seq // TM
    mem_len = mk.shape[1]

    def grp(n):
        return pl.BlockSpec((n, TM, LANES), lambda i: (0, i, 0))

    mem_spec = pl.BlockSpec((1, mem_len, D_MODEL), lambda i: (i // nblk_seq, 0, 0))
    return pl.pallas_call(
        _mix_out_kernel,
        grid=(t // TM,),
        in_specs=[pl.BlockSpec((TM, D_MODEL), lambda i: (i, 0)), grp(3), grp(3), grp(2),
                  _layer_spec(lw['w_o'], layer), _layer_spec(lw['g_cross'], layer),
                  _layer_spec(lw['w_cq'], layer), _layer_spec(lw['g_xq'], layer),
                  mem_spec, mem_spec, _layer_spec(lw['w_co'], layer)],
        out_specs=pl.BlockSpec((TM, D_MODEL), lambda i: (i, 0)),
        out_shape=jax.ShapeDtypeStruct((t, D_MODEL), _F32),
        compiler_params=pltpu.CompilerParams(dimension_semantics=("parallel",), vmem_limit_bytes=VMEM_LIMIT),
        name="mix_out",
    )(x, oa, ob, oc, lw['w_o'], lw['g_cross'], lw['w_cq'], lw['g_xq'], mk, mv, lw['w_co'])


def _ffn_kernel(x_ref, xp_ref, xn_ref, gffn_ref, wup_ref, cw_ref, cb_ref, wdn_ref, y_ref, *, nblk_seq):
    i = pl.program_id(0)
    tm = x_ref.shape[0]
    x = x_ref[...]
    g = gffn_ref[...]
    is_first = (i % nblk_seq) == 0
    is_last = (i % nblk_seq) == nblk_seq - 1
    hp = jnp.where(is_first, 0.0, _rms(xp_ref[...], g))
    hn = jnp.where(is_last, 0.0, _rms(xn_ref[...], g))
    h = jnp.concatenate([hp, _rms(x, g), hn], axis=0).astype(_BF)
    acc = x

    def up_proj(c):
        return [_dot(h, wup_ref[:, part * D_FF + c * FF_CH: part * D_FF + (c + 1) * FF_CH]) for part in range(2)]

    def conv(u, lo):
        cw = cw_ref[:, lo:lo + FF_CH]
        up = pltpu.roll(u, 1, 0)[8:8 + tm]
        un = pltpu.roll(u, tm + 15, 0)[8:8 + tm]
        return up * cw[0:1] + u[8:8 + tm] * cw[1:2] + un * cw[2:3] + cb_ref[:, lo:lo + FF_CH]

    pending = []
    u_next = up_proj(0)
    for c in range(N_FF_CH):
        u_val, u_gate = u_next
        if c + 1 < N_FF_CH:
            u_next = up_proj(c + 1)
        a = conv(u_val, c * FF_CH)
        half_gate = conv(u_gate, D_FF + c * FF_CH)
        pending.append((half_gate * (1.0 + jnp.tanh(half_gate)) * a).astype(_BF))
        rest = N_FF_CH - 1 - c
        if len(pending) == 2 and rest != 1 or rest == 0:
            lo = (c + 1 - len(pending)) * FF_CH
            act = pending[0] if len(pending) == 1 else jnp.concatenate(pending, axis=1)
            acc = acc + _dot(act, wdn_ref[lo:(c + 1) * FF_CH, :])
            pending = []
    y_ref[...] = acc


def _ffn(x, lw, layer, seq):
    t = x.shape[0]
    nblk_seq = seq // TM
    nhalo = t // 8
    per = TM // 8
    return pl.pallas_call(
        functools.partial(_ffn_kernel, nblk_seq=nblk_seq),
        grid=(t // TM,),
        in_specs=[pl.BlockSpec((TM, D_MODEL), lambda i: (i, 0)),
                  pl.BlockSpec((8, D_MODEL), lambda i: (jnp.maximum(i * per - 1, 0), 0)),
                  pl.BlockSpec((8, D_MODEL), lambda i: (jnp.minimum((i + 1) * per, nhalo - 1), 0))]
        + [_layer_spec(lw[n], layer) for n in ('g_ffn', 'w_up', 'conv_w', 'conv_b', 'w_down')],
        out_specs=pl.BlockSpec((TM, D_MODEL), lambda i: (i, 0)),
        out_shape=jax.ShapeDtypeStruct((t, D_MODEL), _F32),
        compiler_params=pltpu.CompilerParams(dimension_semantics=("parallel",), vmem_limit_bytes=VMEM_LIMIT),
        name="conv_ffn",
    )(x, x, x, lw['g_ffn'], lw['w_up'], lw['conv_w'], lw['conv_b'], lw['w_down'])


def _relayout_w_in(w):
    nl = w.shape[0]
    lead = w[..., :SPLIT_CKV]
    a = w[..., SPLIT_CKV:SPLIT_CKV + 16]
    b = w[..., SPLIT_CKV + 16:SPLIT_KR]
    z32 = jnp.zeros(a.shape[:-1] + (32,), w.dtype)
    kr = jnp.concatenate([b, a, z32, a, b, z32], axis=-1)
    na = w[..., SPLIT_KR:SPLIT_NA].reshape(nl, D_MODEL, 3, NA_HEADS // 2, LANES)
    na_ab = na[:, :, :, 0:2].reshape(nl, D_MODEL, 3 * 2 * LANES)
    na_c = na[:, :, :, 2].reshape(nl, D_MODEL, 3 * LANES)
    dl = w[..., SPLIT_NA:].reshape(nl, D_MODEL, 3, DIL_GROUPS, 2, 2, 2, HEAD_DIM // 2)
    qk = dl[:, :, 0:2].transpose(0, 1, 3, 2, 4, 6, 5, 7).reshape(nl, D_MODEL, DIL_GROUPS, 4 * LANES)
    v = dl[:, :, 2].reshape(nl, D_MODEL, DIL_GROUPS, 2 * LANES)
    dil = jnp.concatenate([qk, v], axis=-1).reshape(nl, D_MODEL, DIL_GROUPS * 6 * LANES)
    out = jnp.concatenate([lead, kr, na_ab, na_c, dil], axis=-1)
    assert out.shape[-1] == D_IN_P
    return out


def _pad_heads(w, real, n_heads):
    nl, k, _ = w.shape
    w = w.reshape(nl, k, n_heads, real)
    return jnp.pad(w, ((0, 0), (0, 0), (0, 0), (0, LANES - real))).reshape(nl, k, n_heads * LANES)


def _rows(v):
    return v[:, None, :].astype(_F32)


def _prep_layers(p):
    nl = p['w_in'].shape[0]
    kn, qn = p['mla_kn'], p['mla_qn']
    z32 = jnp.zeros((nl, 32), _F32)
    z64 = jnp.zeros((nl, 64), _F32)
    dq, dk = p['dil_qn'], p['dil_kn']

    def rot(gain):
        return jnp.concatenate([gain[:, :32], gain[:, :32], gain[:, 32:], gain[:, 32:]], axis=-1)

    gate_half = jnp.concatenate([jnp.ones((D_FF,), _F32), jnp.full((D_FF,), 0.5, _F32)])

    return {
        'g_mix': _rows(p['norm_mix']), 'w_in': _relayout_w_in(p['w_in'].astype(_BF)),
        'g_cq': _rows(p['mla_q_norm']), 'g_ckv': _rows(p['mla_kv_norm']),
        'w_uq': _pad_heads(p['w_uq'].astype(_BF), MLA_QK, MLA_HEADS),
        'w_uk': _pad_heads(p['w_uk'].astype(_BF), MLA_NOPE, MLA_HEADS),
        'w_uv': _pad_heads(p['w_uv'].astype(_BF), MLA_V, MLA_HEADS),
        'g_mq': _rows(jnp.concatenate([qn, z32], axis=-1) * (MLA_QK ** -0.5 * LOG2E)),
        'g_mk': _rows(jnp.concatenate([kn[:, :MLA_NOPE], z64], axis=-1)),
        'g_kr': _rows(jnp.concatenate([kn[:, 80:96], kn[:, 64:80], z32, kn[:, 64:80], kn[:, 80:96], z32], axis=-1)),
        'g_nq': _rows(jnp.tile(p['na_qn'], (1, 2)) * HEAD_DIM ** -0.5), 'g_nk': _rows(jnp.tile(p['na_kn'], (1, 2))),
        'g_dq': _rows(rot(dq) * HEAD_DIM ** -0.5), 'g_dk': _rows(rot(dk)),
        'w_o': p['w_o'].astype(_BF),
        'g_cross': _rows(p['norm_cross']), 'w_cq': p['w_cq'].astype(_BF),
        'g_xq': _rows(p['x_qn']), 'g_xk': _rows(p['x_kn']), 'g_mem': _rows(p['norm_mem']),
        'w_ckv': p['w_ckv'].astype(_BF), 'w_co': p['w_co'].astype(_BF),
        'g_ffn': _rows(p['norm_ffn']), 'w_up': p['w_up'].astype(_BF),
        'conv_w': p['conv_w'] * gate_half, 'conv_b': _rows(p['conv_b'] * gate_half),
        'w_down': p['w_down'].astype(_BF),
        'na_bias': _na_bias(p['na_rpb']),
    }


def _na_bias(rpb):
    nl = rpb.shape[0]
    dl = np.arange(NA_WIN_R)
    j = np.arange(NA_WIN_R)
    r_off = j[None, :] - dl[:, None] + (NA_WIN_R - 1)
    qc = np.arange(GRID_W)
    kc = np.arange(GRID_W)
    c_start = np.clip(qc - NA_WIN_C // 2, 0, GRID_W - NA_WIN_C)
    valid = (kc[None, :] >= c_start[:, None]) & (kc[None, :] < c_start[:, None] + NA_WIN_C)
    c_off = kc[None, :] - qc[:, None] + (NA_WIN_C - 1)
    pick = (np.arange(2 * NA_WIN_C - 1)[:, None, None] == c_off[None]) & valid[None]
    rows = rpb[:, :, r_off].reshape(nl, NA_HEADS // 2, 2, NA_WIN_R, NA_WIN_R, 2 * NA_WIN_C - 1)
    b = jnp.einsum('lpedjc,cqk->lpdeqjk', rows, jnp.asarray(pick, _F32), precision=lax.Precision.HIGHEST)
    b = b + jnp.asarray(np.where(valid, 0.0, NEG), _F32)[:, None, :]
    return b.reshape(nl, NA_HEADS // 2, NA_WIN_R, 2 * GRID_W, NA_WIN_R * GRID_W)


def _band_bias():
    i = np.arange(DIL_TQ)[:, None]
    c = np.arange(2 * DIL_TQ)[None, :]
    band = (c >= i) & (c <= i + DIL_TQ)
    out = []
    for edge in range(4):
        ok = band
        if edge & 1:
            ok = ok & (c >= DIL_TQ // 2)
        if edge & 2:
            ok = ok & (c < 2 * DIL_TQ - DIL_TQ // 2)
        out.append(np.tile(np.where(ok, 0.0, NEG), (2, 1)))
    return jnp.asarray(np.stack(out), _F32)


def _rope_tables(seq):
    pos = jnp.arange(seq, dtype=_F32)[:, None]

    def cs(half):
        inv = ROPE_THETA ** (-jnp.arange(half, dtype=_F32) / half)
        ang = pos * inv[None, :]
        return jnp.cos(ang), jnp.sin(ang)

    c16, s16 = cs(MLA_ROPE // 2)
    c32, s32 = cs(HEAD_DIM // 2)
    one = jnp.ones((seq, 64), _F32)
    z16 = jnp.zeros((seq, 16), _F32)
    z32 = jnp.zeros((seq, 32), _F32)
    z64 = jnp.zeros((seq, 64), _F32)
    return {
        'q_c': jnp.concatenate([one, c16, c16, z32], axis=1),
        'q_s1': jnp.concatenate([z64, z16, s16, z32], axis=1),
        'q_s2': jnp.concatenate([z64, -s16, z16, z32], axis=1),
        'k_c': jnp.concatenate([z64, c16, c16, z32], axis=1),
        'k_s': jnp.concatenate([z64, -s16, s16, z32], axis=1),
        'd_c': jnp.concatenate([c32, c32, c32, c32], axis=1),
        'd_s': jnp.concatenate([-s32, -s32, s32, s32], axis=1),
    }


def _trunk_layer(x, mem, lw, layer, tabs, band, batch, seq):
    qm, km, vm, qn, kn, vn, qd, kd, vd = _mix_in(x, lw, layer, tabs, seq)
    oa = _mla(qm, km, vm, batch, seq)
    ob = _na(qn, kn, vn, lw['na_bias'], layer, batch, seq)
    oc = _dil(qd, kd, vd, band, batch, seq)
    mk, mv = _mem_kv(mem, lw, layer)
    x = _mix_out(x, oa, ob, oc, mk, mv, lw, layer, seq)
    return _ffn(x, lw, layer, seq)


def kernel(x_prompt, x_sample, mem_prompt, mem_sample, norm_mix, w_in, mla_q_norm, mla_kv_norm, w_uq, w_uk, w_uv, mla_qn, mla_kn, na_qn, na_kn, na_rpb, dil_qn, dil_kn, w_o, norm_cross, norm_mem, w_cq, w_ckv, x_qn, x_kn, w_co, norm_ffn, w_up, conv_w, conv_b, w_down):
    stacked = {
        'norm_mix': norm_mix, 'w_in': w_in, 'mla_q_norm': mla_q_norm, 'mla_kv_norm': mla_kv_norm,
        'w_uq': w_uq, 'w_uk': w_uk, 'w_uv': w_uv, 'mla_qn': mla_qn, 'mla_kn': mla_kn,
        'na_qn': na_qn, 'na_kn': na_kn, 'na_rpb': na_rpb, 'dil_qn': dil_qn, 'dil_kn': dil_kn,
        'w_o': w_o, 'norm_cross': norm_cross, 'norm_mem': norm_mem, 'w_cq': w_cq, 'w_ckv': w_ckv,
        'x_qn': x_qn, 'x_kn': x_kn, 'w_co': w_co, 'norm_ffn': norm_ffn, 'w_up': w_up,
        'conv_w': conv_w, 'conv_b': conv_b, 'w_down': w_down,
    }
    lw = _prep_layers(stacked)
    band = _band_bias()
    outs = []
    for x, mem in ((x_prompt, mem_prompt), (x_sample, mem_sample)):
        batch, seq, _ = x.shape
        tabs = _rope_tables(seq)
        xt = x.reshape(batch * seq, D_MODEL)
        for layer in range(DEPTH):
            xt = _trunk_layer(xt, mem, lw, layer, tabs, band, batch, seq)
        outs.append(xt.reshape(batch, seq, D_MODEL))
    return tuple(outs)
```

```python
import functools

import jax
import jax.numpy as jnp
import numpy as np
from jax import lax
from jax.experimental import pallas as pl
from jax.experimental.pallas import tpu as pltpu

D_MODEL = 1024
DEPTH = 4
GRID_W = 64
HEAD_DIM = 64
ROPE_THETA = 10000.0
EPS = 1e-6
MLA_HEADS = 6
MLA_Q_RANK = 256
MLA_KV_RANK = 128
MLA_NOPE = 64
MLA_ROPE = 32
MLA_V = 64
MLA_QK = MLA_NOPE + MLA_ROPE
NA_HEADS = 6
NA_WIN_R = 8
NA_WIN_C = 16
DIL_WINDOWS = (128, 512, 2048)
DIL_RATES = (1, 4, 16)
DIL_GROUPS = 3
DIL_HEADS = 4
DIL_ALL = DIL_GROUPS * DIL_HEADS
X_HEADS = 4
X_HEAD_DIM = D_MODEL // X_HEADS
D_FF = 2816
CONV_W = 3
SPLIT_CQ = MLA_Q_RANK
SPLIT_CKV = SPLIT_CQ + MLA_KV_RANK
SPLIT_KR = SPLIT_CKV + MLA_ROPE
SPLIT_NA = SPLIT_KR + 3 * NA_HEADS * HEAD_DIM
D_IN = SPLIT_NA + 3 * DIL_ALL * HEAD_DIM

LANES = 128
NEG = -1e30
LOG2E = 1.4426950408889634
TM = 512
DIL_TQ = 128
DIL_SB = DIL_TQ * max(DIL_RATES)
DIL_UNROLL = 16
NA_RB = 32
FF_CH = 256
N_FF_CH = D_FF // FF_CH
VMEM_LIMIT = 56 * 1024 * 1024

COL_MLA = 0
COL_NA = 512
COL_DIL = COL_NA + 3 * NA_HEADS * HEAD_DIM
D_IN_P = COL_DIL + 3 * DIL_ALL * HEAD_DIM

_BF = jnp.bfloat16
_F32 = jnp.float32


def _dot(a, b):
    return jnp.dot(a, b, preferred_element_type=_F32)


def _dot_t(a, b):
    return lax.dot_general(a, b, (((1,), (1,)), ((), ())), preferred_element_type=_F32)


def _rms(x, g):
    ms = jnp.mean(x * x, axis=-1, keepdims=True)
    return x * lax.rsqrt(ms + EPS) * g


def _lane(shape):
    return lax.broadcasted_iota(jnp.int32, shape, len(shape) - 1)


def _head_join_matrices():
    i = np.arange(2 * LANES)
    same_group = (i[:, None] // LANES) == (i[None, :] // LANES)
    natural = (i[:, None] // HEAD_DIM) == (i[None, :] // HEAD_DIM)
    half = ((i % HEAD_DIM) < HEAD_DIM // 2)
    rotary = same_group & (half[:, None] == half[None, :])
    return [jnp.asarray(m, _BF) for m in (same_group, natural, rotary)]


def _seg_inv(x2, bd_ref, dim):
    ss = _dot((x2 * x2).astype(_BF), bd_ref[...])
    return lax.rsqrt(ss * (1.0 / dim) + EPS)


def _mix_in_kernel(x_ref, gmix_ref, win_ref, gcq_ref, gckv_ref, wuq_ref, wuk_ref, wuv_ref,
                   gmq_ref, gmk_ref, gkr_ref, gnq_ref, gnk_ref, gdq_ref, gdk_ref,
                   bd128_ref, bd64_ref, bdr_ref,
                   tqc_ref, tqs1_ref, tqs2_ref, tkc_ref, tks_ref, tdc_ref, tds_ref,
                   qm_ref, km_ref, vm_ref, qn_ref, kn_ref, vn_ref, qd_ref, kd_ref, vd_ref,
                   perm_ref):
    tm = x_ref.shape[0]
    x = x_ref[...]
    xg = (x * gmix_ref[...]).astype(_BF)
    inv_x = lax.rsqrt(jnp.mean(x * x, axis=-1, keepdims=True) + EPS)
    lane = _lane((tm, LANES))

    def project(lo, hi):
        return _dot(xg, win_ref[:, lo:hi]) * inv_x

    def latent(z):
        cq = _rms(z[:, 0:MLA_Q_RANK], gcq_ref[...]).astype(_BF)
        ckv = _rms(z[:, MLA_Q_RANK:MLA_Q_RANK + MLA_KV_RANK], gckv_ref[...]).astype(_BF)
        kr = z[:, 384:512]
        kr_own = jnp.where((lane >= MLA_NOPE) & (lane < MLA_QK), kr, 0.0)
        krg = kr * gkr_ref[...]
        kr_rot = krg * tkc_ref[...] + pltpu.roll(krg, 64, 1) * tks_ref[...]
        qpre = _dot(cq, wuq_ref[...])
        kpre = _dot(ckv, wuk_ref[...])
        vpre = _dot(ckv, wuv_ref[...])
        row_t = lax.broadcasted_iota(jnp.int32, (LANES, tm), 0)
        for c in range(MLA_HEADS // 2):
            q2 = qpre[:, 2 * c * LANES:(2 * c + 2) * LANES]
            k2 = kpre[:, 2 * c * LANES:(2 * c + 2) * LANES]
            inv_q = _seg_inv(q2, bd128_ref, MLA_QK)
            inv_k = _seg_inv(k2 + jnp.concatenate([kr_own, kr_own], axis=1), bd128_ref, MLA_QK)
            for hh in range(2):
                hd = 2 * c + hh
                sl = slice(hh * LANES, (hh + 1) * LANES)
                qg = q2[:, sl] * gmq_ref[...]
                qr = (qg * tqc_ref[...] + pltpu.roll(qg, 16, 1) * tqs1_ref[...]
                      + pltpu.roll(qg, 112, 1) * tqs2_ref[...])
                qm_ref[hd] = (qr * inv_q[:, sl]).astype(_BF)
                km_ref[hd] = ((k2[:, sl] * gmk_ref[...] + kr_rot) * inv_k[:, sl]).astype(_BF)
                vt = vpre[:, hd * LANES:(hd + 1) * LANES].T
                vm_ref[hd, 0] = jnp.where(row_t == MLA_V, 1.0, vt).astype(_BF)

    def neighbourhood(z):
        q2 = z[:, 0:256]
        k2 = z[:, 256:512]
        qk2 = z[:, 768:1024]
        qn2 = q2 * _seg_inv(q2, bd64_ref, HEAD_DIM)
        kn2 = k2 * _seg_inv(k2, bd64_ref, HEAD_DIM)
        qkn2 = qk2 * _seg_inv(qk2, bd64_ref, HEAD_DIM)
        for p in range(2):
            sl = slice(p * LANES, (p + 1) * LANES)
            qn_ref[p] = (qn2[:, sl] * gnq_ref[...]).astype(_BF)
            kn_ref[p] = (kn2[:, sl] * gnk_ref[...]).astype(_BF)
            vn_ref[p] = z[:, 512 + p * LANES: 512 + (p + 1) * LANES].astype(_BF)
        qn_ref[2] = (qkn2[:, :LANES] * gnq_ref[...]).astype(_BF)
        kn_ref[2] = (qkn2[:, LANES:] * gnk_ref[...]).astype(_BF)
        vn_ref[2] = z[:, 1024:1152].astype(_BF)

    def put(out_ref, p, val, rate, slot):
        if rate == 1:
            out_ref[p] = val.astype(_BF)
        else:
            rows = tm // rate
            perm_ref[slot] = val
            for rho in range(rate):
                out_ref[p, rho * rows:(rho + 1) * rows, :] = (
                    perm_ref[slot, pl.ds(rho, rows, stride=rate), :].astype(_BF))

    def dilated(g, z):
        q2 = z[:, 0:256]
        k2 = z[:, 256:512]
        qn2 = q2 * _seg_inv(q2, bdr_ref, HEAD_DIM)
        kn2 = k2 * _seg_inv(k2, bdr_ref, HEAD_DIM)
        for pp in range(2):
            p = 2 * g + pp
            sl = slice(pp * LANES, (pp + 1) * LANES)
            qx = qn2[:, sl] * gdq_ref[...]
            qx = qx * tdc_ref[...] + pltpu.roll(qx, 64, 1) * tds_ref[...]
            kx = kn2[:, sl] * gdk_ref[...]
            kx = kx * tdc_ref[...] + pltpu.roll(kx, 64, 1) * tds_ref[...]
            put(qd_ref, p, qx, DIL_RATES[g], 3 * pp)
            put(kd_ref, p, kx, DIL_RATES[g], 3 * pp + 1)
            put(vd_ref, p, z[:, 512 + pp * LANES: 512 + (pp + 1) * LANES], DIL_RATES[g], 3 * pp + 2)

    dil_lo = [COL_DIL + g * 768 for g in range(DIL_GROUPS)]
    stages = [(dil_lo[2], dil_lo[2] + 768, functools.partial(dilated, 2)),
              (dil_lo[1], dil_lo[1] + 768, functools.partial(dilated, 1)),
              (dil_lo[0], dil_lo[0] + 768, functools.partial(dilated, 0)),
              (COL_MLA, COL_NA, latent),
              (COL_NA, COL_DIL, neighbourhood)]
    z_next = project(stages[0][0], stages[0][1])
    for idx, (_, _, finish) in enumerate(stages):
        z = z_next
        if idx + 1 < len(stages):
            z_next = project(stages[idx + 1][0], stages[idx + 1][1])
        finish(z)


def _const_spec(shape):
    nd = len(shape)
    return pl.BlockSpec(shape, lambda *_, _nd=nd: (0,) * _nd)


def _layer_spec(arr, layer):
    nd = arr.ndim
    return pl.BlockSpec((None,) + arr.shape[1:], lambda *_, _l=layer, _nd=nd: (_l,) + (0,) * (_nd - 1))


def _mix_in(x, lw, layer, tabs, seq):
    t = x.shape[0]
    nblk_seq = seq // TM
    tab_spec = pl.BlockSpec((TM, LANES), lambda i: (i % nblk_seq, 0))
    weights = [lw[n] for n in ('g_mix', 'w_in', 'g_cq', 'g_ckv', 'w_uq', 'w_uk', 'w_uv',
                               'g_mq', 'g_mk', 'g_kr', 'g_nq', 'g_nk', 'g_dq', 'g_dk')]
    joins = _head_join_matrices()
    tables = [tabs['q_c'], tabs['q_s1'], tabs['q_s2'], tabs['k_c'], tabs['k_s'], tabs['d_c'], tabs['d_s']]

    def out(n):
        return (jax.ShapeDtypeStruct((n, t, LANES), _BF), pl.BlockSpec((n, TM, LANES), lambda i: (0, i, 0)))

    vt_out = (jax.ShapeDtypeStruct((MLA_HEADS, t // TM, LANES, TM), _BF),
              pl.BlockSpec((MLA_HEADS, 1, LANES, TM), lambda i: (0, i, 0, 0)))
    outs = [out(6), out(6), vt_out, out(3), out(3), out(3), out(6), out(6), out(6)]
    return pl.pallas_call(
        _mix_in_kernel,
        grid=(t // TM,),
        in_specs=[pl.BlockSpec((TM, D_MODEL), lambda i: (i, 0))]
        + [_layer_spec(w, layer) for w in weights] + [_const_spec(m.shape) for m in joins]
        + [tab_spec] * len(tables),
        out_specs=[o[1] for o in outs],
        out_shape=[o[0] for o in outs],
        scratch_shapes=[pltpu.VMEM((6, TM, LANES), _F32)],
        compiler_params=pltpu.CompilerParams(dimension_semantics=("parallel",), vmem_limit_bytes=VMEM_LIMIT),
        name="mix_in",
    )(x, *weights, *joins, *tables)


def _mla_kernel(q_ref, k_ref, vt_ref, o_ref, sa_sc, sb_sc, m_sc, acc_sc, *, tk):
    tq = q_ref.shape[1]
    nk = k_ref.shape[1] // tk
    sub = tk // TM
    m_sc[...] = jnp.full(m_sc.shape, NEG, _F32)
    acc_sc[...] = jnp.zeros(acc_sc.shape, _F32)

    def scores(j, dst):
        rows = pl.ds(pl.multiple_of(j * tk, tk), tk)
        for hh in range(2):
            dst[hh] = _dot_t(k_ref[hh, rows, :], q_ref[hh])

    def consume(j, src):
        for hh in range(2):
            s = src[hh]
            m_prev = m_sc[hh]
            m_new = jnp.maximum(m_prev, jnp.max(s, axis=0, keepdims=True))
            p = jnp.exp2(s - m_new).astype(_BF)
            vt = jnp.concatenate([vt_ref[hh, j * sub + c] for c in range(sub)], axis=1)
            acc_sc[hh] = jnp.exp2(m_prev - m_new) * acc_sc[hh] + _dot(vt, p)
            m_sc[hh] = m_new

    def pair(i, carry):
        j = 2 * i
        scores(j + 1, sb_sc)
        consume(j, sa_sc)
        scores(j + 2, sa_sc)
        consume(j + 1, sb_sc)
        return carry

    scores(0, sa_sc)
    lax.fori_loop(0, nk // 2 - 1, pair, 0)
    scores(nk - 1, sb_sc)
    consume(nk - 2, sa_sc)
    consume(nk - 1, sb_sc)

    lane = _lane((tq, LANES))
    outs = []
    for hh in range(2):
        acc = acc_sc[hh]
        outs.append((acc / acc[MLA_V:MLA_V + 1, :]).T)
    o_ref[0] = jnp.where(lane < MLA_V, outs[0], pltpu.roll(outs[1], MLA_V, 1)).astype(o_ref.dtype)


def _mla(q, k, vt, batch, seq, tq=2048, tk=512):
    t = q.shape[1]
    nq = seq // tq
    nblk = seq // TM
    assert seq % (2 * tk) == 0 and tk % TM == 0
    return pl.pallas_call(
        functools.partial(_mla_kernel, tk=tk),
        grid=(batch, MLA_HEADS // 2, nq),
        in_specs=[pl.BlockSpec((2, tq, LANES), lambda b, p, i: (p, b * nq + i, 0)),
                  pl.BlockSpec((2, seq, LANES), lambda b, p, i: (p, b, 0)),
                  pl.BlockSpec((2, nblk, LANES, TM), lambda b, p, i: (p, b, 0, 0))],
        out_specs=pl.BlockSpec((1, tq, LANES), lambda b, p, i: (p, b * nq + i, 0)),
        out_shape=jax.ShapeDtypeStruct((MLA_HEADS // 2, t, LANES), _BF),
        scratch_shapes=[pltpu.VMEM((2, tk, tq), _F32), pltpu.VMEM((2, tk, tq), _F32),
                        pltpu.VMEM((2, 1, tq), _F32), pltpu.VMEM((2, LANES, tq), _F32)],
        compiler_params=pltpu.CompilerParams(
            dimension_semantics=("parallel", "parallel", "arbitrary"), vmem_limit_bytes=VMEM_LIMIT),
        name="mla_attention",
    )(q, k, vt)


def _na_kernel(q_ref, k_ref, v_ref, bias_ref, o_ref, *, rows):
    i = pl.program_id(2)
    lane = _lane((GRID_W, LANES))
    first = lane < HEAD_DIM
    nkeys = NA_WIN_R * GRID_W

    def score(rr):
        r = i * NA_RB + rr
        rs = jnp.clip(r - NA_WIN_R // 2, 0, rows - NA_WIN_R)
        start = pl.multiple_of(rs * GRID_W, GRID_W)
        q = q_ref[0, rr * GRID_W:(rr + 1) * GRID_W, :]
        zero = jnp.zeros_like(q)
        q2 = jnp.concatenate([jnp.where(first, q, zero), jnp.where(first, zero, q)], axis=0)
        return _dot_t(q2, k_ref[0, pl.ds(start, nkeys), :]) + bias_ref[0, r - rs], start

    def softmax(s):
        p = jnp.exp(s - jnp.max(s, axis=1, keepdims=True))
        return p.astype(_BF), jnp.sum(p, axis=1, keepdims=True)

    def value(rr, p, l, start):
        o2 = _dot(p, v_ref[0, pl.ds(start, nkeys), :]) / l
        o = jnp.where(first, o2[:GRID_W], o2[GRID_W:])
        o_ref[0, rr * GRID_W:(rr + 1) * GRID_W, :] = o.astype(o_ref.dtype)

    scored, normed = {}, {}
    for step in range(NA_RB + 2):
        if step < NA_RB:
            scored[step] = score(step)
        if 1 <= step <= NA_RB:
            s, start = scored.pop(step - 1)
            normed[step - 1] = softmax(s) + (start,)
        if step >= 2:
            value(step - 2, *normed.pop(step - 2))


def _na(q, k, v, bias, layer, batch, seq):
    t = q.shape[1]
    rows = seq // GRID_W
    assert rows >= NA_WIN_R and rows % NA_RB == 0
    nrb = rows // NA_RB
    qrows = NA_RB * GRID_W
    return pl.pallas_call(
        functools.partial(_na_kernel, rows=rows),
        grid=(batch, NA_HEADS // 2, nrb),
        in_specs=[pl.BlockSpec((1, qrows, LANES), lambda b, p, i: (p, b * nrb + i, 0)),
                  pl.BlockSpec((1, seq, LANES), lambda b, p, i: (p, b, 0)),
                  pl.BlockSpec((1, seq, LANES), lambda b, p, i: (p, b, 0)),
                  pl.BlockSpec((None, 1, NA_WIN_R, 2 * GRID_W, NA_WIN_R * GRID_W),
                               lambda b, p, i: (layer, p, 0, 0, 0))],
        out_specs=pl.BlockSpec((1, qrows, LANES), lambda b, p, i: (p, b * nrb + i, 0)),
        out_shape=jax.ShapeDtypeStruct((NA_HEADS // 2, t, LANES), _BF),
        compiler_params=pltpu.CompilerParams(
            dimension_semantics=("parallel", "parallel", "arbitrary"), vmem_limit_bytes=VMEM_LIMIT),
        name="na_attention",
    )(q, k, v, bias)


def _dil_kernel(q0_ref, q1_ref, q2_ref, k0_ref, k1_ref, k2_ref, v0_ref, v1_ref, v2_ref, band_ref,
                o_ref, og_sc, lse_sc, *, seq):
    sb = pl.program_id(2)
    q_refs = (q0_ref, q1_ref, q2_ref)
    k_refs = (k0_ref, k1_ref, k2_ref)
    v_refs = (v0_ref, v1_ref, v2_ref)
    lane = _lane((DIL_TQ, LANES))
    first_qk = (lane % HEAD_DIM) < (HEAD_DIM // 2)
    first_v = lane < HEAD_DIM
    n_tiles = DIL_SB // DIL_TQ
    half = DIL_TQ // 2

    for g in range(DIL_GROUPS):
        rate = DIL_RATES[g]
        length = seq // rate
        per_blk = TM // rate
        chunk = min(half, per_blk)
        n_chunks = (2 * DIL_TQ) // chunk
        tiles_per_res = n_tiles // rate
        q_ref, k_ref, v_ref = q_refs[g], k_refs[g], v_refs[g]

        def row_of(m, rho, _per_blk=per_blk):
            return (m // _per_blk) * TM + rho * _per_blk + (m % _per_blk)

        def body(tg, carry, _g=g, _rate=rate, _length=length, _per_blk=per_blk, _chunk=chunk,
                 _n_chunks=n_chunks, _tpr=tiles_per_res, _q=q_ref, _k=k_ref, _v=v_ref):
            def score(u):
                tt = tg * DIL_UNROLL + u
                rho = tt // _tpr
                mt = tt % _tpr
                m0 = sb * (DIL_SB // _rate) + mt * DIL_TQ
                q = jnp.concatenate(
                    [_q[0, pl.ds(pl.multiple_of(row_of(m0 + c * _chunk, rho), _chunk), _chunk), :]
                     for c in range(DIL_TQ // _chunk)], axis=0)
                zero = jnp.zeros_like(q)
                q2 = jnp.concatenate([jnp.where(first_qk, q, zero), jnp.where(first_qk, zero, q)], axis=0)
                win = []
                for c in range(_n_chunks):
                    ms = jnp.clip(m0 - half + c * _chunk, 0, _length - _chunk)
                    win.append(pl.multiple_of(row_of(ms, rho), _chunk))
                kw = jnp.concatenate([_k[0, pl.ds(row, _chunk), :] for row in win], axis=0)
                edge = (m0 == 0).astype(jnp.int32) + 2 * (m0 == _length - DIL_TQ).astype(jnp.int32)
                return _dot_t(q2, kw) + band_ref[edge], rho + _rate * mt * DIL_TQ, win

            def softmax(s):
                m = jnp.max(s, axis=1, keepdims=True)
                p = jnp.exp(s - m)
                return p.astype(_BF), m, jnp.sum(p, axis=1, keepdims=True)

            def value(p, m, l, base, win):
                vw = jnp.concatenate([_v[0, pl.ds(row, _chunk), :] for row in win], axis=0)
                o2 = _dot(p, vw) / l
                lse2 = jnp.broadcast_to(m + jnp.log(l), (2 * DIL_TQ, LANES))
                if _rate == 1:
                    dst = pl.ds(pl.multiple_of(base, DIL_TQ), DIL_TQ)
                else:
                    dst = pl.ds(base, DIL_TQ, stride=_rate)
                og_sc[_g, dst, :] = jnp.where(first_v, o2[:DIL_TQ], o2[DIL_TQ:])
                lse_sc[_g, dst, :] = jnp.where(first_v, lse2[:DIL_TQ], lse2[DIL_TQ:])

            scored, normed = {}, {}
            for step in range(DIL_UNROLL + 2):
                if step < DIL_UNROLL:
                    scored[step] = score(step)
                if 1 <= step <= DIL_UNROLL:
                    s, base, win = scored.pop(step - 1)
                    normed[step - 1] = softmax(s) + (base, win)
                if step >= 2:
                    value(*normed.pop(step - 2))
            return carry

        lax.fori_loop(0, n_tiles // DIL_UNROLL, body, 0)

    m = jnp.maximum(jnp.maximum(lse_sc[0], lse_sc[1]), lse_sc[2])
    w0 = jnp.exp(lse_sc[0] - m)
    w1 = jnp.exp(lse_sc[1] - m)
    w2 = jnp.exp(lse_sc[2] - m)
    o = (w0 * og_sc[0] + w1 * og_sc[1] + w2 * og_sc[2]) / (w0 + w1 + w2)
    o_ref[0] = o.astype(o_ref.dtype)


def _dil(q, k, v, band, batch, seq):
    t = q.shape[1]
    assert seq % DIL_SB == 0
    nsb = seq // DIL_SB

    def q_spec(g):
        return pl.BlockSpec((1, DIL_SB, LANES), lambda b, sp, s, _g=g: (2 * _g + sp, b * nsb + s, 0))

    def kv_spec(g):
        return pl.BlockSpec((1, seq, LANES), lambda b, sp, s, _g=g: (2 * _g + sp, b, 0))

    return pl.pallas_call(
        functools.partial(_dil_kernel, seq=seq),
        grid=(batch, 2, nsb),
        in_specs=[kv_spec(0), kv_spec(1), kv_spec(2), kv_spec(0), kv_spec(1), kv_spec(2),
                  kv_spec(0), kv_spec(1), kv_spec(2),
                  pl.BlockSpec((4, 2 * DIL_TQ, 2 * DIL_TQ), lambda b, sp, s: (0, 0, 0))],
        out_specs=pl.BlockSpec((1, DIL_SB, LANES), lambda b, sp, s: (sp, b * nsb + s, 0)),
        out_shape=jax.ShapeDtypeStruct((2, t, LANES), _BF),
        scratch_shapes=[pltpu.VMEM((DIL_GROUPS, DIL_SB, LANES), _F32),
                        pltpu.VMEM((DIL_GROUPS, DIL_SB, LANES), _F32)],
        compiler_params=pltpu.CompilerParams(
            dimension_semantics=("parallel", "parallel", "arbitrary"), vmem_limit_bytes=VMEM_LIMIT),
        name="dil_attention",
    )(q, q, q, k, k, k, v, v, v, band)


def _mem_kv_kernel(mem_ref, gmem_ref, wckv_ref, gxk_ref, k_ref, v_ref):
    hm = _rms(mem_ref[0], gmem_ref[...]).astype(_BF)
    kv = _dot(hm, wckv_ref[...])
    for hd in range(X_HEADS):
        kh = kv[:, hd * X_HEAD_DIM:(hd + 1) * X_HEAD_DIM]
        k_ref[0, :, hd * X_HEAD_DIM:(hd + 1) * X_HEAD_DIM] = _rms(kh, gxk_ref[...]).astype(_BF)
    v_ref[0] = kv[:, D_MODEL:].astype(_BF)


def _mem_kv(mem, lw, layer):
    b, m, _ = mem.shape
    return pl.pallas_call(
        _mem_kv_kernel,
        grid=(b,),
        in_specs=[pl.BlockSpec((1, m, D_MODEL), lambda i: (i, 0, 0))]
        + [_layer_spec(lw[n], layer) for n in ('g_mem', 'w_ckv', 'g_xk')],
        out_specs=[pl.BlockSpec((1, m, D_MODEL), lambda i: (i, 0, 0))] * 2,
        out_shape=[jax.ShapeDtypeStruct((b, m, D_MODEL), _BF)] * 2,
        compiler_params=pltpu.CompilerParams(dimension_semantics=("parallel",), vmem_limit_bytes=VMEM_LIMIT),
        name="mem_kv",
    )(mem, lw['g_mem'], lw['w_ckv'], lw['g_xk'])


def _mix_out_kernel(x_ref, oa_ref, ob_ref, oc_ref, wo_ref, gcross_ref, wcq_ref, gxq_ref,
                    mk_ref, mv_ref, wco_ref, y_ref):
    mix = jnp.concatenate([oa_ref[p] for p in range(3)] + [ob_ref[p] for p in range(3)]
                          + [oc_ref[p] for p in range(2)], axis=1)
    x1 = x_ref[...] + _dot(mix, wo_ref[...])
    hc = _rms(x1, gcross_ref[...]).astype(_BF)
    q = _dot(hc, wcq_ref[...])
    outs = []
    for hd in range(X_HEADS):
        sl = slice(hd * X_HEAD_DIM, (hd + 1) * X_HEAD_DIM)
        qh = (_rms(q[:, sl], gxq_ref[...]) * X_HEAD_DIM ** -0.5).astype(_BF)
        s = _dot_t(qh, mk_ref[0, :, sl])
        m = jnp.max(s, axis=1, keepdims=True)
        p = jnp.exp(s - m)
        l = jnp.sum(p, axis=1, keepdims=True)
        outs.append((_dot(p.astype(_BF), mv_ref[0, :, sl]) / l).astype(_BF))
    o = jnp.concatenate(outs, axis=1)
    y_ref[...] = x1 + _dot(o, wco_ref[...])


def _mix_out(x, oa, ob, oc, mk, mv, lw, layer, seq):
    t = x.shape[0]
    nblk_seq = seq // TM
    mem_len = mk.shape[1]

    def grp(n):
        return pl.BlockSpec((n, TM, LANES), lambda i: (0, i, 0))

    mem_spec = pl.BlockSpec((1, mem_len, D_MODEL), lambda i: (i // nblk_seq, 0, 0))
    return pl.pallas_call(
        _mix_out_kernel,
        grid=(t // TM,),
        in_specs=[pl.BlockSpec((TM, D_MODEL), lambda i: (i, 0)), grp(3), grp(3), grp(2),
                  _layer_spec(lw['w_o'], layer), _layer_spec(lw['g_cross'], layer),
                  _layer_spec(lw['w_cq'], layer), _layer_spec(lw['g_xq'], layer),
                  mem_spec, mem_spec, _layer_spec(lw['w_co'], layer)],
        out_specs=pl.BlockSpec((TM, D_MODEL), lambda i: (i, 0)),
        out_shape=jax.ShapeDtypeStruct((t, D_MODEL), _F32),
        compiler_params=pltpu.CompilerParams(dimension_semantics=("parallel",), vmem_limit_bytes=VMEM_LIMIT),
        name="mix_out",
    )(x, oa, ob, oc, lw['w_o'], lw['g_cross'], lw['w_cq'], lw['g_xq'], mk, mv, lw['w_co'])


def _ffn_kernel(x_ref, xp_ref, xn_ref, gffn_ref, wup_ref, cw_ref, cb_ref, wdn_ref, y_ref, *, nblk_seq):
    i = pl.program_id(0)
    tm = x_ref.shape[0]
    x = x_ref[...]
    g = gffn_ref[...]
    is_first = (i % nblk_seq) == 0
    is_last = (i % nblk_seq) == nblk_seq - 1
    hp = jnp.where(is_first, 0.0, _rms(xp_ref[...], g))
    hn = jnp.where(is_last, 0.0, _rms(xn_ref[...], g))
    h = jnp.concatenate([hp, _rms(x, g), hn], axis=0).astype(_BF)
    acc = x

    def up_proj(c):
        return [_dot(h, wup_ref[:, part * D_FF + c * FF_CH: part * D_FF + (c + 1) * FF_CH]) for part in range(2)]

    def conv(u, lo):
        cw = cw_ref[:, lo:lo + FF_CH]
        up = pltpu.roll(u, 1, 0)[8:8 + tm]
        un = pltpu.roll(u, tm + 15, 0)[8:8 + tm]
        return up * cw[0:1] + u[8:8 + tm] * cw[1:2] + un * cw[2:3] + cb_ref[:, lo:lo + FF_CH]

    pending = []
    u_next = up_proj(0)
    for c in range(N_FF_CH):
        u_val, u_gate = u_next
        if c + 1 < N_FF_CH:
            u_next = up_proj(c + 1)
        a = conv(u_val, c * FF_CH)
        half_gate = conv(u_gate, D_FF + c * FF_CH)
        pending.append((half_gate * (1.0 + jnp.tanh(half_gate)) * a).astype(_BF))
        rest = N_FF_CH - 1 - c
        if len(pending) == 2 and rest != 1 or rest == 0:
            lo = (c + 1 - len(pending)) * FF_CH
            act = pending[0] if len(pending) == 1 else jnp.concatenate(pending, axis=1)
            acc = acc + _dot(act, wdn_ref[lo:(c + 1) * FF_CH, :])
            pending = []
    y_ref[...] = acc


def _ffn(x, lw, layer, seq):
    t = x.shape[0]
    nblk_seq = seq // TM
    nhalo = t // 8
    per = TM // 8
    return pl.pallas_call(
        functools.partial(_ffn_kernel, nblk_seq=nblk_seq),
        grid=(t // TM,),
        in_specs=[pl.BlockSpec((TM, D_MODEL), lambda i: (i, 0)),
                  pl.BlockSpec((8, D_MODEL), lambda i: (jnp.maximum(i * per - 1, 0), 0)),
                  pl.BlockSpec((8, D_MODEL), lambda i: (jnp.minimum((i + 1) * per, nhalo - 1), 0))]
        + [_layer_spec(lw[n], layer) for n in ('g_ffn', 'w_up', 'conv_w', 'conv_b', 'w_down')],
        out_specs=pl.BlockSpec((TM, D_MODEL), lambda i: (i, 0)),
        out_shape=jax.ShapeDtypeStruct((t, D_MODEL), _F32),
        compiler_params=pltpu.CompilerParams(dimension_semantics=("parallel",), vmem_limit_bytes=VMEM_LIMIT),
        name="conv_ffn",
    )(x, x, x, lw['g_ffn'], lw['w_up'], lw['conv_w'], lw['conv_b'], lw['w_down'])


def _relayout_w_in(w):
    nl = w.shape[0]
    lead = w[..., :SPLIT_CKV]
    a = w[..., SPLIT_CKV:SPLIT_CKV + 16]
    b = w[..., SPLIT_CKV + 16:SPLIT_KR]
    z32 = jnp.zeros(a.shape[:-1] + (32,), w.dtype)
    kr = jnp.concatenate([b, a, z32, a, b, z32], axis=-1)
    na = w[..., SPLIT_KR:SPLIT_NA].reshape(nl, D_MODEL, 3, NA_HEADS // 2, LANES)
    na_ab = na[:, :, :, 0:2].reshape(nl, D_MODEL, 3 * 2 * LANES)
    na_c = na[:, :, :, 2].reshape(nl, D_MODEL, 3 * LANES)
    dl = w[..., SPLIT_NA:].reshape(nl, D_MODEL, 3, DIL_GROUPS, 2, 2, 2, HEAD_DIM // 2)
    qk = dl[:, :, 0:2].transpose(0, 1, 3, 2, 4, 6, 5, 7).reshape(nl, D_MODEL, DIL_GROUPS, 4 * LANES)
    v = dl[:, :, 2].reshape(nl, D_MODEL, DIL_GROUPS, 2 * LANES)
    dil = jnp.concatenate([qk, v], axis=-1).reshape(nl, D_MODEL, DIL_GROUPS * 6 * LANES)
    out = jnp.concatenate([lead, kr, na_ab, na_c, dil], axis=-1)
    assert out.shape[-1] == D_IN_P
    return out


def _pad_heads(w, real, n_heads):
    nl, k, _ = w.shape
    w = w.reshape(nl, k, n_heads, real)
    return jnp.pad(w, ((0, 0), (0, 0), (0, 0), (0, LANES - real))).reshape(nl, k, n_heads * LANES)


def _rows(v):
    return v[:, None, :].astype(_F32)


def _prep_layers(p):
    nl = p['w_in'].shape[0]
    kn, qn = p['mla_kn'], p['mla_qn']
    z32 = jnp.zeros((nl, 32), _F32)
    z64 = jnp.zeros((nl, 64), _F32)
    dq, dk = p['dil_qn'], p['dil_kn']

    def rot(gain):
        return jnp.concatenate([gain[:, :32], gain[:, :32], gain[:, 32:], gain[:, 32:]], axis=-1)

    gate_half = jnp.concatenate([jnp.ones((D_FF,), _F32), jnp.full((D_FF,), 0.5, _F32)])

    return {
        'g_mix': _rows(p['norm_mix']), 'w_in': _relayout_w_in(p['w_in'].astype(_BF)),
        'g_cq': _rows(p['mla_q_norm']), 'g_ckv': _rows(p['mla_kv_norm']),
        'w_uq': _pad_heads(p['w_uq'].astype(_BF), MLA_QK, MLA_HEADS),
        'w_uk': _pad_heads(p['w_uk'].astype(_BF), MLA_NOPE, MLA_HEADS),
        'w_uv': _pad_heads(p['w_uv'].astype(_BF), MLA_V, MLA_HEADS),
        'g_mq': _rows(jnp.concatenate([qn, z32], axis=-1) * (MLA_QK ** -0.5 * LOG2E)),
        'g_mk': _rows(jnp.concatenate([kn[:, :MLA_NOPE], z64], axis=-1)),
        'g_kr': _rows(jnp.concatenate([kn[:, 80:96], kn[:, 64:80], z32, kn[:, 64:80], kn[:, 80:96], z32], axis=-1)),
        'g_nq': _rows(jnp.tile(p['na_qn'], (1, 2)) * HEAD_DIM ** -0.5), 'g_nk': _rows(jnp.tile(p['na_kn'], (1, 2))),
        'g_dq': _rows(rot(dq) * HEAD_DIM ** -0.5), 'g_dk': _rows(rot(dk)),
        'w_o': p['w_o'].astype(_BF),
        'g_cross': _rows(p['norm_cross']), 'w_cq': p['w_cq'].astype(_BF),
        'g_xq': _rows(p['x_qn']), 'g_xk': _rows(p['x_kn']), 'g_mem': _rows(p['norm_mem']),
        'w_ckv': p['w_ckv'].astype(_BF), 'w_co': p['w_co'].astype(_BF),
        'g_ffn': _rows(p['norm_ffn']), 'w_up': p['w_up'].astype(_BF),
        'conv_w': p['conv_w'] * gate_half, 'conv_b': _rows(p['conv_b'] * gate_half),
        'w_down': p['w_down'].astype(_BF),
        'na_bias': _na_bias(p['na_rpb']),
    }


def _na_bias(rpb):
    nl = rpb.shape[0]
    dl = np.arange(NA_WIN_R)
    j = np.arange(NA_WIN_R)
    r_off = j[None, :] - dl[:, None] + (NA_WIN_R - 1)
    qc = np.arange(GRID_W)
    kc = np.arange(GRID_W)
    c_start = np.clip(qc - NA_WIN_C // 2, 0, GRID_W - NA_WIN_C)
    valid = (kc[None, :] >= c_start[:, None]) & (kc[None, :] < c_start[:, None] + NA_WIN_C)
    c_off = kc[None, :] - qc[:, None] + (NA_WIN_C - 1)
    pick = (np.arange(2 * NA_WIN_C - 1)[:, None, None] == c_off[None]) & valid[None]
    rows = rpb[:, :, r_off].reshape(nl, NA_HEADS // 2, 2, NA_WIN_R, NA_WIN_R, 2 * NA_WIN_C - 1)
    b = jnp.einsum('lpedjc,cqk->lpdeqjk', rows, jnp.asarray(pick, _F32), precision=lax.Precision.HIGHEST)
    b = b + jnp.asarray(np.where(valid, 0.0, NEG), _F32)[:, None, :]
    return b.reshape(nl, NA_HEADS // 2, NA_WIN_R, 2 * GRID_W, NA_WIN_R * GRID_W)


def _band_bias():
    i = np.arange(DIL_TQ)[:, None]
    c = np.arange(2 * DIL_TQ)[None, :]
    band = (c >= i) & (c <= i + DIL_TQ)
    out = []
    for edge in range(4):
        ok = band
        if edge & 1:
            ok = ok & (c >= DIL_TQ // 2)
        if edge & 2:
            ok = ok & (c < 2 * DIL_TQ - DIL_TQ // 2)
        out.append(np.tile(np.where(ok, 0.0, NEG), (2, 1)))
    return jnp.asarray(np.stack(out), _F32)


def _rope_tables(seq):
    pos = jnp.arange(seq, dtype=_F32)[:, None]

    def cs(half):
        inv = ROPE_THETA ** (-jnp.arange(half, dtype=_F32) / half)
        ang = pos * inv[None, :]
        return jnp.cos(ang), jnp.sin(ang)

    c16, s16 = cs(MLA_ROPE // 2)
    c32, s32 = cs(HEAD_DIM // 2)
    one = jnp.ones((seq, 64), _F32)
    z16 = jnp.zeros((seq, 16), _F32)
    z32 = jnp.zeros((seq, 32), _F32)
    z64 = jnp.zeros((seq, 64), _F32)
    return {
        'q_c': jnp.concatenate([one, c16, c16, z32], axis=1),
        'q_s1': jnp.concatenate([z64, z16, s16, z32], axis=1),
        'q_s2': jnp.concatenate([z64, -s16, z16, z32], axis=1),
        'k_c': jnp.concatenate([z64, c16, c16, z32], axis=1),
        'k_s': jnp.concatenate([z64, -s16, s16, z32], axis=1),
        'd_c': jnp.concatenate([c32, c32, c32, c32], axis=1),
        'd_s': jnp.concatenate([-s32, -s32, s32, s32], axis=1),
    }


def _trunk_layer(x, mem, lw, layer, tabs, band, batch, seq):
    qm, km, vm, qn, kn, vn, qd, kd, vd = _mix_in(x, lw, layer, tabs, seq)
    oa = _mla(qm, km, vm, batch, seq)
    ob = _na(qn, kn, vn, lw['na_bias'], layer, batch, seq)
    oc = _dil(qd, kd, vd, band, batch, seq)
    mk, mv = _mem_kv(mem, lw, layer)
    x = _mix_out(x, oa, ob, oc, mk, mv, lw, layer, seq)
    return _ffn(x, lw, layer, seq)


def kernel(x_prompt, x_sample, mem_prompt, mem_sample, norm_mix, w_in, mla_q_norm, mla_kv_norm, w_uq, w_uk, w_uv, mla_qn, mla_kn, na_qn, na_kn, na_rpb, dil_qn, dil_kn, w_o, norm_cross, norm_mem, w_cq, w_ckv, x_qn, x_kn, w_co, norm_ffn, w_up, conv_w, conv_b, w_down):
    stacked = {
        'norm_mix': norm_mix, 'w_in': w_in, 'mla_q_norm': mla_q_norm, 'mla_kv_norm': mla_kv_norm,
        'w_uq': w_uq, 'w_uk': w_uk, 'w_uv': w_uv, 'mla_qn': mla_qn, 'mla_kn': mla_kn,
        'na_qn': na_qn, 'na_kn': na_kn, 'na_rpb': na_rpb, 'dil_qn': dil_qn, 'dil_kn': dil_kn,
        'w_o': w_o, 'norm_cross': norm_cross, 'norm_mem': norm_mem, 'w_cq': w_cq, 'w_ckv': w_ckv,
        'x_qn': x_qn, 'x_kn': x_kn, 'w_co': w_co, 'norm_ffn': norm_ffn, 'w_up': w_up,
        'conv_w': conv_w, 'conv_b': conv_b, 'w_down': w_down,
    }
    lw = _prep_layers(stacked)
    band = _band_bias()
    outs = []
    for x, mem in ((x_prompt, mem_prompt), (x_sample, mem_sample)):
        batch, seq, _ = x.shape
        tabs = _rope_tables(seq)
        xt = x.reshape(batch * seq, D_MODEL)
        for layer in range(DEPTH):
            xt = _trunk_layer(xt, mem, lw, layer, tabs, band, batch, seq)
        outs.append(xt.reshape(batch, seq, D_MODEL))
    return tuple(outs)
```

```python
import functools

import jax
import jax.numpy as jnp
import numpy as np
from jax import lax
from jax.experimental import pallas as pl
from jax.experimental.pallas import tpu as pltpu

D_MODEL = 1024
DEPTH = 4
GRID_W = 64
HEAD_DIM = 64
ROPE_THETA = 10000.0
EPS = 1e-6
MLA_HEADS = 6
MLA_Q_RANK = 256
MLA_KV_RANK = 128
MLA_NOPE = 64
MLA_ROPE = 32
MLA_V = 64
MLA_QK = MLA_NOPE + MLA_ROPE
NA_HEADS = 6
NA_WIN_R = 8
NA_WIN_C = 16
DIL_WINDOWS = (128, 512, 2048)
DIL_RATES = (1, 4, 16)
DIL_GROUPS = 3
DIL_HEADS = 4
DIL_ALL = DIL_GROUPS * DIL_HEADS
X_HEADS = 4
X_HEAD_DIM = D_MODEL // X_HEADS
D_FF = 2816
CONV_W = 3
SPLIT_CQ = MLA_Q_RANK
SPLIT_CKV = SPLIT_CQ + MLA_KV_RANK
SPLIT_KR = SPLIT_CKV + MLA_ROPE
SPLIT_NA = SPLIT_KR + 3 * NA_HEADS * HEAD_DIM
D_IN = SPLIT_NA + 3 * DIL_ALL * HEAD_DIM

LANES = 128
NEG = -1e30
LOG2E = 1.4426950408889634
TM = 512
DIL_TQ = 128
DIL_SB = DIL_TQ * max(DIL_RATES)
DIL_UNROLL = 16
NA_RB = 32
FF_CH = 256
N_FF_CH = D_FF // FF_CH
VMEM_LIMIT = 56 * 1024 * 1024

COL_MLA = 0
COL_NA = 512
COL_DIL = COL_NA + 3 * NA_HEADS * HEAD_DIM
D_IN_P = COL_DIL + 3 * DIL_ALL * HEAD_DIM

_BF = jnp.bfloat16
_F32 = jnp.float32


def _dot(a, b):
    return jnp.dot(a, b, preferred_element_type=_F32)


def _dot_t(a, b):
    return lax.dot_general(a, b, (((1,), (1,)), ((), ())), preferred_element_type=_F32)


def _rms(x, g):
    ms = jnp.mean(x * x, axis=-1, keepdims=True)
    return x * lax.rsqrt(ms + EPS) * g


def _lane(shape):
    return lax.broadcasted_iota(jnp.int32, shape, len(shape) - 1)


def _head_join_matrices():
    i = np.arange(2 * LANES)
    same_group = (i[:, None] // LANES) == (i[None, :] // LANES)
    natural = (i[:, None] // HEAD_DIM) == (i[None, :] // HEAD_DIM)
    half = ((i % HEAD_DIM) < HEAD_DIM // 2)
    rotary = same_group & (half[:, None] == half[None, :])
    return [jnp.asarray(m, _BF) for m in (same_group, natural, rotary)]


def _seg_inv(x2, bd_ref, dim):
    ss = _dot((x2 * x2).astype(_BF), bd_ref[...])
    return lax.rsqrt(ss * (1.0 / dim) + EPS)


def _mix_in_kernel(x_ref, gmix_ref, win_ref, gcq_ref, gckv_ref, wuq_ref, wuk_ref, wuv_ref,
                   gmq_ref, gmk_ref, gkr_ref, gnq_ref, gnk_ref, gdq_ref, gdk_ref,
                   bd128_ref, bd64_ref, bdr_ref,
                   tqc_ref, tqs1_ref, tqs2_ref, tkc_ref, tks_ref, tdc_ref, tds_ref,
                   qm_ref, km_ref, vm_ref, qn_ref, kn_ref, vn_ref, qd_ref, kd_ref, vd_ref,
                   perm_ref):
    tm = x_ref.shape[0]
    x = x_ref[...]
    xg = (x * gmix_ref[...]).astype(_BF)
    inv_x = lax.rsqrt(jnp.mean(x * x, axis=-1, keepdims=True) + EPS)
    lane = _lane((tm, LANES))

    def project(lo, hi):
        return _dot(xg, win_ref[:, lo:hi]) * inv_x

    def latent(z):
        cq = _rms(z[:, 0:MLA_Q_RANK], gcq_ref[...]).astype(_BF)
        ckv = _rms(z[:, MLA_Q_RANK:MLA_Q_RANK + MLA_KV_RANK], gckv_ref[...]).astype(_BF)
        kr = z[:, 384:512]
        kr_own = jnp.where((lane >= MLA_NOPE) & (lane < MLA_QK), kr, 0.0)
        krg = kr * gkr_ref[...]
        kr_rot = krg * tkc_ref[...] + pltpu.roll(krg, 64, 1) * tks_ref[...]
        qpre = _dot(cq, wuq_ref[...])
        kpre = _dot(ckv, wuk_ref[...])
        vpre = _dot(ckv, wuv_ref[...])
        row_t = lax.broadcasted_iota(jnp.int32, (LANES, tm), 0)
        for c in range(MLA_HEADS // 2):
            q2 = qpre[:, 2 * c * LANES:(2 * c + 2) * LANES]
            k2 = kpre[:, 2 * c * LANES:(2 * c + 2) * LANES]
            inv_q = _seg_inv(q2, bd128_ref, MLA_QK)
            inv_k = _seg_inv(k2 + jnp.concatenate([kr_own, kr_own], axis=1), bd128_ref, MLA_QK)
            for hh in range(2):
                hd = 2 * c + hh
                sl = slice(hh * LANES, (hh + 1) * LANES)
                qg = q2[:, sl] * gmq_ref[...]
                qr = (qg * tqc_ref[...] + pltpu.roll(qg, 16, 1) * tqs1_ref[...]
                      + pltpu.roll(qg, 112, 1) * tqs2_ref[...])
                qm_ref[hd] = (qr * inv_q[:, sl]).astype(_BF)
                km_ref[hd] = ((k2[:, sl] * gmk_ref[...] + kr_rot) * inv_k[:, sl]).astype(_BF)
                vt = vpre[:, hd * LANES:(hd + 1) * LANES].T
                vm_ref[hd, 0] = jnp.where(row_t == MLA_V, 1.0, vt).astype(_BF)

    def neighbourhood(z):
        q2 = z[:, 0:256]
        k2 = z[:, 256:512]
        qk2 = z[:, 768:1024]
        qn2 = q2 * _seg_inv(q2, bd64_ref, HEAD_DIM)
        kn2 = k2 * _seg_inv(k2, bd64_ref, HEAD_DIM)
        qkn2 = qk2 * _seg_inv(qk2, bd64_ref, HEAD_DIM)
        for p in range(2):
            sl = slice(p * LANES, (p + 1) * LANES)
            qn_ref[p] = (qn2[:, sl] * gnq_ref[...]).astype(_BF)
            kn_ref[p] = (kn2[:, sl] * gnk_ref[...]).astype(_BF)
            vn_ref[p] = z[:, 512 + p * LANES: 512 + (p + 1) * LANES].astype(_BF)
        qn_ref[2] = (qkn2[:, :LANES] * gnq_ref[...]).astype(_BF)
        kn_ref[2] = (qkn2[:, LANES:] * gnk_ref[...]).astype(_BF)
        vn_ref[2] = z[:, 1024:1152].astype(_BF)

    def put(out_ref, p, val, rate, slot):
        if rate == 1:
            out_ref[p] = val.astype(_BF)
        else:
            rows = tm // rate
            perm_ref[slot] = val
            for rho in range(rate):
                out_ref[p, rho * rows:(rho + 1) * rows, :] = (
                    perm_ref[slot, pl.ds(rho, rows, stride=rate), :].astype(_BF))

    def dilated(g, z):
        q2 = z[:, 0:256]
        k2 = z[:, 256:512]
        qn2 = q2 * _seg_inv(q2, bdr_ref, HEAD_DIM)
        kn2 = k2 * _seg_inv(k2, bdr_ref, HEAD_DIM)
        for pp in range(2):
            p = 2 * g + pp
            sl = slice(pp * LANES, (pp + 1) * LANES)
            qx = qn2[:, sl] * gdq_ref[...]
            qx = qx * tdc_ref[...] + pltpu.roll(qx, 64, 1) * tds_ref[...]
            kx = kn2[:, sl] * gdk_ref[...]
            kx = kx * tdc_ref[...] + pltpu.roll(kx, 64, 1) * tds_ref[...]
            put(qd_ref, p, qx, DIL_RATES[g], 3 * pp)
            put(kd_ref, p, kx, DIL_RATES[g], 3 * pp + 1)
            put(vd_ref, p, z[:, 512 + pp * LANES: 512 + (pp + 1) * LANES], DIL_RATES[g], 3 * pp + 2)

    dil_lo = [COL_DIL + g * 768 for g in range(DIL_GROUPS)]
    stages = [(dil_lo[2], dil_lo[2] + 768, functools.partial(dilated, 2)),
              (dil_lo[1], dil_lo[1] + 768, functools.partial(dilated, 1)),
              (dil_lo[0], dil_lo[0] + 768, functools.partial(dilated, 0)),
              (COL_MLA, COL_NA, latent),
              (COL_NA, COL_DIL, neighbourhood)]
    z_next = project(stages[0][0], stages[0][1])
    for idx, (_, _, finish) in enumerate(stages):
        z = z_next
        if idx + 1 < len(stages):
            z_next = project(stages[idx + 1][0], stages[idx + 1][1])
        finish(z)


def _const_spec(shape):
    nd = len(shape)
    return pl.BlockSpec(shape, lambda *_, _nd=nd: (0,) * _nd)


def _layer_spec(arr, layer):
    nd = arr.ndim
    return pl.BlockSpec((None,) + arr.shape[1:], lambda *_, _l=layer, _nd=nd: (_l,) + (0,) * (_nd - 1))


def _mix_in(x, lw, layer, tabs, seq):
    t = x.shape[0]
    nblk_seq = seq // TM
    tab_spec = pl.BlockSpec((TM, LANES), lambda i: (i % nblk_seq, 0))
    weights = [lw[n] for n in ('g_mix', 'w_in', 'g_cq', 'g_ckv', 'w_uq', 'w_uk', 'w_uv',
                               'g_mq', 'g_mk', 'g_kr', 'g_nq', 'g_nk', 'g_dq', 'g_dk')]
    joins = _head_join_matrices()
    tables = [tabs['q_c'], tabs['q_s1'], tabs['q_s2'], tabs['k_c'], tabs['k_s'], tabs['d_c'], tabs['d_s']]

    def out(n):
        return (jax.ShapeDtypeStruct((n, t, LANES), _BF), pl.BlockSpec((n, TM, LANES), lambda i: (0, i, 0)))

    vt_out = (jax.ShapeDtypeStruct((MLA_HEADS, t // TM, LANES, TM), _BF),
              pl.BlockSpec((MLA_HEADS, 1, LANES, TM), lambda i: (0, i, 0, 0)))
    outs = [out(6), out(6), vt_out, out(3), out(3), out(3), out(6), out(6), out(6)]
    return pl.pallas_call(
        _mix_in_kernel,
        grid=(t // TM,),
        in_specs=[pl.BlockSpec((TM, D_MODEL), lambda i: (i, 0))]
        + [_layer_spec(w, layer) for w in weights] + [_const_spec(m.shape) for m in joins]
        + [tab_spec] * len(tables),
        out_specs=[o[1] for o in outs],
        out_shape=[o[0] for o in outs],
        scratch_shapes=[pltpu.VMEM((6, TM, LANES), _F32)],
        compiler_params=pltpu.CompilerParams(dimension_semantics=("parallel",), vmem_limit_bytes=VMEM_LIMIT),
        name="mix_in",
    )(x, *weights, *joins, *tables)


def _mla_kernel(q_ref, k_ref, vt_ref, o_ref, sa_sc, sb_sc, m_sc, acc_sc, *, tk):
    tq = q_ref.shape[1]
    nk = k_ref.shape[1] // tk
    sub = tk // TM
    m_sc[...] = jnp.full(m_sc.shape, NEG, _F32)
    acc_sc[...] = jnp.zeros(acc_sc.shape, _F32)

    def scores(j, dst):
        rows = pl.ds(pl.multiple_of(j * tk, tk), tk)
        for hh in range(2):
            dst[hh] = _dot_t(k_ref[hh, rows, :], q_ref[hh])

    def consume(j, src):
        for hh in range(2):
            s = src[hh]
            m_prev = m_sc[hh]
            m_new = jnp.maximum(m_prev, jnp.max(s, axis=0, keepdims=True))
            p = jnp.exp2(s - m_new).astype(_BF)
            vt = jnp.concatenate([vt_ref[hh, j * sub + c] for c in range(sub)], axis=1)
            acc_sc[hh] = jnp.exp2(m_prev - m_new) * acc_sc[hh] + _dot(vt, p)
            m_sc[hh] = m_new

    def pair(i, carry):
        j = 2 * i
        scores(j + 1, sb_sc)
        consume(j, sa_sc)
        scores(j + 2, sa_sc)
        consume(j + 1, sb_sc)
        return carry

    scores(0, sa_sc)
    lax.fori_loop(0, nk // 2 - 1, pair, 0)
    scores(nk - 1, sb_sc)
    consume(nk - 2, sa_sc)
    consume(nk - 1, sb_sc)

    lane = _lane((tq, LANES))
    outs = []
    for hh in range(2):
        acc = acc_sc[hh]
        outs.append((acc / acc[MLA_V:MLA_V + 1, :]).T)
    o_ref[0] = jnp.where(lane < MLA_V, outs[0], pltpu.roll(outs[1], MLA_V, 1)).astype(o_ref.dtype)


def _mla(q, k, vt, batch, seq, tq=2048, tk=512):
    t = q.shape[1]
    nq = seq // tq
    nblk = seq // TM
    assert seq % (2 * tk) == 0 and tk % TM == 0
    return pl.pallas_call(
        functools.partial(_mla_kernel, tk=tk),
        grid=(batch, MLA_HEADS // 2, nq),
        in_specs=[pl.BlockSpec((2, tq, LANES), lambda b, p, i: (p, b * nq + i, 0)),
                  pl.BlockSpec((2, seq, LANES), lambda b, p, i: (p, b, 0)),
                  pl.BlockSpec((2, nblk, LANES, TM), lambda b, p, i: (p, b, 0, 0))],
        out_specs=pl.BlockSpec((1, tq, LANES), lambda b, p, i: (p, b * nq + i, 0)),
        out_shape=jax.ShapeDtypeStruct((MLA_HEADS // 2, t, LANES), _BF),
        scratch_shapes=[pltpu.VMEM((2, tk, tq), _F32), pltpu.VMEM((2, tk, tq), _F32),
                        pltpu.VMEM((2, 1, tq), _F32), pltpu.VMEM((2, LANES, tq), _F32)],
        compiler_params=pltpu.CompilerParams(
            dimension_semantics=("parallel", "parallel", "arbitrary"), vmem_limit_bytes=VMEM_LIMIT),
        name="mla_attention",
    )(q, k, vt)


def _na_kernel(q_ref, k_ref, v_ref, bias_ref, o_ref, *, rows):
    i = pl.program_id(2)
    lane = _lane((GRID_W, LANES))
    first = lane < HEAD_DIM
    nkeys = NA_WIN_R * GRID_W

    def score(rr):
        r = i * NA_RB + rr
        rs = jnp.clip(r - NA_WIN_R // 2, 0, rows - NA_WIN_R)
        start = pl.multiple_of(rs * GRID_W, GRID_W)
        q = q_ref[0, rr * GRID_W:(rr + 1) * GRID_W, :]
        zero = jnp.zeros_like(q)
        q2 = jnp.concatenate([jnp.where(first, q, zero), jnp.where(first, zero, q)], axis=0)
        return _dot_t(q2, k_ref[0, pl.ds(start, nkeys), :]) + bias_ref[0, r - rs], start

    def softmax(s):
        p = jnp.exp(s - jnp.max(s, axis=1, keepdims=True))
        return p.astype(_BF), jnp.sum(p, axis=1, keepdims=True)

    def value(rr, p, l, start):
        o2 = _dot(p, v_ref[0, pl.ds(start, nkeys), :]) / l
        o = jnp.where(first, o2[:GRID_W], o2[GRID_W:])
        o_ref[0, rr * GRID_W:(rr + 1) * GRID_W, :] = o.astype(o_ref.dtype)

    scored, normed = {}, {}
    for step in range(NA_RB + 2):
        if step < NA_RB:
            scored[step] = score(step)
        if 1 <= step <= NA_RB:
            s, start = scored.pop(step - 1)
            normed[step - 1] = softmax(s) + (start,)
        if step >= 2:
            value(step - 2, *normed.pop(step - 2))


def _na(q, k, v, bias, layer, batch, seq):
    t = q.shape[1]
    rows = seq // GRID_W
    assert rows >= NA_WIN_R and rows % NA_RB == 0
    nrb = rows // NA_RB
    qrows = NA_RB * GRID_W
    return pl.pallas_call(
        functools.partial(_na_kernel, rows=rows),
        grid=(batch, NA_HEADS // 2, nrb),
        in_specs=[pl.BlockSpec((1, qrows, LANES), lambda b, p, i: (p, b * nrb + i, 0)),
                  pl.BlockSpec((1, seq, LANES), lambda b, p, i: (p, b, 0)),
                  pl.BlockSpec((1, seq, LANES), lambda b, p, i: (p, b, 0)),
                  pl.BlockSpec((None, 1, NA_WIN_R, 2 * GRID_W, NA_WIN_R * GRID_W),
                               lambda b, p, i: (layer, p, 0, 0, 0))],
        out_specs=pl.BlockSpec((1, qrows, LANES), lambda b, p, i: (p, b * nrb + i, 0)),
        out_shape=jax.ShapeDtypeStruct((NA_HEADS // 2, t, LANES), _BF),
        compiler_params=pltpu.CompilerParams(
            dimension_semantics=("parallel", "parallel", "arbitrary"), vmem_limit_bytes=VMEM_LIMIT),
        name="na_attention",
    )(q, k, v, bias)


def _dil_kernel(q0_ref, q1_ref, q2_ref, k0_ref, k1_ref, k2_ref, v0_ref, v1_ref, v2_ref, band_ref,
                o_ref, og_sc, lse_sc, *, seq):
    sb = pl.program_id(2)
    q_refs = (q0_ref, q1_ref, q2_ref)
    k_refs = (k0_ref, k1_ref, k2_ref)
    v_refs = (v0_ref, v1_ref, v2_ref)
    lane = _lane((DIL_TQ, LANES))
    first_qk = (lane % HEAD_DIM) < (HEAD_DIM // 2)
    first_v = lane < HEAD_DIM
    n_tiles = DIL_SB // DIL_TQ
    half = DIL_TQ // 2

    for g in range(DIL_GROUPS):
        rate = DIL_RATES[g]
        length = seq // rate
        per_blk = TM // rate
        chunk = min(half, per_blk)
        n_chunks = (2 * DIL_TQ) // chunk
        tiles_per_res = n_tiles // rate
        q_ref, k_ref, v_ref = q_refs[g], k_refs[g], v_refs[g]

        def row_of(m, rho, _per_blk=per_blk):
            return (m // _per_blk) * TM + rho * _per_blk + (m % _per_blk)

        def body(tg, carry, _g=g, _rate=rate, _length=length, _per_blk=per_blk, _chunk=chunk,
                 _n_chunks=n_chunks, _tpr=tiles_per_res, _q=q_ref, _k=k_ref, _v=v_ref):
            def score(u):
                tt = tg * DIL_UNROLL + u
                rho = tt // _tpr
                mt = tt % _tpr
                m0 = sb * (DIL_SB // _rate) + mt * DIL_TQ
                q = jnp.concatenate(
                    [_q[0, pl.ds(pl.multiple_of(row_of(m0 + c * _chunk, rho), _chunk), _chunk), :]
                     for c in range(DIL_TQ // _chunk)], axis=0)
                zero = jnp.zeros_like(q)
                q2 = jnp.concatenate([jnp.where(first_qk, q, zero), jnp.where(first_qk, zero, q)], axis=0)
                win = []
                for c in range(_n_chunks):
                    ms = jnp.clip(m0 - half + c * _chunk, 0, _length - _chunk)
                    win.append(pl.multiple_of(row_of(ms, rho), _chunk))
                kw = jnp.concatenate([_k[0, pl.ds(row, _chunk), :] for row in win], axis=0)
                edge = (m0 == 0).astype(jnp.int32) + 2 * (m0 == _length - DIL_TQ).astype(jnp.int32)
                return _dot_t(q2, kw) + band_ref[edge], rho + _rate * mt * DIL_TQ, win

            def softmax(s):
                m = jnp.max(s, axis=1, keepdims=True)
                p = jnp.exp(s - m)
                return p.astype(_BF), m, jnp.sum(p, axis=1, keepdims=True)

            def value(p, m, l, base, win):
                vw = jnp.concatenate([_v[0, pl.ds(row, _chunk), :] for row in win], axis=0)
                o2 = _dot(p, vw) / l
                lse2 = jnp.broadcast_to(m + jnp.log(l), (2 * DIL_TQ, LANES))
                if _rate == 1:
                    dst = pl.ds(pl.multiple_of(base, DIL_TQ), DIL_TQ)
                else:
                    dst = pl.ds(base, DIL_TQ, stride=_rate)
                og_sc[_g, dst, :] = jnp.where(first_v, o2[:DIL_TQ], o2[DIL_TQ:])
                lse_sc[_g, dst, :] = jnp.where(first_v, lse2[:DIL_TQ], lse2[DIL_TQ:])

            scored, normed = {}, {}
            for step in range(DIL_UNROLL + 2):
                if step < DIL_UNROLL:
                    scored[step] = score(step)
                if 1 <= step <= DIL_UNROLL:
                    s, base, win = scored.pop(step - 1)
                    normed[step - 1] = softmax(s) + (base, win)
                if step >= 2:
                    value(*normed.pop(step - 2))
            return carry

        lax.fori_loop(0, n_tiles // DIL_UNROLL, body, 0)

    m = jnp.maximum(jnp.maximum(lse_sc[0], lse_sc[1]), lse_sc[2])
    w0 = jnp.exp(lse_sc[0] - m)
    w1 = jnp.exp(lse_sc[1] - m)
    w2 = jnp.exp(lse_sc[2] - m)
    o = (w0 * og_sc[0] + w1 * og_sc[1] + w2 * og_sc[2]) / (w0 + w1 + w2)
    o_ref[0] = o.astype(o_ref.dtype)


def _dil(q, k, v, band, batch, seq):
    t = q.shape[1]
    assert seq % DIL_SB == 0
    nsb = seq // DIL_SB

    def q_spec(g):
        return pl.BlockSpec((1, DIL_SB, LANES), lambda b, sp, s, _g=g: (2 * _g + sp, b * nsb + s, 0))

    def kv_spec(g):
        return pl.BlockSpec((1, seq, LANES), lambda b, sp, s, _g=g: (2 * _g + sp, b, 0))

    return pl.pallas_call(
        functools.partial(_dil_kernel, seq=seq),
        grid=(batch, 2, nsb),
        in_specs=[kv_spec(0), kv_spec(1), kv_spec(2), kv_spec(0), kv_spec(1), kv_spec(2),
                  kv_spec(0), kv_spec(1), kv_spec(2),
                  pl.BlockSpec((4, 2 * DIL_TQ, 2 * DIL_TQ), lambda b, sp, s: (0, 0, 0))],
        out_specs=pl.BlockSpec((1, DIL_SB, LANES), lambda b, sp, s: (sp, b * nsb + s, 0)),
        out_shape=jax.ShapeDtypeStruct((2, t, LANES), _BF),
        scratch_shapes=[pltpu.VMEM((DIL_GROUPS, DIL_SB, LANES), _F32),
                        pltpu.VMEM((DIL_GROUPS, DIL_SB, LANES), _F32)],
        compiler_params=pltpu.CompilerParams(
            dimension_semantics=("parallel", "parallel", "arbitrary"), vmem_limit_bytes=VMEM_LIMIT),
        name="dil_attention",
    )(q, q, q, k, k, k, v, v, v, band)


def _mem_kv_kernel(mem_ref, gmem_ref, wckv_ref, gxk_ref, k_ref, v_ref):
    hm = _rms(mem_ref[0], gmem_ref[...]).astype(_BF)
    kv = _dot(hm, wckv_ref[...])
    for hd in range(X_HEADS):
        kh = kv[:, hd * X_HEAD_DIM:(hd + 1) * X_HEAD_DIM]
        k_ref[0, :, hd * X_HEAD_DIM:(hd + 1) * X_HEAD_DIM] = _rms(kh, gxk_ref[...]).astype(_BF)
    v_ref[0] = kv[:, D_MODEL:].astype(_BF)


def _mem_kv(mem, lw, layer):
    b, m, _ = mem.shape
    return pl.pallas_call(
        _mem_kv_kernel,
        grid=(b,),
        in_specs=[pl.BlockSpec((1, m, D_MODEL), lambda i: (i, 0, 0))]
        + [_layer_spec(lw[n], layer) for n in ('g_mem', 'w_ckv', 'g_xk')],
        out_specs=[pl.BlockSpec((1, m, D_MODEL), lambda i: (i, 0, 0))] * 2,
        out_shape=[jax.ShapeDtypeStruct((b, m, D_MODEL), _BF)] * 2,
        compiler_params=pltpu.CompilerParams(dimension_semantics=("parallel",), vmem_limit_bytes=VMEM_LIMIT),
        name="mem_kv",
    )(mem, lw['g_mem'], lw['w_ckv'], lw['g_xk'])


def _mix_out_kernel(x_ref, oa_ref, ob_ref, oc_ref, wo_ref, gcross_ref, wcq_ref, gxq_ref,
                    mk_ref, mv_ref, wco_ref, y_ref):
    mix = jnp.concatenate([oa_ref[p] for p in range(3)] + [ob_ref[p] for p in range(3)]
                          + [oc_ref[p] for p in range(2)], axis=1)
    x1 = x_ref[...] + _dot(mix, wo_ref[...])
    inv_x = lax.rsqrt(jnp.mean(x1 * x1, axis=-1, keepdims=True) + EPS)
    q = _dot((x1 * gcross_ref[...]).astype(_BF), wcq_ref[...]) * inv_x
    heads = [slice(hd * X_HEAD_DIM, (hd + 1) * X_HEAD_DIM) for hd in range(X_HEADS)]
    scores = [_dot_t((_rms(q[:, sl], gxq_ref[...]) * X_HEAD_DIM ** -0.5).astype(_BF), mk_ref[0, :, sl])
              for sl in heads]
    probs = []
    for s in scores:
        p = jnp.exp(s - jnp.max(s, axis=1, keepdims=True))
        probs.append((p.astype(_BF), jnp.sum(p, axis=1, keepdims=True)))
    outs = [(_dot(p, mv_ref[0, :, sl]) / l).astype(_BF) for (p, l), sl in zip(probs, heads)]
    o = jnp.concatenate(outs, axis=1)
    y_ref[...] = x1 + _dot(o, wco_ref[...])


def _mix_out(x, oa, ob, oc, mk, mv, lw, layer, seq):
    t = x.shape[0]
    nblk_seq = seq // TM
    mem_len = mk.shape[1]

    def grp(n):
        return pl.BlockSpec((n, TM, LANES), lambda i: (0, i, 0))

    mem_spec = pl.BlockSpec((1, mem_len, D_MODEL), lambda i: (i // nblk_seq, 0, 0))
    return pl.pallas_call(
        _mix_out_kernel,
        grid=(t // TM,),
        in_specs=[pl.BlockSpec((TM, D_MODEL), lambda i: (i, 0)), grp(3), grp(3), grp(2),
                  _layer_spec(lw['w_o'], layer), _layer_spec(lw['g_cross'], layer),
                  _layer_spec(lw['w_cq'], layer), _layer_spec(lw['g_xq'], layer),
                  mem_spec, mem_spec, _layer_spec(lw['w_co'], layer)],
        out_specs=pl.BlockSpec((TM, D_MODEL), lambda i: (i, 0)),
        out_shape=jax.ShapeDtypeStruct((t, D_MODEL), _F32),
        compiler_params=pltpu.CompilerParams(dimension_semantics=("parallel",), vmem_limit_bytes=VMEM_LIMIT),
        name="mix_out",
    )(x, oa, ob, oc, lw['w_o'], lw['g_cross'], lw['w_cq'], lw['g_xq'], mk, mv, lw['w_co'])


def _ffn_kernel(x_ref, xp_ref, xn_ref, gffn_ref, wup_ref, cw_ref, cb_ref, wdn_ref, y_ref, *, nblk_seq):
    i = pl.program_id(0)
    tm = x_ref.shape[0]
    x = x_ref[...]
    g = gffn_ref[...]
    is_first = (i % nblk_seq) == 0
    is_last = (i % nblk_seq) == nblk_seq - 1
    hp = jnp.where(is_first, 0.0, _rms(xp_ref[...], g))
    hn = jnp.where(is_last, 0.0, _rms(xn_ref[...], g))
    h = jnp.concatenate([hp, _rms(x, g), hn], axis=0).astype(_BF)
    def up_proj(c):
        return [_dot(h, wup_ref[:, part * D_FF + c * FF_CH: part * D_FF + (c + 1) * FF_CH]) for part in range(2)]

    def conv(u, lo):
        cw = cw_ref[:, lo:lo + FF_CH]
        up = pltpu.roll(u, 1, 0)[8:8 + tm]
        un = pltpu.roll(u, tm + 15, 0)[8:8 + tm]
        return up * cw[0:1] + u[8:8 + tm] * cw[1:2] + un * cw[2:3] + cb_ref[:, lo:lo + FF_CH]

    acts = []
    u_next = up_proj(0)
    for c in range(N_FF_CH):
        u_val, u_gate = u_next
        if c + 1 < N_FF_CH:
            u_next = up_proj(c + 1)
        a = conv(u_val, c * FF_CH)
        half_gate = conv(u_gate, D_FF + c * FF_CH)
        acts.append((half_gate * (1.0 + jnp.tanh(half_gate)) * a).astype(_BF))
    y_ref[...] = x + _dot(jnp.concatenate(acts, axis=1), wdn_ref[...])


def _ffn(x, lw, layer, seq):
    t = x.shape[0]
    nblk_seq = seq // TM
    nhalo = t // 8
    per = TM // 8
    return pl.pallas_call(
        functools.partial(_ffn_kernel, nblk_seq=nblk_seq),
        grid=(t // TM,),
        in_specs=[pl.BlockSpec((TM, D_MODEL), lambda i: (i, 0)),
                  pl.BlockSpec((8, D_MODEL), lambda i: (jnp.maximum(i * per - 1, 0), 0)),
                  pl.BlockSpec((8, D_MODEL), lambda i: (jnp.minimum((i + 1) * per, nhalo - 1), 0))]
        + [_layer_spec(lw[n], layer) for n in ('g_ffn', 'w_up', 'conv_w', 'conv_b', 'w_down')],
        out_specs=pl.BlockSpec((TM, D_MODEL), lambda i: (i, 0)),
        out_shape=jax.ShapeDtypeStruct((t, D_MODEL), _F32),
        compiler_params=pltpu.CompilerParams(dimension_semantics=("parallel",), vmem_limit_bytes=VMEM_LIMIT),
        name="conv_ffn",
    )(x, x, x, lw['g_ffn'], lw['w_up'], lw['conv_w'], lw['conv_b'], lw['w_down'])


def _relayout_w_in(w):
    nl = w.shape[0]
    lead = w[..., :SPLIT_CKV]
    a = w[..., SPLIT_CKV:SPLIT_CKV + 16]
    b = w[..., SPLIT_CKV + 16:SPLIT_KR]
    z32 = jnp.zeros(a.shape[:-1] + (32,), w.dtype)
    kr = jnp.concatenate([b, a, z32, a, b, z32], axis=-1)
    na = w[..., SPLIT_KR:SPLIT_NA].reshape(nl, D_MODEL, 3, NA_HEADS // 2, LANES)
    na_ab = na[:, :, :, 0:2].reshape(nl, D_MODEL, 3 * 2 * LANES)
    na_c = na[:, :, :, 2].reshape(nl, D_MODEL, 3 * LANES)
    dl = w[..., SPLIT_NA:].reshape(nl, D_MODEL, 3, DIL_GROUPS, 2, 2, 2, HEAD_DIM // 2)
    qk = dl[:, :, 0:2].transpose(0, 1, 3, 2, 4, 6, 5, 7).reshape(nl, D_MODEL, DIL_GROUPS, 4 * LANES)
    v = dl[:, :, 2].reshape(nl, D_MODEL, DIL_GROUPS, 2 * LANES)
    dil = jnp.concatenate([qk, v], axis=-1).reshape(nl, D_MODEL, DIL_GROUPS * 6 * LANES)
    out = jnp.concatenate([lead, kr, na_ab, na_c, dil], axis=-1)
    assert out.shape[-1] == D_IN_P
    return out


def _pad_heads(w, real, n_heads):
    nl, k, _ = w.shape
    w = w.reshape(nl, k, n_heads, real)
    return jnp.pad(w, ((0, 0), (0, 0), (0, 0), (0, LANES - real))).reshape(nl, k, n_heads * LANES)


def _rows(v):
    return v[:, None, :].astype(_F32)


def _prep_layers(p):
    nl = p['w_in'].shape[0]
    kn, qn = p['mla_kn'], p['mla_qn']
    z32 = jnp.zeros((nl, 32), _F32)
    z64 = jnp.zeros((nl, 64), _F32)
    dq, dk = p['dil_qn'], p['dil_kn']

    def rot(gain):
        return jnp.concatenate([gain[:, :32], gain[:, :32], gain[:, 32:], gain[:, 32:]], axis=-1)

    gate_half = jnp.concatenate([jnp.ones((D_FF,), _F32), jnp.full((D_FF,), 0.5, _F32)])

    return {
        'g_mix': _rows(p['norm_mix']), 'w_in': _relayout_w_in(p['w_in'].astype(_BF)),
        'g_cq': _rows(p['mla_q_norm']), 'g_ckv': _rows(p['mla_kv_norm']),
        'w_uq': _pad_heads(p['w_uq'].astype(_BF), MLA_QK, MLA_HEADS),
        'w_uk': _pad_heads(p['w_uk'].astype(_BF), MLA_NOPE, MLA_HEADS),
        'w_uv': _pad_heads(p['w_uv'].astype(_BF), MLA_V, MLA_HEADS),
        'g_mq': _rows(jnp.concatenate([qn, z32], axis=-1) * (MLA_QK ** -0.5 * LOG2E)),
        'g_mk': _rows(jnp.concatenate([kn[:, :MLA_NOPE], z64], axis=-1)),
        'g_kr': _rows(jnp.concatenate([kn[:, 80:96], kn[:, 64:80], z32, kn[:, 64:80], kn[:, 80:96], z32], axis=-1)),
        'g_nq': _rows(jnp.tile(p['na_qn'], (1, 2)) * HEAD_DIM ** -0.5), 'g_nk': _rows(jnp.tile(p['na_kn'], (1, 2))),
        'g_dq': _rows(rot(dq) * HEAD_DIM ** -0.5), 'g_dk': _rows(rot(dk)),
        'w_o': p['w_o'].astype(_BF),
        'g_cross': _rows(p['norm_cross']), 'w_cq': p['w_cq'].astype(_BF),
        'g_xq': _rows(p['x_qn']), 'g_xk': _rows(p['x_kn']), 'g_mem': _rows(p['norm_mem']),
        'w_ckv': p['w_ckv'].astype(_BF), 'w_co': p['w_co'].astype(_BF),
        'g_ffn': _rows(p['norm_ffn']), 'w_up': p['w_up'].astype(_BF),
        'conv_w': p['conv_w'] * gate_half, 'conv_b': _rows(p['conv_b'] * gate_half),
        'w_down': p['w_down'].astype(_BF),
        'na_bias': _na_bias(p['na_rpb']),
    }


def _na_bias(rpb):
    nl = rpb.shape[0]
    dl = np.arange(NA_WIN_R)
    j = np.arange(NA_WIN_R)
    r_off = j[None, :] - dl[:, None] + (NA_WIN_R - 1)
    qc = np.arange(GRID_W)
    kc = np.arange(GRID_W)
    c_start = np.clip(qc - NA_WIN_C // 2, 0, GRID_W - NA_WIN_C)
    valid = (kc[None, :] >= c_start[:, None]) & (kc[None, :] < c_start[:, None] + NA_WIN_C)
    c_off = kc[None, :] - qc[:, None] + (NA_WIN_C - 1)
    pick = (np.arange(2 * NA_WIN_C - 1)[:, None, None] == c_off[None]) & valid[None]
    rows = rpb[:, :, r_off].reshape(nl, NA_HEADS // 2, 2, NA_WIN_R, NA_WIN_R, 2 * NA_WIN_C - 1)
    b = jnp.einsum('lpedjc,cqk->lpdeqjk', rows, jnp.asarray(pick, _F32), precision=lax.Precision.HIGHEST)
    b = b + jnp.asarray(np.where(valid, 0.0, NEG), _F32)[:, None, :]
    return b.reshape(nl, NA_HEADS // 2, NA_WIN_R, 2 * GRID_W, NA_WIN_R * GRID_W)


def _band_bias():
    i = np.arange(DIL_TQ)[:, None]
    c = np.arange(2 * DIL_TQ)[None, :]
    band = (c >= i) & (c <= i + DIL_TQ)
    out = []
    for edge in range(4):
        ok = band
        if edge & 1:
            ok = ok & (c >= DIL_TQ // 2)
        if edge & 2:
            ok = ok & (c < 2 * DIL_TQ - DIL_TQ // 2)
        out.append(np.tile(np.where(ok, 0.0, NEG), (2, 1)))
    return jnp.asarray(np.stack(out), _F32)


def _rope_tables(seq):
    pos = jnp.arange(seq, dtype=_F32)[:, None]

    def cs(half):
        inv = ROPE_THETA ** (-jnp.arange(half, dtype=_F32) / half)
        ang = pos * inv[None, :]
        return jnp.cos(ang), jnp.sin(ang)

    c16, s16 = cs(MLA_ROPE // 2)
    c32, s32 = cs(HEAD_DIM // 2)
    one = jnp.ones((seq, 64), _F32)
    z16 = jnp.zeros((seq, 16), _F32)
    z32 = jnp.zeros((seq, 32), _F32)
    z64 = jnp.zeros((seq, 64), _F32)
    return {
        'q_c': jnp.concatenate([one, c16, c16, z32], axis=1),
        'q_s1': jnp.concatenate([z64, z16, s16, z32], axis=1),
        'q_s2': jnp.concatenate([z64, -s16, z16, z32], axis=1),
        'k_c': jnp.concatenate([z64, c16, c16, z32], axis=1),
        'k_s': jnp.concatenate([z64, -s16, s16, z32], axis=1),
        'd_c': jnp.concatenate([c32, c32, c32, c32], axis=1),
        'd_s': jnp.concatenate([-s32, -s32, s32, s32], axis=1),
    }


def _trunk_layer(x, mem, lw, layer, tabs, band, batch, seq):
    qm, km, vm, qn, kn, vn, qd, kd, vd = _mix_in(x, lw, layer, tabs, seq)
    oa = _mla(qm, km, vm, batch, seq)
    ob = _na(qn, kn, vn, lw['na_bias'], layer, batch, seq)
    oc = _dil(qd, kd, vd, band, batch, seq)
    mk, mv = _mem_kv(mem, lw, layer)
    x = _mix_out(x, oa, ob, oc, mk, mv, lw, layer, seq)
    return _ffn(x, lw, layer, seq)


def kernel(x_prompt, x_sample, mem_prompt, mem_sample, norm_mix, w_in, mla_q_norm, mla_kv_norm, w_uq, w_uk, w_uv, mla_qn, mla_kn, na_qn, na_kn, na_rpb, dil_qn, dil_kn, w_o, norm_cross, norm_mem, w_cq, w_ckv, x_qn, x_kn, w_co, norm_ffn, w_up, conv_w, conv_b, w_down):
    stacked = {
        'norm_mix': norm_mix, 'w_in': w_in, 'mla_q_norm': mla_q_norm, 'mla_kv_norm': mla_kv_norm,
        'w_uq': w_uq, 'w_uk': w_uk, 'w_uv': w_uv, 'mla_qn': mla_qn, 'mla_kn': mla_kn,
        'na_qn': na_qn, 'na_kn': na_kn, 'na_rpb': na_rpb, 'dil_qn': dil_qn, 'dil_kn': dil_kn,
        'w_o': w_o, 'norm_cross': norm_cross, 'norm_mem': norm_mem, 'w_cq': w_cq, 'w_ckv': w_ckv,
        'x_qn': x_qn, 'x_kn': x_kn, 'w_co': w_co, 'norm_ffn': norm_ffn, 'w_up': w_up,
        'conv_w': conv_w, 'conv_b': conv_b, 'w_down': w_down,
    }
    lw = _prep_layers(stacked)
    band = _band_bias()
    outs = []
    for x, mem in ((x_prompt, mem_prompt), (x_sample, mem_sample)):
        batch, seq, _ = x.shape
        tabs = _rope_tables(seq)
        xt = x.reshape(batch * seq, D_MODEL)
        for layer in range(DEPTH):
            xt = _trunk_layer(xt, mem, lw, layer, tabs, band, batch, seq)
        outs.append(xt.reshape(batch, seq, D_MODEL))
    return tuple(outs)
```

```python
import functools

import jax
import jax.numpy as jnp
import numpy as np
from jax import lax
from jax.experimental import pallas as pl
from jax.experimental.pallas import tpu as pltpu

D_MODEL = 1024
DEPTH = 4
GRID_W = 64
HEAD_DIM = 64
ROPE_THETA = 10000.0
EPS = 1e-6
MLA_HEADS = 6
MLA_Q_RANK = 256
MLA_KV_RANK = 128
MLA_NOPE = 64
MLA_ROPE = 32
MLA_V = 64
MLA_QK = MLA_NOPE + MLA_ROPE
NA_HEADS = 6
NA_WIN_R = 8
NA_WIN_C = 16
DIL_WINDOWS = (128, 512, 2048)
DIL_RATES = (1, 4, 16)
DIL_GROUPS = 3
DIL_HEADS = 4
DIL_ALL = DIL_GROUPS * DIL_HEADS
X_HEADS = 4
X_HEAD_DIM = D_MODEL // X_HEADS
D_FF = 2816
CONV_W = 3
SPLIT_CQ = MLA_Q_RANK
SPLIT_CKV = SPLIT_CQ + MLA_KV_RANK
SPLIT_KR = SPLIT_CKV + MLA_ROPE
SPLIT_NA = SPLIT_KR + 3 * NA_HEADS * HEAD_DIM
D_IN = SPLIT_NA + 3 * DIL_ALL * HEAD_DIM

LANES = 128
NEG = -1e30
LOG2E = 1.4426950408889634
TM = 512
DIL_TQ = 128
DIL_SB = DIL_TQ * max(DIL_RATES)
DIL_UNROLL = 16
NA_RB = 32
FF_CH = 256
N_FF_CH = D_FF // FF_CH
VMEM_LIMIT = 56 * 1024 * 1024

COL_MLA = 0
COL_NA = 512
COL_DIL = COL_NA + 3 * NA_HEADS * HEAD_DIM
D_IN_P = COL_DIL + 3 * DIL_ALL * HEAD_DIM

_BF = jnp.bfloat16
_F32 = jnp.float32


def _dot(a, b):
    return jnp.dot(a, b, preferred_element_type=_F32)


def _dot_t(a, b):
    return lax.dot_general(a, b, (((1,), (1,)), ((), ())), preferred_element_type=_F32)


def _rms(x, g):
    ms = jnp.mean(x * x, axis=-1, keepdims=True)
    return x * lax.rsqrt(ms + EPS) * g


def _lane(shape):
    return lax.broadcasted_iota(jnp.int32, shape, len(shape) - 1)


def _head_join_matrices():
    i = np.arange(2 * LANES)
    same_group = (i[:, None] // LANES) == (i[None, :] // LANES)
    natural = (i[:, None] // HEAD_DIM) == (i[None, :] // HEAD_DIM)
    half = ((i % HEAD_DIM) < HEAD_DIM // 2)
    rotary = same_group & (half[:, None] == half[None, :])
    return [jnp.asarray(m, _BF) for m in (same_group, natural, rotary)]


def _seg_inv(slabs, bd_ref, dim):
    rows = slabs[0].shape[0]
    sq = jnp.concatenate([(x2 * x2).astype(_BF) for x2 in slabs], axis=0)
    inv = lax.rsqrt(_dot(sq, bd_ref[...]) * (1.0 / dim) + EPS)
    return [inv[n * rows:(n + 1) * rows] for n in range(len(slabs))]


def _mix_in_kernel(x_ref, gmix_ref, win_ref, gcq_ref, gckv_ref, wuq_ref, wuk_ref, wuv_ref,
                   gmq_ref, gmk_ref, gkr_ref, gnq_ref, gnk_ref, gdq_ref, gdk_ref,
                   bd128_ref, bd64_ref, bdr_ref,
                   tqc_ref, tqs1_ref, tqs2_ref, tkc_ref, tks_ref, tdc_ref, tds_ref,
                   qm_ref, km_ref, vm_ref, qn_ref, kn_ref, vn_ref, qd_ref, kd_ref, vd_ref,
                   perm_ref):
    tm = x_ref.shape[0]
    x = x_ref[...]
    xg = (x * gmix_ref[...]).astype(_BF)
    inv_x = lax.rsqrt(jnp.mean(x * x, axis=-1, keepdims=True) + EPS)
    lane = _lane((tm, LANES))

    def project(lo, hi):
        return _dot(xg, win_ref[:, lo:hi]) * inv_x

    def latent(z):
        cq = _rms(z[:, 0:MLA_Q_RANK], gcq_ref[...]).astype(_BF)
        ckv = _rms(z[:, MLA_Q_RANK:MLA_Q_RANK + MLA_KV_RANK], gckv_ref[...]).astype(_BF)
        kr = z[:, 384:512]
        kr_own = jnp.where((lane >= MLA_NOPE) & (lane < MLA_QK), kr, 0.0)
        krg = kr * gkr_ref[...]
        kr_rot = krg * tkc_ref[...] + pltpu.roll(krg, 64, 1) * tks_ref[...]
        qpre = _dot(cq, wuq_ref[...])
        kpre = _dot(ckv, wuk_ref[...])
        vpre = _dot(ckv, wuv_ref[...])
        row_t = lax.broadcasted_iota(jnp.int32, (LANES, tm), 0)
        npair = MLA_HEADS // 2
        q_pairs = [qpre[:, 2 * c * LANES:(2 * c + 2) * LANES] for c in range(npair)]
        k_pairs = [kpre[:, 2 * c * LANES:(2 * c + 2) * LANES] for c in range(npair)]
        kr2 = jnp.concatenate([kr_own, kr_own], axis=1)
        invs = _seg_inv(q_pairs + [k2 + kr2 for k2 in k_pairs], bd128_ref, MLA_QK)
        for c in range(npair):
            q2, k2, inv_q, inv_k = q_pairs[c], k_pairs[c], invs[c], invs[npair + c]
            for hh in range(2):
                hd = 2 * c + hh
                sl = slice(hh * LANES, (hh + 1) * LANES)
                qg = q2[:, sl] * gmq_ref[...]
                qr = (qg * tqc_ref[...] + pltpu.roll(qg, 16, 1) * tqs1_ref[...]
                      + pltpu.roll(qg, 112, 1) * tqs2_ref[...])
                qm_ref[hd] = (qr * inv_q[:, sl]).astype(_BF)
                km_ref[hd] = ((k2[:, sl] * gmk_ref[...] + kr_rot) * inv_k[:, sl]).astype(_BF)
                vt = vpre[:, hd * LANES:(hd + 1) * LANES].T
                vm_ref[hd, 0] = jnp.where(row_t == MLA_V, 1.0, vt).astype(_BF)

    def neighbourhood(z):
        q2 = z[:, 0:256]
        k2 = z[:, 256:512]
        qk2 = z[:, 768:1024]
        inv_q, inv_k, inv_qk = _seg_inv([q2, k2, qk2], bd64_ref, HEAD_DIM)
        qn2, kn2, qkn2 = q2 * inv_q, k2 * inv_k, qk2 * inv_qk
        for p in range(2):
            sl = slice(p * LANES, (p + 1) * LANES)
            qn_ref[p] = (qn2[:, sl] * gnq_ref[...]).astype(_BF)
            kn_ref[p] = (kn2[:, sl] * gnk_ref[...]).astype(_BF)
            vn_ref[p] = z[:, 512 + p * LANES: 512 + (p + 1) * LANES].astype(_BF)
        qn_ref[2] = (qkn2[:, :LANES] * gnq_ref[...]).astype(_BF)
        kn_ref[2] = (qkn2[:, LANES:] * gnk_ref[...]).astype(_BF)
        vn_ref[2] = z[:, 1024:1152].astype(_BF)

    def put(out_ref, p, val, rate, slot):
        if rate == 1:
            out_ref[p] = val.astype(_BF)
        else:
            rows = tm // rate
            perm_ref[slot] = val
            for rho in range(rate):
                out_ref[p, rho * rows:(rho + 1) * rows, :] = (
                    perm_ref[slot, pl.ds(rho, rows, stride=rate), :].astype(_BF))

    def dilated(g, z):
        q2 = z[:, 0:256]
        k2 = z[:, 256:512]
        inv_q, inv_k = _seg_inv([q2, k2], bdr_ref, HEAD_DIM)
        qn2, kn2 = q2 * inv_q, k2 * inv_k
        for pp in range(2):
            p = 2 * g + pp
            sl = slice(pp * LANES, (pp + 1) * LANES)
            qx = qn2[:, sl] * gdq_ref[...]
            qx = qx * tdc_ref[...] + pltpu.roll(qx, 64, 1) * tds_ref[...]
            kx = kn2[:, sl] * gdk_ref[...]
            kx = kx * tdc_ref[...] + pltpu.roll(kx, 64, 1) * tds_ref[...]
            put(qd_ref, p, qx, DIL_RATES[g], 3 * pp)
            put(kd_ref, p, kx, DIL_RATES[g], 3 * pp + 1)
            put(vd_ref, p, z[:, 512 + pp * LANES: 512 + (pp + 1) * LANES], DIL_RATES[g], 3 * pp + 2)

    dil_lo = [COL_DIL + g * 768 for g in range(DIL_GROUPS)]
    stages = [(dil_lo[2], dil_lo[2] + 768, functools.partial(dilated, 2)),
              (dil_lo[1], dil_lo[1] + 768, functools.partial(dilated, 1)),
              (dil_lo[0], dil_lo[0] + 768, functools.partial(dilated, 0)),
              (COL_MLA, COL_NA, latent),
              (COL_NA, COL_DIL, neighbourhood)]
    z_next = project(stages[0][0], stages[0][1])
    for idx, (_, _, finish) in enumerate(stages):
        z = z_next
        if idx + 1 < len(stages):
            z_next = project(stages[idx + 1][0], stages[idx + 1][1])
        finish(z)


def _const_spec(shape):
    nd = len(shape)
    return pl.BlockSpec(shape, lambda *_, _nd=nd: (0,) * _nd)


def _layer_spec(arr, layer):
    nd = arr.ndim
    return pl.BlockSpec((None,) + arr.shape[1:], lambda *_, _l=layer, _nd=nd: (_l,) + (0,) * (_nd - 1))


def _mix_in(x, lw, layer, tabs, seq):
    t = x.shape[0]
    nblk_seq = seq // TM
    tab_spec = pl.BlockSpec((TM, LANES), lambda i: (i % nblk_seq, 0))
    weights = [lw[n] for n in ('g_mix', 'w_in', 'g_cq', 'g_ckv', 'w_uq', 'w_uk', 'w_uv',
                               'g_mq', 'g_mk', 'g_kr', 'g_nq', 'g_nk', 'g_dq', 'g_dk')]
    joins = _head_join_matrices()
    tables = [tabs['q_c'], tabs['q_s1'], tabs['q_s2'], tabs['k_c'], tabs['k_s'], tabs['d_c'], tabs['d_s']]

    def out(n):
        return (jax.ShapeDtypeStruct((n, t, LANES), _BF), pl.BlockSpec((n, TM, LANES), lambda i: (0, i, 0)))

    vt_out = (jax.ShapeDtypeStruct((MLA_HEADS, t // TM, LANES, TM), _BF),
              pl.BlockSpec((MLA_HEADS, 1, LANES, TM), lambda i: (0, i, 0, 0)))
    outs = [out(6), out(6), vt_out, out(3), out(3), out(3), out(6), out(6), out(6)]
    return pl.pallas_call(
        _mix_in_kernel,
        grid=(t // TM,),
        in_specs=[pl.BlockSpec((TM, D_MODEL), lambda i: (i, 0))]
        + [_layer_spec(w, layer) for w in weights] + [_const_spec(m.shape) for m in joins]
        + [tab_spec] * len(tables),
        out_specs=[o[1] for o in outs],
        out_shape=[o[0] for o in outs],
        scratch_shapes=[pltpu.VMEM((6, TM, LANES), _F32)],
        compiler_params=pltpu.CompilerParams(dimension_semantics=("parallel",), vmem_limit_bytes=VMEM_LIMIT),
        name="mix_in",
    )(x, *weights, *joins, *tables)


def _mla_kernel(q_ref, k_ref, vt_ref, o_ref, sa_sc, sb_sc, m_sc, acc_sc, *, tk):
    tq = q_ref.shape[1]
    nk = k_ref.shape[1] // tk
    sub = tk // TM
    m_sc[...] = jnp.full(m_sc.shape, NEG, _F32)
    acc_sc[...] = jnp.zeros(acc_sc.shape, _F32)

    def scores(j, dst):
        rows = pl.ds(pl.multiple_of(j * tk, tk), tk)
        for hh in range(2):
            dst[hh] = _dot_t(k_ref[hh, rows, :], q_ref[hh])

    def consume(j, src):
        for hh in range(2):
            s = src[hh]
            m_prev = m_sc[hh]
            m_new = jnp.maximum(m_prev, jnp.max(s, axis=0, keepdims=True))
            p = jnp.exp2(s - m_new).astype(_BF)
            vt = jnp.concatenate([vt_ref[hh, j * sub + c] for c in range(sub)], axis=1)
            acc_sc[hh] = jnp.exp2(m_prev - m_new) * acc_sc[hh] + _dot(vt, p)
            m_sc[hh] = m_new

    def pair(i, carry):
        j = 2 * i
        scores(j + 1, sb_sc)
        consume(j, sa_sc)
        scores(j + 2, sa_sc)
        consume(j + 1, sb_sc)
        return carry

    scores(0, sa_sc)
    lax.fori_loop(0, nk // 2 - 1, pair, 0)
    scores(nk - 1, sb_sc)
    consume(nk - 2, sa_sc)
    consume(nk - 1, sb_sc)

    lane = _lane((tq, LANES))
    outs = []
    for hh in range(2):
        acc = acc_sc[hh]
        outs.append((acc / acc[MLA_V:MLA_V + 1, :]).T)
    o_ref[0] = jnp.where(lane < MLA_V, outs[0], pltpu.roll(outs[1], MLA_V, 1)).astype(o_ref.dtype)


def _mla(q, k, vt, batch, seq, tq=2048, tk=512):
    t = q.shape[1]
    nq = seq // tq
    nblk = seq // TM
    assert seq % (2 * tk) == 0 and tk % TM == 0
    return pl.pallas_call(
        functools.partial(_mla_kernel, tk=tk),
        grid=(batch, MLA_HEADS // 2, nq),
        in_specs=[pl.BlockSpec((2, tq, LANES), lambda b, p, i: (p, b * nq + i, 0)),
                  pl.BlockSpec((2, seq, LANES), lambda b, p, i: (p, b, 0)),
                  pl.BlockSpec((2, nblk, LANES, TM), lambda b, p, i: (p, b, 0, 0))],
        out_specs=pl.BlockSpec((1, tq, LANES), lambda b, p, i: (p, b * nq + i, 0)),
        out_shape=jax.ShapeDtypeStruct((MLA_HEADS // 2, t, LANES), _BF),
        scratch_shapes=[pltpu.VMEM((2, tk, tq), _F32), pltpu.VMEM((2, tk, tq), _F32),
                        pltpu.VMEM((2, 1, tq), _F32), pltpu.VMEM((2, LANES, tq), _F32)],
        compiler_params=pltpu.CompilerParams(
            dimension_semantics=("parallel", "parallel", "arbitrary"), vmem_limit_bytes=VMEM_LIMIT),
        name="mla_attention",
    )(q, k, vt)


def _na_kernel(q_ref, k_ref, v_ref, bias_ref, o_ref, *, rows):
    i = pl.program_id(2)
    lane = _lane((GRID_W, LANES))
    first = lane < HEAD_DIM
    nkeys = NA_WIN_R * GRID_W

    def score(rr):
        r = i * NA_RB + rr
        rs = jnp.clip(r - NA_WIN_R // 2, 0, rows - NA_WIN_R)
        start = pl.multiple_of(rs * GRID_W, GRID_W)
        q = q_ref[0, rr * GRID_W:(rr + 1) * GRID_W, :]
        zero = jnp.zeros_like(q)
        q2 = jnp.concatenate([jnp.where(first, q, zero), jnp.where(first, zero, q)], axis=0)
        return _dot_t(q2, k_ref[0, pl.ds(start, nkeys), :]) + bias_ref[0, r - rs], start

    def softmax(s):
        p = jnp.exp(s - jnp.max(s, axis=1, keepdims=True))
        return p.astype(_BF), jnp.sum(p, axis=1, keepdims=True)

    def value(rr, p, l, start):
        o2 = _dot(p, v_ref[0, pl.ds(start, nkeys), :]) / l
        o = jnp.where(first, o2[:GRID_W], o2[GRID_W:])
        o_ref[0, rr * GRID_W:(rr + 1) * GRID_W, :] = o.astype(o_ref.dtype)

    scored, normed = {}, {}
    for step in range(NA_RB + 2):
        if step < NA_RB:
            scored[step] = score(step)
        if 1 <= step <= NA_RB:
            s, start = scored.pop(step - 1)
            normed[step - 1] = softmax(s) + (start,)
        if step >= 2:
            value(step - 2, *normed.pop(step - 2))


def _na(q, k, v, bias, layer, batch, seq):
    t = q.shape[1]
    rows = seq // GRID_W
    assert rows >= NA_WIN_R and rows % NA_RB == 0
    nrb = rows // NA_RB
    qrows = NA_RB * GRID_W
    return pl.pallas_call(
        functools.partial(_na_kernel, rows=rows),
        grid=(batch, NA_HEADS // 2, nrb),
        in_specs=[pl.BlockSpec((1, qrows, LANES), lambda b, p, i: (p, b * nrb + i, 0)),
                  pl.BlockSpec((1, seq, LANES), lambda b, p, i: (p, b, 0)),
                  pl.BlockSpec((1, seq, LANES), lambda b, p, i: (p, b, 0)),
                  pl.BlockSpec((None, 1, NA_WIN_R, 2 * GRID_W, NA_WIN_R * GRID_W),
                               lambda b, p, i: (layer, p, 0, 0, 0))],
        out_specs=pl.BlockSpec((1, qrows, LANES), lambda b, p, i: (p, b * nrb + i, 0)),
        out_shape=jax.ShapeDtypeStruct((NA_HEADS // 2, t, LANES), _BF),
        compiler_params=pltpu.CompilerParams(
            dimension_semantics=("parallel", "parallel", "arbitrary"), vmem_limit_bytes=VMEM_LIMIT),
        name="na_attention",
    )(q, k, v, bias)


def _dil_kernel(q0_ref, q1_ref, q2_ref, k0_ref, k1_ref, k2_ref, v0_ref, v1_ref, v2_ref, band_ref,
                o_ref, og_sc, lse_sc, *, seq):
    sb = pl.program_id(2)
    q_refs = (q0_ref, q1_ref, q2_ref)
    k_refs = (k0_ref, k1_ref, k2_ref)
    v_refs = (v0_ref, v1_ref, v2_ref)
    lane = _lane((DIL_TQ, LANES))
    first_qk = (lane % HEAD_DIM) < (HEAD_DIM // 2)
    first_v = lane < HEAD_DIM
    n_tiles = DIL_SB // DIL_TQ
    half = DIL_TQ // 2

    for g in range(DIL_GROUPS):
        rate = DIL_RATES[g]
        length = seq // rate
        per_blk = TM // rate
        chunk = min(half, per_blk)
        n_chunks = (2 * DIL_TQ) // chunk
        tiles_per_res = n_tiles // rate
        q_ref, k_ref, v_ref = q_refs[g], k_refs[g], v_refs[g]

        def row_of(m, rho, _per_blk=per_blk):
            return (m // _per_blk) * TM + rho * _per_blk + (m % _per_blk)

        def body(tg, carry, _g=g, _rate=rate, _length=length, _per_blk=per_blk, _chunk=chunk,
                 _n_chunks=n_chunks, _tpr=tiles_per_res, _q=q_ref, _k=k_ref, _v=v_ref):
            def score(u):
                tt = tg * DIL_UNROLL + u
                rho = tt // _tpr
                mt = tt % _tpr
                m0 = sb * (DIL_SB // _rate) + mt * DIL_TQ
                q = jnp.concatenate(
                    [_q[0, pl.ds(pl.multiple_of(row_of(m0 + c * _chunk, rho), _chunk), _chunk), :]
                     for c in range(DIL_TQ // _chunk)], axis=0)
                zero = jnp.zeros_like(q)
                q2 = jnp.concatenate([jnp.where(first_qk, q, zero), jnp.where(first_qk, zero, q)], axis=0)
                win = []
                for c in range(_n_chunks):
                    ms = jnp.clip(m0 - half + c * _chunk, 0, _length - _chunk)
                    win.append(pl.multiple_of(row_of(ms, rho), _chunk))
                kw = jnp.concatenate([_k[0, pl.ds(row, _chunk), :] for row in win], axis=0)
                edge = (m0 == 0).astype(jnp.int32) + 2 * (m0 == _length - DIL_TQ).astype(jnp.int32)
                return _dot_t(q2, kw) + band_ref[edge], rho + _rate * mt * DIL_TQ, win

            def softmax(s):
                m = jnp.max(s, axis=1, keepdims=True)
                p = jnp.exp(s - m)
                return p.astype(_BF), m, jnp.sum(p, axis=1, keepdims=True)

            def value(p, m, l, base, win):
                vw = jnp.concatenate([_v[0, pl.ds(row, _chunk), :] for row in win], axis=0)
                o2 = _dot(p, vw) / l
                lse2 = jnp.broadcast_to(m + jnp.log(l), (2 * DIL_TQ, LANES))
                if _rate == 1:
                    dst = pl.ds(pl.multiple_of(base, DIL_TQ), DIL_TQ)
                else:
                    dst = pl.ds(base, DIL_TQ, stride=_rate)
                og_sc[_g, dst, :] = jnp.where(first_v, o2[:DIL_TQ], o2[DIL_TQ:])
                lse_sc[_g, dst, :] = jnp.where(first_v, lse2[:DIL_TQ], lse2[DIL_TQ:])

            scored, normed = {}, {}
            for step in range(DIL_UNROLL + 2):
                if step < DIL_UNROLL:
                    scored[step] = score(step)
                if 1 <= step <= DIL_UNROLL:
                    s, base, win = scored.pop(step - 1)
                    normed[step - 1] = softmax(s) + (base, win)
                if step >= 2:
                    value(*normed.pop(step - 2))
            return carry

        lax.fori_loop(0, n_tiles // DIL_UNROLL, body, 0)

    m = jnp.maximum(jnp.maximum(lse_sc[0], lse_sc[1]), lse_sc[2])
    w0 = jnp.exp(lse_sc[0] - m)
    w1 = jnp.exp(lse_sc[1] - m)
    w2 = jnp.exp(lse_sc[2] - m)
    o = (w0 * og_sc[0] + w1 * og_sc[1] + w2 * og_sc[2]) / (w0 + w1 + w2)
    o_ref[0] = o.astype(o_ref.dtype)


def _dil(q, k, v, band, batch, seq):
    t = q.shape[1]
    assert seq % DIL_SB == 0
    nsb = seq // DIL_SB

    def q_spec(g):
        return pl.BlockSpec((1, DIL_SB, LANES), lambda b, sp, s, _g=g: (2 * _g + sp, b * nsb + s, 0))

    def kv_spec(g):
        return pl.BlockSpec((1, seq, LANES), lambda b, sp, s, _g=g: (2 * _g + sp, b, 0))

    return pl.pallas_call(
        functools.partial(_dil_kernel, seq=seq),
        grid=(batch, 2, nsb),
        in_specs=[kv_spec(0), kv_spec(1), kv_spec(2), kv_spec(0), kv_spec(1), kv_spec(2),
                  kv_spec(0), kv_spec(1), kv_spec(2),
                  pl.BlockSpec((4, 2 * DIL_TQ, 2 * DIL_TQ), lambda b, sp, s: (0, 0, 0))],
        out_specs=pl.BlockSpec((1, DIL_SB, LANES), lambda b, sp, s: (sp, b * nsb + s, 0)),
        out_shape=jax.ShapeDtypeStruct((2, t, LANES), _BF),
        scratch_shapes=[pltpu.VMEM((DIL_GROUPS, DIL_SB, LANES), _F32),
                        pltpu.VMEM((DIL_GROUPS, DIL_SB, LANES), _F32)],
        compiler_params=pltpu.CompilerParams(
            dimension_semantics=("parallel", "parallel", "arbitrary"), vmem_limit_bytes=VMEM_LIMIT),
        name="dil_attention",
    )(q, q, q, k, k, k, v, v, v, band)


def _mem_kv_kernel(mem_ref, gmem_ref, wckv_ref, gxk_ref, k_ref, v_ref):
    hm = _rms(mem_ref[0], gmem_ref[...]).astype(_BF)
    kv = _dot(hm, wckv_ref[...])
    for hd in range(X_HEADS):
        kh = kv[:, hd * X_HEAD_DIM:(hd + 1) * X_HEAD_DIM]
        k_ref[0, :, hd * X_HEAD_DIM:(hd + 1) * X_HEAD_DIM] = _rms(kh, gxk_ref[...]).astype(_BF)
    v_ref[0] = kv[:, D_MODEL:].astype(_BF)


def _mem_kv(mem, lw, layer):
    b, m, _ = mem.shape
    return pl.pallas_call(
        _mem_kv_kernel,
        grid=(b,),
        in_specs=[pl.BlockSpec((1, m, D_MODEL), lambda i: (i, 0, 0))]
        + [_layer_spec(lw[n], layer) for n in ('g_mem', 'w_ckv', 'g_xk')],
        out_specs=[pl.BlockSpec((1, m, D_MODEL), lambda i: (i, 0, 0))] * 2,
        out_shape=[jax.ShapeDtypeStruct((b, m, D_MODEL), _BF)] * 2,
        compiler_params=pltpu.CompilerParams(dimension_semantics=("parallel",), vmem_limit_bytes=VMEM_LIMIT),
        name="mem_kv",
    )(mem, lw['g_mem'], lw['w_ckv'], lw['g_xk'])


def _mix_out_kernel(x_ref, oa_ref, ob_ref, oc_ref, wo_ref, gcross_ref, wcq_ref, gxq_ref,
                    mk_ref, mv_ref, wco_ref, y_ref):
    mix = jnp.concatenate([oa_ref[p] for p in range(3)] + [ob_ref[p] for p in range(3)]
                          + [oc_ref[p] for p in range(2)], axis=1)
    x1 = x_ref[...] + _dot(mix, wo_ref[...])
    inv_x = lax.rsqrt(jnp.mean(x1 * x1, axis=-1, keepdims=True) + EPS)
    q = _dot((x1 * gcross_ref[...]).astype(_BF), wcq_ref[...]) * inv_x
    heads = [slice(hd * X_HEAD_DIM, (hd + 1) * X_HEAD_DIM) for hd in range(X_HEADS)]
    scores = [_dot_t((_rms(q[:, sl], gxq_ref[...]) * X_HEAD_DIM ** -0.5).astype(_BF), mk_ref[0, :, sl])
              for sl in heads]
    probs = []
    for s in scores:
        p = jnp.exp(s - jnp.max(s, axis=1, keepdims=True))
        probs.append((p.astype(_BF), jnp.sum(p, axis=1, keepdims=True)))
    outs = [(_dot(p, mv_ref[0, :, sl]) / l).astype(_BF) for (p, l), sl in zip(probs, heads)]
    o = jnp.concatenate(outs, axis=1)
    y_ref[...] = x1 + _dot(o, wco_ref[...])


def _mix_out(x, oa, ob, oc, mk, mv, lw, layer, seq):
    t = x.shape[0]
    nblk_seq = seq // TM
    mem_len = mk.shape[1]

    def grp(n):
        return pl.BlockSpec((n, TM, LANES), lambda i: (0, i, 0))

    mem_spec = pl.BlockSpec((1, mem_len, D_MODEL), lambda i: (i // nblk_seq, 0, 0))
    return pl.pallas_call(
        _mix_out_kernel,
        grid=(t // TM,),
        in_specs=[pl.BlockSpec((TM, D_MODEL), lambda i: (i, 0)), grp(3), grp(3), grp(2),
                  _layer_spec(lw['w_o'], layer), _layer_spec(lw['g_cross'], layer),
                  _layer_spec(lw['w_cq'], layer), _layer_spec(lw['g_xq'], layer),
                  mem_spec, mem_spec, _layer_spec(lw['w_co'], layer)],
        out_specs=pl.BlockSpec((TM, D_MODEL), lambda i: (i, 0)),
        out_shape=jax.ShapeDtypeStruct((t, D_MODEL), _F32),
        compiler_params=pltpu.CompilerParams(dimension_semantics=("parallel",), vmem_limit_bytes=VMEM_LIMIT),
        name="mix_out",
    )(x, oa, ob, oc, lw['w_o'], lw['g_cross'], lw['w_cq'], lw['g_xq'], mk, mv, lw['w_co'])


def _ffn_kernel(x_ref, xp_ref, xn_ref, gffn_ref, wup_ref, cw_ref, cb_ref, wdn_ref, y_ref, *, nblk_seq):
    i = pl.program_id(0)
    tm = x_ref.shape[0]
    x = x_ref[...]
    g = gffn_ref[...]
    is_first = (i % nblk_seq) == 0
    is_last = (i % nblk_seq) == nblk_seq - 1
    hp = jnp.where(is_first, 0.0, _rms(xp_ref[...], g))
    hn = jnp.where(is_last, 0.0, _rms(xn_ref[...], g))
    h = jnp.concatenate([hp, _rms(x, g), hn], axis=0).astype(_BF)
    def up_proj(c):
        return [_dot(h, wup_ref[:, part * D_FF + c * FF_CH: part * D_FF + (c + 1) * FF_CH]) for part in range(2)]

    def conv(u, lo):
        cw = cw_ref[:, lo:lo + FF_CH]
        up = pltpu.roll(u, 1, 0)[8:8 + tm]
        un = pltpu.roll(u, tm + 15, 0)[8:8 + tm]
        return up * cw[0:1] + u[8:8 + tm] * cw[1:2] + un * cw[2:3] + cb_ref[:, lo:lo + FF_CH]

    acts = []
    u_next = up_proj(0)
    for c in range(N_FF_CH):
        u_val, u_gate = u_next
        if c + 1 < N_FF_CH:
            u_next = up_proj(c + 1)
        a = conv(u_val, c * FF_CH)
        half_gate = conv(u_gate, D_FF + c * FF_CH)
        acts.append((half_gate * (1.0 + jnp.tanh(half_gate)) * a).astype(_BF))
    y_ref[...] = x + _dot(jnp.concatenate(acts, axis=1), wdn_ref[...])


def _ffn(x, lw, layer, seq):
    t = x.shape[0]
    nblk_seq = seq // TM
    nhalo = t // 8
    per = TM // 8
    return pl.pallas_call(
        functools.partial(_ffn_kernel, nblk_seq=nblk_seq),
        grid=(t // TM,),
        in_specs=[pl.BlockSpec((TM, D_MODEL), lambda i: (i, 0)),
                  pl.BlockSpec((8, D_MODEL), lambda i: (jnp.maximum(i * per - 1, 0), 0)),
                  pl.BlockSpec((8, D_MODEL), lambda i: (jnp.minimum((i + 1) * per, nhalo - 1), 0))]
        + [_layer_spec(lw[n], layer) for n in ('g_ffn', 'w_up', 'conv_w', 'conv_b', 'w_down')],
        out_specs=pl.BlockSpec((TM, D_MODEL), lambda i: (i, 0)),
        out_shape=jax.ShapeDtypeStruct((t, D_MODEL), _F32),
        compiler_params=pltpu.CompilerParams(dimension_semantics=("parallel",), vmem_limit_bytes=VMEM_LIMIT),
        name="conv_ffn",
    )(x, x, x, lw['g_ffn'], lw['w_up'], lw['conv_w'], lw['conv_b'], lw['w_down'])


def _relayout_w_in(w):
    nl = w.shape[0]
    lead = w[..., :SPLIT_CKV]
    a = w[..., SPLIT_CKV:SPLIT_CKV + 16]
    b = w[..., SPLIT_CKV + 16:SPLIT_KR]
    z32 = jnp.zeros(a.shape[:-1] + (32,), w.dtype)
    kr = jnp.concatenate([b, a, z32, a, b, z32], axis=-1)
    na = w[..., SPLIT_KR:SPLIT_NA].reshape(nl, D_MODEL, 3, NA_HEADS // 2, LANES)
    na_ab = na[:, :, :, 0:2].reshape(nl, D_MODEL, 3 * 2 * LANES)
    na_c = na[:, :, :, 2].reshape(nl, D_MODEL, 3 * LANES)
    dl = w[..., SPLIT_NA:].reshape(nl, D_MODEL, 3, DIL_GROUPS, 2, 2, 2, HEAD_DIM // 2)
    qk = dl[:, :, 0:2].transpose(0, 1, 3, 2, 4, 6, 5, 7).reshape(nl, D_MODEL, DIL_GROUPS, 4 * LANES)
    v = dl[:, :, 2].reshape(nl, D_MODEL, DIL_GROUPS, 2 * LANES)
    dil = jnp.concatenate([qk, v], axis=-1).reshape(nl, D_MODEL, DIL_GROUPS * 6 * LANES)
    out = jnp.concatenate([lead, kr, na_ab, na_c, dil], axis=-1)
    assert out.shape[-1] == D_IN_P
    return out


def _pad_heads(w, real, n_heads):
    nl, k, _ = w.shape
    w = w.reshape(nl, k, n_heads, real)
    return jnp.pad(w, ((0, 0), (0, 0), (0, 0), (0, LANES - real))).reshape(nl, k, n_heads * LANES)


def _rows(v):
    return v[:, None, :].astype(_F32)


def _prep_layers(p):
    nl = p['w_in'].shape[0]
    kn, qn = p['mla_kn'], p['mla_qn']
    z32 = jnp.zeros((nl, 32), _F32)
    z64 = jnp.zeros((nl, 64), _F32)
    dq, dk = p['dil_qn'], p['dil_kn']

    def rot(gain):
        return jnp.concatenate([gain[:, :32], gain[:, :32], gain[:, 32:], gain[:, 32:]], axis=-1)

    gate_half = jnp.concatenate([jnp.ones((D_FF,), _F32), jnp.full((D_FF,), 0.5, _F32)])

    return {
        'g_mix': _rows(p['norm_mix']), 'w_in': _relayout_w_in(p['w_in'].astype(_BF)),
        'g_cq': _rows(p['mla_q_norm']), 'g_ckv': _rows(p['mla_kv_norm']),
        'w_uq': _pad_heads(p['w_uq'].astype(_BF), MLA_QK, MLA_HEADS),
        'w_uk': _pad_heads(p['w_uk'].astype(_BF), MLA_NOPE, MLA_HEADS),
        'w_uv': _pad_heads(p['w_uv'].astype(_BF), MLA_V, MLA_HEADS),
        'g_mq': _rows(jnp.concatenate([qn, z32], axis=-1) * (MLA_QK ** -0.5 * LOG2E)),
        'g_mk': _rows(jnp.concatenate([kn[:, :MLA_NOPE], z64], axis=-1)),
        'g_kr': _rows(jnp.concatenate([kn[:, 80:96], kn[:, 64:80], z32, kn[:, 64:80], kn[:, 80:96], z32], axis=-1)),
        'g_nq': _rows(jnp.tile(p['na_qn'], (1, 2)) * HEAD_DIM ** -0.5), 'g_nk': _rows(jnp.tile(p['na_kn'], (1, 2))),
        'g_dq': _rows(rot(dq) * HEAD_DIM ** -0.5), 'g_dk': _rows(rot(dk)),
        'w_o': p['w_o'].astype(_BF),
        'g_cross': _rows(p['norm_cross']), 'w_cq': p['w_cq'].astype(_BF),
        'g_xq': _rows(p['x_qn']), 'g_xk': _rows(p['x_kn']), 'g_mem': _rows(p['norm_mem']),
        'w_ckv': p['w_ckv'].astype(_BF), 'w_co': p['w_co'].astype(_BF),
        'g_ffn': _rows(p['norm_ffn']), 'w_up': p['w_up'].astype(_BF),
        'conv_w': p['conv_w'] * gate_half, 'conv_b': _rows(p['conv_b'] * gate_half),
        'w_down': p['w_down'].astype(_BF),
        'na_bias': _na_bias(p['na_rpb']),
    }


def _na_bias(rpb):
    nl = rpb.shape[0]
    dl = np.arange(NA_WIN_R)
    j = np.arange(NA_WIN_R)
    r_off = j[None, :] - dl[:, None] + (NA_WIN_R - 1)
    qc = np.arange(GRID_W)
    kc = np.arange(GRID_W)
    c_start = np.clip(qc - NA_WIN_C // 2, 0, GRID_W - NA_WIN_C)
    valid = (kc[None, :] >= c_start[:, None]) & (kc[None, :] < c_start[:, None] + NA_WIN_C)
    c_off = kc[None, :] - qc[:, None] + (NA_WIN_C - 1)
    pick = (np.arange(2 * NA_WIN_C - 1)[:, None, None] == c_off[None]) & valid[None]
    rows = rpb[:, :, r_off].reshape(nl, NA_HEADS // 2, 2, NA_WIN_R, NA_WIN_R, 2 * NA_WIN_C - 1)
    b = jnp.einsum('lpedjc,cqk->lpdeqjk', rows, jnp.asarray(pick, _F32), precision=lax.Precision.HIGHEST)
    b = b + jnp.asarray(np.where(valid, 0.0, NEG), _F32)[:, None, :]
    return b.reshape(nl, NA_HEADS // 2, NA_WIN_R, 2 * GRID_W, NA_WIN_R * GRID_W)


def _band_bias():
    i = np.arange(DIL_TQ)[:, None]
    c = np.arange(2 * DIL_TQ)[None, :]
    band = (c >= i) & (c <= i + DIL_TQ)
    out = []
    for edge in range(4):
        ok = band
        if edge & 1:
            ok = ok & (c >= DIL_TQ // 2)
        if edge & 2:
            ok = ok & (c < 2 * DIL_TQ - DIL_TQ // 2)
        out.append(np.tile(np.where(ok, 0.0, NEG), (2, 1)))
    return jnp.asarray(np.stack(out), _F32)


def _rope_tables(seq):
    pos = jnp.arange(seq, dtype=_F32)[:, None]

    def cs(half):
        inv = ROPE_THETA ** (-jnp.arange(half, dtype=_F32) / half)
        ang = pos * inv[None, :]
        return jnp.cos(ang), jnp.sin(ang)

    c16, s16 = cs(MLA_ROPE // 2)
    c32, s32 = cs(HEAD_DIM // 2)
    one = jnp.ones((seq, 64), _F32)
    z16 = jnp.zeros((seq, 16), _F32)
    z32 = jnp.zeros((seq, 32), _F32)
    z64 = jnp.zeros((seq, 64), _F32)
    return {
        'q_c': jnp.concatenate([one, c16, c16, z32], axis=1),
        'q_s1': jnp.concatenate([z64, z16, s16, z32], axis=1),
        'q_s2': jnp.concatenate([z64, -s16, z16, z32], axis=1),
        'k_c': jnp.concatenate([z64, c16, c16, z32], axis=1),
        'k_s': jnp.concatenate([z64, -s16, s16, z32], axis=1),
        'd_c': jnp.concatenate([c32, c32, c32, c32], axis=1),
        'd_s': jnp.concatenate([-s32, -s32, s32, s32], axis=1),
    }


def _trunk_layer(x, mem, lw, layer, tabs, band, batch, seq):
    qm, km, vm, qn, kn, vn, qd, kd, vd = _mix_in(x, lw, layer, tabs, seq)
    oa = _mla(qm, km, vm, batch, seq)
    ob = _na(qn, kn, vn, lw['na_bias'], layer, batch, seq)
    oc = _dil(qd, kd, vd, band, batch, seq)
    mk, mv = _mem_kv(mem, lw, layer)
    x = _mix_out(x, oa, ob, oc, mk, mv, lw, layer, seq)
    return _ffn(x, lw, layer, seq)


def kernel(x_prompt, x_sample, mem_prompt, mem_sample, norm_mix, w_in, mla_q_norm, mla_kv_norm, w_uq, w_uk, w_uv, mla_qn, mla_kn, na_qn, na_kn, na_rpb, dil_qn, dil_kn, w_o, norm_cross, norm_mem, w_cq, w_ckv, x_qn, x_kn, w_co, norm_ffn, w_up, conv_w, conv_b, w_down):
    stacked = {
        'norm_mix': norm_mix, 'w_in': w_in, 'mla_q_norm': mla_q_norm, 'mla_kv_norm': mla_kv_norm,
        'w_uq': w_uq, 'w_uk': w_uk, 'w_uv': w_uv, 'mla_qn': mla_qn, 'mla_kn': mla_kn,
        'na_qn': na_qn, 'na_kn': na_kn, 'na_rpb': na_rpb, 'dil_qn': dil_qn, 'dil_kn': dil_kn,
        'w_o': w_o, 'norm_cross': norm_cross, 'norm_mem': norm_mem, 'w_cq': w_cq, 'w_ckv': w_ckv,
        'x_qn': x_qn, 'x_kn': x_kn, 'w_co': w_co, 'norm_ffn': norm_ffn, 'w_up': w_up,
        'conv_w': conv_w, 'conv_b': conv_b, 'w_down': w_down,
    }
    lw = _prep_layers(stacked)
    band = _band_bias()
    outs = []
    for x, mem in ((x_prompt, mem_prompt), (x_sample, mem_sample)):
        batch, seq, _ = x.shape
        tabs = _rope_tables(seq)
        xt = x.reshape(batch * seq, D_MODEL)
        for layer in range(DEPTH):
            xt = _trunk_layer(xt, mem, lw, layer, tabs, band, batch, seq)
        outs.append(xt.reshape(batch, seq, D_MODEL))
    return tuple(outs)
```

```python
import functools

import jax
import jax.numpy as jnp
import numpy as np
from jax import lax
from jax.experimental import pallas as pl
from jax.experimental.pallas import tpu as pltpu

D_MODEL = 1024
DEPTH = 4
GRID_W = 64
HEAD_DIM = 64
ROPE_THETA = 10000.0
EPS = 1e-6
MLA_HEADS = 6
MLA_Q_RANK = 256
MLA_KV_RANK = 128
MLA_NOPE = 64
MLA_ROPE = 32
MLA_V = 64
MLA_QK = MLA_NOPE + MLA_ROPE
NA_HEADS = 6
NA_WIN_R = 8
NA_WIN_C = 16
DIL_WINDOWS = (128, 512, 2048)
DIL_RATES = (1, 4, 16)
DIL_GROUPS = 3
DIL_HEADS = 4
DIL_ALL = DIL_GROUPS * DIL_HEADS
X_HEADS = 4
X_HEAD_DIM = D_MODEL // X_HEADS
D_FF = 2816
CONV_W = 3
SPLIT_CQ = MLA_Q_RANK
SPLIT_CKV = SPLIT_CQ + MLA_KV_RANK
SPLIT_KR = SPLIT_CKV + MLA_ROPE
SPLIT_NA = SPLIT_KR + 3 * NA_HEADS * HEAD_DIM
D_IN = SPLIT_NA + 3 * DIL_ALL * HEAD_DIM

LANES = 128
NEG = -1e30
LOG2E = 1.4426950408889634
TM = 512
DIL_TQ = 128
DIL_SB = DIL_TQ * max(DIL_RATES)
DIL_UNROLL = 16
NA_RB = 32
FF_CH = 256
N_FF_CH = D_FF // FF_CH
VMEM_LIMIT = 56 * 1024 * 1024

COL_MLA = 0
COL_NA = 512
COL_DIL = COL_NA + 3 * NA_HEADS * HEAD_DIM
D_IN_P = COL_DIL + 3 * DIL_ALL * HEAD_DIM

_BF = jnp.bfloat16
_F32 = jnp.float32


def _dot(a, b):
    return jnp.dot(a, b, preferred_element_type=_F32)


def _dot_t(a, b):
    return lax.dot_general(a, b, (((1,), (1,)), ((), ())), preferred_element_type=_F32)


def _rms(x, g):
    ms = jnp.mean(x * x, axis=-1, keepdims=True)
    return x * lax.rsqrt(ms + EPS) * g


def _lane(shape):
    return lax.broadcasted_iota(jnp.int32, shape, len(shape) - 1)


def _head_join_matrices():
    i = np.arange(2 * LANES)
    same_group = (i[:, None] // LANES) == (i[None, :] // LANES)
    natural = (i[:, None] // HEAD_DIM) == (i[None, :] // HEAD_DIM)
    half = ((i % HEAD_DIM) < HEAD_DIM // 2)
    rotary = same_group & (half[:, None] == half[None, :])
    return [jnp.asarray(m, _BF) for m in (same_group, natural, rotary)]


def _seg_inv(slabs, bd_ref, dim):
    rows = slabs[0].shape[0]
    sq = jnp.concatenate([(x2 * x2).astype(_BF) for x2 in slabs], axis=0)
    inv = lax.rsqrt(_dot(sq, bd_ref[...]) * (1.0 / dim) + EPS)
    return [inv[n * rows:(n + 1) * rows] for n in range(len(slabs))]


def _mix_in_kernel(x_ref, gmix_ref, win_ref, gcq_ref, gckv_ref, wuq_ref, wuk_ref, wuv_ref,
                   gmq_ref, gmk_ref, gkr_ref, gnq_ref, gnk_ref, gdq_ref, gdk_ref,
                   bd128_ref, bd64_ref, bdr_ref,
                   tqc_ref, tqs1_ref, tqs2_ref, tkc_ref, tks_ref, tdc_ref, tds_ref,
                   qm_ref, km_ref, vm_ref, qn_ref, kn_ref, vn_ref, qd_ref, kd_ref, vd_ref,
                   perm_ref):
    tm = x_ref.shape[0]
    x = x_ref[...]
    xg = (x * gmix_ref[...]).astype(_BF)
    inv_x = lax.rsqrt(jnp.mean(x * x, axis=-1, keepdims=True) + EPS)
    lane = _lane((tm, LANES))

    def project(lo, hi):
        return _dot(xg, win_ref[:, lo:hi]) * inv_x

    def latent(z):
        cq = _rms(z[:, 0:MLA_Q_RANK], gcq_ref[...]).astype(_BF)
        ckv = _rms(z[:, MLA_Q_RANK:MLA_Q_RANK + MLA_KV_RANK], gckv_ref[...]).astype(_BF)
        kr = z[:, 384:512]
        kr_own = jnp.where((lane >= MLA_NOPE) & (lane < MLA_QK), kr, 0.0)
        krg = kr * gkr_ref[...]
        kr_rot = krg * tkc_ref[...] + pltpu.roll(krg, 64, 1) * tks_ref[...]
        qpre = _dot(cq, wuq_ref[...])
        kpre = _dot(ckv, wuk_ref[...])
        vpre = _dot(ckv, wuv_ref[...])
        row_t = lax.broadcasted_iota(jnp.int32, (LANES, tm), 0)
        npair = MLA_HEADS // 2
        q_pairs = [qpre[:, 2 * c * LANES:(2 * c + 2) * LANES] for c in range(npair)]
        k_pairs = [kpre[:, 2 * c * LANES:(2 * c + 2) * LANES] for c in range(npair)]
        kr2 = jnp.concatenate([kr_own, kr_own], axis=1)
        invs = _seg_inv(q_pairs + [k2 + kr2 for k2 in k_pairs], bd128_ref, MLA_QK)
        for c in range(npair):
            q2, k2, inv_q, inv_k = q_pairs[c], k_pairs[c], invs[c], invs[npair + c]
            for hh in range(2):
                hd = 2 * c + hh
                sl = slice(hh * LANES, (hh + 1) * LANES)
                qg = q2[:, sl] * gmq_ref[...]
                qr = (qg * tqc_ref[...] + pltpu.roll(qg, 16, 1) * tqs1_ref[...]
                      + pltpu.roll(qg, 112, 1) * tqs2_ref[...])
                qm_ref[hd] = (qr * inv_q[:, sl]).astype(_BF)
                km_ref[hd] = ((k2[:, sl] * gmk_ref[...] + kr_rot) * inv_k[:, sl]).astype(_BF)
                vt = vpre[:, hd * LANES:(hd + 1) * LANES].T
                vm_ref[hd, 0] = jnp.where(row_t == MLA_V, 1.0, vt).astype(_BF)

    def neighbourhood(z):
        q2 = z[:, 0:256]
        k2 = z[:, 256:512]
        qk2 = z[:, 768:1024]
        inv_q, inv_k, inv_qk = _seg_inv([q2, k2, qk2], bd64_ref, HEAD_DIM)
        qn2, kn2, qkn2 = q2 * inv_q, k2 * inv_k, qk2 * inv_qk
        for p in range(2):
            sl = slice(p * LANES, (p + 1) * LANES)
            qn_ref[p] = (qn2[:, sl] * gnq_ref[...]).astype(_BF)
            kn_ref[p] = (kn2[:, sl] * gnk_ref[...]).astype(_BF)
            vn_ref[p] = z[:, 512 + p * LANES: 512 + (p + 1) * LANES].astype(_BF)
        qn_ref[2] = (qkn2[:, :LANES] * gnq_ref[...]).astype(_BF)
        kn_ref[2] = (qkn2[:, LANES:] * gnk_ref[...]).astype(_BF)
        vn_ref[2] = z[:, 1024:1152].astype(_BF)

    def put(out_ref, p, val, rate, slot):
        if rate == 1:
            out_ref[p] = val.astype(_BF)
        else:
            rows = tm // rate
            perm_ref[slot] = val
            for rho in range(rate):
                out_ref[p, rho * rows:(rho + 1) * rows, :] = (
                    perm_ref[slot, pl.ds(rho, rows, stride=rate), :].astype(_BF))

    def dilated(g, z):
        q2 = z[:, 0:256]
        k2 = z[:, 256:512]
        inv_q, inv_k = _seg_inv([q2, k2], bdr_ref, HEAD_DIM)
        qn2, kn2 = q2 * inv_q, k2 * inv_k
        for pp in range(2):
            p = 2 * g + pp
            sl = slice(pp * LANES, (pp + 1) * LANES)
            qx = qn2[:, sl] * gdq_ref[...]
            qx = qx * tdc_ref[...] + pltpu.roll(qx, 64, 1) * tds_ref[...]
            kx = kn2[:, sl] * gdk_ref[...]
            kx = kx * tdc_ref[...] + pltpu.roll(kx, 64, 1) * tds_ref[...]
            put(qd_ref, p, qx, DIL_RATES[g], 3 * pp)
            put(kd_ref, p, kx, DIL_RATES[g], 3 * pp + 1)
            put(vd_ref, p, z[:, 512 + pp * LANES: 512 + (pp + 1) * LANES], DIL_RATES[g], 3 * pp + 2)

    dil_lo = [COL_DIL + g * 768 for g in range(DIL_GROUPS)]
    stages = [(dil_lo[2], dil_lo[2] + 768, functools.partial(dilated, 2)),
              (dil_lo[1], dil_lo[1] + 768, functools.partial(dilated, 1)),
              (dil_lo[0], dil_lo[0] + 768, functools.partial(dilated, 0)),
              (COL_MLA, COL_NA, latent),
              (COL_NA, COL_DIL, neighbourhood)]
    z_next = project(stages[0][0], stages[0][1])
    for idx, (_, _, finish) in enumerate(stages):
        z = z_next
        if idx + 1 < len(stages):
            z_next = project(stages[idx + 1][0], stages[idx + 1][1])
        finish(z)


def _const_spec(shape):
    nd = len(shape)
    return pl.BlockSpec(shape, lambda *_, _nd=nd: (0,) * _nd)


def _layer_spec(arr, layer):
    nd = arr.ndim
    return pl.BlockSpec((None,) + arr.shape[1:], lambda *_, _l=layer, _nd=nd: (_l,) + (0,) * (_nd - 1))


def _mix_in(x, lw, layer, tabs, seq):
    t = x.shape[0]
    nblk_seq = seq // TM
    tab_spec = pl.BlockSpec((TM, LANES), lambda i: (i % nblk_seq, 0))
    weights = [lw[n] for n in ('g_mix', 'w_in', 'g_cq', 'g_ckv', 'w_uq', 'w_uk', 'w_uv',
                               'g_mq', 'g_mk', 'g_kr', 'g_nq', 'g_nk', 'g_dq', 'g_dk')]
    joins = _head_join_matrices()
    tables = [tabs['q_c'], tabs['q_s1'], tabs['q_s2'], tabs['k_c'], tabs['k_s'], tabs['d_c'], tabs['d_s']]

    def out(n):
        return (jax.ShapeDtypeStruct((n, t, LANES), _BF), pl.BlockSpec((n, TM, LANES), lambda i: (0, i, 0)))

    vt_out = (jax.ShapeDtypeStruct((MLA_HEADS, t // TM, LANES, TM), _BF),
              pl.BlockSpec((MLA_HEADS, 1, LANES, TM), lambda i: (0, i, 0, 0)))
    outs = [out(6), out(6), vt_out, out(3), out(3), out(3), out(6), out(6), out(6)]
    return pl.pallas_call(
        _mix_in_kernel,
        grid=(t // TM,),
        in_specs=[pl.BlockSpec((TM, D_MODEL), lambda i: (i, 0))]
        + [_layer_spec(w, layer) for w in weights] + [_const_spec(m.shape) for m in joins]
        + [tab_spec] * len(tables),
        out_specs=[o[1] for o in outs],
        out_shape=[o[0] for o in outs],
        scratch_shapes=[pltpu.VMEM((6, TM, LANES), _F32)],
        compiler_params=pltpu.CompilerParams(dimension_semantics=("parallel",), vmem_limit_bytes=VMEM_LIMIT),
        name="mix_in",
    )(x, *weights, *joins, *tables)


def _mla_kernel(q_ref, k_ref, vt_ref, o_ref, sa_sc, sb_sc, m_sc, acc_sc, *, tk):
    tq = q_ref.shape[1]
    nk = k_ref.shape[1] // tk
    sub = tk // TM
    m_sc[...] = jnp.full(m_sc.shape, NEG, _F32)
    acc_sc[...] = jnp.zeros(acc_sc.shape, _F32)

    def scores(j, dst):
        rows = pl.ds(pl.multiple_of(j * tk, tk), tk)
        for hh in range(2):
            dst[hh] = _dot_t(k_ref[hh, rows, :], q_ref[hh])

    def consume(j, src):
        for hh in range(2):
            s = src[hh]
            m_prev = m_sc[hh]
            m_new = jnp.maximum(m_prev, jnp.max(s, axis=0, keepdims=True))
            p = jnp.exp2(s - m_new).astype(_BF)
            vt = jnp.concatenate([vt_ref[hh, j * sub + c] for c in range(sub)], axis=1)
            acc_sc[hh] = jnp.exp2(m_prev - m_new) * acc_sc[hh] + _dot(vt, p)
            m_sc[hh] = m_new

    def pair(i, carry):
        j = 2 * i
        scores(j + 1, sb_sc)
        consume(j, sa_sc)
        scores(j + 2, sa_sc)
        consume(j + 1, sb_sc)
        return carry

    scores(0, sa_sc)
    lax.fori_loop(0, nk // 2 - 1, pair, 0)
    scores(nk - 1, sb_sc)
    consume(nk - 2, sa_sc)
    consume(nk - 1, sb_sc)

    lane = _lane((tq, LANES))
    outs = []
    for hh in range(2):
        acc = acc_sc[hh]
        outs.append((acc / acc[MLA_V:MLA_V + 1, :]).T)
    o_ref[0] = jnp.where(lane < MLA_V, outs[0], pltpu.roll(outs[1], MLA_V, 1)).astype(o_ref.dtype)


def _mla(q, k, vt, batch, seq, tq=2048, tk=512):
    t = q.shape[1]
    nq = seq // tq
    nblk = seq // TM
    assert seq % (2 * tk) == 0 and tk % TM == 0
    return pl.pallas_call(
        functools.partial(_mla_kernel, tk=tk),
        grid=(batch, MLA_HEADS // 2, nq),
        in_specs=[pl.BlockSpec((2, tq, LANES), lambda b, p, i: (p, b * nq + i, 0)),
                  pl.BlockSpec((2, seq, LANES), lambda b, p, i: (p, b, 0)),
                  pl.BlockSpec((2, nblk, LANES, TM), lambda b, p, i: (p, b, 0, 0))],
        out_specs=pl.BlockSpec((1, tq, LANES), lambda b, p, i: (p, b * nq + i, 0)),
        out_shape=jax.ShapeDtypeStruct((MLA_HEADS // 2, t, LANES), _BF),
        scratch_shapes=[pltpu.VMEM((2, tk, tq), _F32), pltpu.VMEM((2, tk, tq), _F32),
                        pltpu.VMEM((2, 1, tq), _F32), pltpu.VMEM((2, LANES, tq), _F32)],
        compiler_params=pltpu.CompilerParams(
            dimension_semantics=("parallel", "parallel", "arbitrary"), vmem_limit_bytes=VMEM_LIMIT),
        name="mla_attention",
    )(q, k, vt)


def _na_kernel(q_ref, k_ref, v_ref, bias_ref, o_ref, *, rows):
    i = pl.program_id(2)
    lane = _lane((GRID_W, LANES))
    first = lane < HEAD_DIM
    nkeys = NA_WIN_R * GRID_W

    def score(rr):
        r = i * NA_RB + rr
        rs = jnp.clip(r - NA_WIN_R // 2, 0, rows - NA_WIN_R)
        start = pl.multiple_of(rs * GRID_W, GRID_W)
        q = q_ref[0, rr * GRID_W:(rr + 1) * GRID_W, :]
        zero = jnp.zeros_like(q)
        q2 = jnp.concatenate([jnp.where(first, q, zero), jnp.where(first, zero, q)], axis=0)
        return _dot_t(q2, k_ref[0, pl.ds(start, nkeys), :]) + bias_ref[0, r - rs], start

    def softmax(s):
        p = jnp.exp(s - jnp.max(s, axis=1, keepdims=True))
        return p.astype(_BF), jnp.sum(p, axis=1, keepdims=True)

    def value(rr, p, l, start):
        o2 = _dot(p, v_ref[0, pl.ds(start, nkeys), :]) / l
        o = jnp.where(first, o2[:GRID_W], o2[GRID_W:])
        o_ref[0, rr * GRID_W:(rr + 1) * GRID_W, :] = o.astype(o_ref.dtype)

    scored, normed = {}, {}
    for step in range(NA_RB + 2):
        if step < NA_RB:
            scored[step] = score(step)
        if 1 <= step <= NA_RB:
            s, start = scored.pop(step - 1)
            normed[step - 1] = softmax(s) + (start,)
        if step >= 2:
            value(step - 2, *normed.pop(step - 2))


def _na(q, k, v, bias, layer, batch, seq):
    t = q.shape[1]
    rows = seq // GRID_W
    assert rows >= NA_WIN_R and rows % NA_RB == 0
    nrb = rows // NA_RB
    qrows = NA_RB * GRID_W
    return pl.pallas_call(
        functools.partial(_na_kernel, rows=rows),
        grid=(batch, NA_HEADS // 2, nrb),
        in_specs=[pl.BlockSpec((1, qrows, LANES), lambda b, p, i: (p, b * nrb + i, 0)),
                  pl.BlockSpec((1, seq, LANES), lambda b, p, i: (p, b, 0)),
                  pl.BlockSpec((1, seq, LANES), lambda b, p, i: (p, b, 0)),
                  pl.BlockSpec((None, 1, NA_WIN_R, 2 * GRID_W, NA_WIN_R * GRID_W),
                               lambda b, p, i: (layer, p, 0, 0, 0))],
        out_specs=pl.BlockSpec((1, qrows, LANES), lambda b, p, i: (p, b * nrb + i, 0)),
        out_shape=jax.ShapeDtypeStruct((NA_HEADS // 2, t, LANES), _BF),
        compiler_params=pltpu.CompilerParams(
            dimension_semantics=("parallel", "parallel", "arbitrary"), vmem_limit_bytes=VMEM_LIMIT),
        name="na_attention",
    )(q, k, v, bias)


def _dil_kernel(q0_ref, q1_ref, q2_ref, k0_ref, k1_ref, k2_ref, v0_ref, v1_ref, v2_ref, band_ref,
                o_ref, og_sc, lse_sc, *, seq):
    sb = pl.program_id(2)
    q_refs = (q0_ref, q1_ref, q2_ref)
    k_refs = (k0_ref, k1_ref, k2_ref)
    v_refs = (v0_ref, v1_ref, v2_ref)
    lane = _lane((DIL_TQ, LANES))
    first_qk = (lane % HEAD_DIM) < (HEAD_DIM // 2)
    first_v = lane < HEAD_DIM
    n_tiles = DIL_SB // DIL_TQ
    half = DIL_TQ // 2

    for g in range(DIL_GROUPS):
        rate = DIL_RATES[g]
        length = seq // rate
        per_blk = TM // rate
        chunk = min(half, per_blk)
        n_chunks = (2 * DIL_TQ) // chunk
        tiles_per_res = n_tiles // rate
        q_ref, k_ref, v_ref = q_refs[g], k_refs[g], v_refs[g]

        def row_of(m, rho, _per_blk=per_blk):
            return (m // _per_blk) * TM + rho * _per_blk + (m % _per_blk)

        def body(tg, carry, _g=g, _rate=rate, _length=length, _per_blk=per_blk, _chunk=chunk,
                 _n_chunks=n_chunks, _tpr=tiles_per_res, _q=q_ref, _k=k_ref, _v=v_ref):
            def score(u):
                tt = tg * DIL_UNROLL + u
                rho = tt // _tpr
                mt = tt % _tpr
                m0 = sb * (DIL_SB // _rate) + mt * DIL_TQ
                q = jnp.concatenate(
                    [_q[0, pl.ds(pl.multiple_of(row_of(m0 + c * _chunk, rho), _chunk), _chunk), :]
                     for c in range(DIL_TQ // _chunk)], axis=0)
                zero = jnp.zeros_like(q)
                q2 = jnp.concatenate([jnp.where(first_qk, q, zero), jnp.where(first_qk, zero, q)], axis=0)
                win = []
                for c in range(_n_chunks):
                    ms = jnp.clip(m0 - half + c * _chunk, 0, _length - _chunk)
                    win.append(pl.multiple_of(row_of(ms, rho), _chunk))
                kw = jnp.concatenate([_k[0, pl.ds(row, _chunk), :] for row in win], axis=0)
                edge = (m0 == 0).astype(jnp.int32) + 2 * (m0 == _length - DIL_TQ).astype(jnp.int32)
                return _dot_t(q2, kw) + band_ref[edge], rho + _rate * mt * DIL_TQ, win

            def softmax(s):
                m = jnp.max(s, axis=1, keepdims=True)
                p = jnp.exp(s - m)
                return p.astype(_BF), m, jnp.sum(p, axis=1, keepdims=True)

            def value(p, m, l, base, win):
                vw = jnp.concatenate([_v[0, pl.ds(row, _chunk), :] for row in win], axis=0)
                o2 = _dot(p, vw) / l
                lse2 = jnp.broadcast_to(m + jnp.log(l), (2 * DIL_TQ, LANES))
                if _rate == 1:
                    dst = pl.ds(pl.multiple_of(base, DIL_TQ), DIL_TQ)
                else:
                    dst = pl.ds(base, DIL_TQ, stride=_rate)
                og_sc[_g, dst, :] = jnp.where(first_v, o2[:DIL_TQ], o2[DIL_TQ:])
                lse_sc[_g, dst, :] = jnp.where(first_v, lse2[:DIL_TQ], lse2[DIL_TQ:])

            scored, normed = {}, {}
            for step in range(DIL_UNROLL + 2):
                if step < DIL_UNROLL:
                    scored[step] = score(step)
                if 1 <= step <= DIL_UNROLL:
                    s, base, win = scored.pop(step - 1)
                    normed[step - 1] = softmax(s) + (base, win)
                if step >= 2:
                    value(*normed.pop(step - 2))
            return carry

        lax.fori_loop(0, n_tiles // DIL_UNROLL, body, 0)

    m = jnp.maximum(jnp.maximum(lse_sc[0], lse_sc[1]), lse_sc[2])
    w0 = jnp.exp(lse_sc[0] - m)
    w1 = jnp.exp(lse_sc[1] - m)
    w2 = jnp.exp(lse_sc[2] - m)
    o = (w0 * og_sc[0] + w1 * og_sc[1] + w2 * og_sc[2]) / (w0 + w1 + w2)
    o_ref[0] = o.astype(o_ref.dtype)


def _dil(q, k, v, band, batch, seq):
    t = q.shape[1]
    assert seq % DIL_SB == 0
    nsb = seq // DIL_SB

    def q_spec(g):
        return pl.BlockSpec((1, DIL_SB, LANES), lambda b, sp, s, _g=g: (2 * _g + sp, b * nsb + s, 0))

    def kv_spec(g):
        return pl.BlockSpec((1, seq, LANES), lambda b, sp, s, _g=g: (2 * _g + sp, b, 0))

    return pl.pallas_call(
        functools.partial(_dil_kernel, seq=seq),
        grid=(batch, 2, nsb),
        in_specs=[kv_spec(0), kv_spec(1), kv_spec(2), kv_spec(0), kv_spec(1), kv_spec(2),
                  kv_spec(0), kv_spec(1), kv_spec(2),
                  pl.BlockSpec((4, 2 * DIL_TQ, 2 * DIL_TQ), lambda b, sp, s: (0, 0, 0))],
        out_specs=pl.BlockSpec((1, DIL_SB, LANES), lambda b, sp, s: (sp, b * nsb + s, 0)),
        out_shape=jax.ShapeDtypeStruct((2, t, LANES), _BF),
        scratch_shapes=[pltpu.VMEM((DIL_GROUPS, DIL_SB, LANES), _F32),
                        pltpu.VMEM((DIL_GROUPS, DIL_SB, LANES), _F32)],
        compiler_params=pltpu.CompilerParams(
            dimension_semantics=("parallel", "parallel", "arbitrary"), vmem_limit_bytes=VMEM_LIMIT),
        name="dil_attention",
    )(q, q, q, k, k, k, v, v, v, band)


def _mem_kv_kernel(mem_ref, gmem_ref, wckv_ref, gxk_ref, k_ref, v_ref):
    hm = _rms(mem_ref[0], gmem_ref[...]).astype(_BF)
    kv = _dot(hm, wckv_ref[...])
    for hd in range(X_HEADS):
        kh = kv[:, hd * X_HEAD_DIM:(hd + 1) * X_HEAD_DIM]
        k_ref[0, :, hd * X_HEAD_DIM:(hd + 1) * X_HEAD_DIM] = _rms(kh, gxk_ref[...]).astype(_BF)
    v_ref[0] = kv[:, D_MODEL:].astype(_BF)


def _mem_kv(mem, lw):
    b, m, _ = mem.shape
    weights = [lw[n] for n in ('g_mem', 'w_ckv', 'g_xk')]
    nl = weights[0].shape[0]

    def per_layer(arr):
        nd = arr.ndim
        return pl.BlockSpec((None,) + arr.shape[1:], lambda l, i, _nd=nd: (l,) + (0,) * (_nd - 1))

    out_spec = pl.BlockSpec((None, 1, m, D_MODEL), lambda l, i: (l, i, 0, 0))
    return pl.pallas_call(
        _mem_kv_kernel,
        grid=(nl, b),
        in_specs=[pl.BlockSpec((1, m, D_MODEL), lambda l, i: (i, 0, 0))] + [per_layer(w) for w in weights],
        out_specs=[out_spec] * 2,
        out_shape=[jax.ShapeDtypeStruct((nl, b, m, D_MODEL), _BF)] * 2,
        compiler_params=pltpu.CompilerParams(
            dimension_semantics=("parallel", "parallel"), vmem_limit_bytes=VMEM_LIMIT),
        name="mem_kv",
    )(mem, *weights)


def _mix_out_kernel(x_ref, oa_ref, ob_ref, oc_ref, wo_ref, gcross_ref, wcq_ref, gxq_ref,
                    mk_ref, mv_ref, wco_ref, y_ref):
    mix = jnp.concatenate([oa_ref[p] for p in range(3)] + [ob_ref[p] for p in range(3)]
                          + [oc_ref[p] for p in range(2)], axis=1)
    x1 = x_ref[...] + _dot(mix, wo_ref[...])
    inv_x = lax.rsqrt(jnp.mean(x1 * x1, axis=-1, keepdims=True) + EPS)
    q = _dot((x1 * gcross_ref[...]).astype(_BF), wcq_ref[...]) * inv_x
    heads = [slice(hd * X_HEAD_DIM, (hd + 1) * X_HEAD_DIM) for hd in range(X_HEADS)]
    scores = [_dot_t((_rms(q[:, sl], gxq_ref[...]) * X_HEAD_DIM ** -0.5).astype(_BF), mk_ref[0, :, sl])
              for sl in heads]
    probs = []
    for s in scores:
        p = jnp.exp(s - jnp.max(s, axis=1, keepdims=True))
        probs.append((p.astype(_BF), jnp.sum(p, axis=1, keepdims=True)))
    outs = [(_dot(p, mv_ref[0, :, sl]) / l).astype(_BF) for (p, l), sl in zip(probs, heads)]
    o = jnp.concatenate(outs, axis=1)
    y_ref[...] = x1 + _dot(o, wco_ref[...])


def _mix_out(x, oa, ob, oc, mk, mv, lw, layer, seq):
    t = x.shape[0]
    nblk_seq = seq // TM
    mem_len = mk.shape[2]

    def grp(n):
        return pl.BlockSpec((n, TM, LANES), lambda i: (0, i, 0))

    mem_spec = pl.BlockSpec((None, 1, mem_len, D_MODEL), lambda i: (layer, i // nblk_seq, 0, 0))
    return pl.pallas_call(
        _mix_out_kernel,
        grid=(t // TM,),
        in_specs=[pl.BlockSpec((TM, D_MODEL), lambda i: (i, 0)), grp(3), grp(3), grp(2),
                  _layer_spec(lw['w_o'], layer), _layer_spec(lw['g_cross'], layer),
                  _layer_spec(lw['w_cq'], layer), _layer_spec(lw['g_xq'], layer),
                  mem_spec, mem_spec, _layer_spec(lw['w_co'], layer)],
        out_specs=pl.BlockSpec((TM, D_MODEL), lambda i: (i, 0)),
        out_shape=jax.ShapeDtypeStruct((t, D_MODEL), _F32),
        compiler_params=pltpu.CompilerParams(dimension_semantics=("parallel",), vmem_limit_bytes=VMEM_LIMIT),
        name="mix_out",
    )(x, oa, ob, oc, lw['w_o'], lw['g_cross'], lw['w_cq'], lw['g_xq'], mk, mv, lw['w_co'])


def _ffn_kernel(x_ref, xp_ref, xn_ref, gffn_ref, wup_ref, cw_ref, cb_ref, wdn_ref, y_ref, *, nblk_seq):
    i = pl.program_id(0)
    tm = x_ref.shape[0]
    x = x_ref[...]
    g = gffn_ref[...]
    is_first = (i % nblk_seq) == 0
    is_last = (i % nblk_seq) == nblk_seq - 1
    hp = jnp.where(is_first, 0.0, _rms(xp_ref[...], g))
    hn = jnp.where(is_last, 0.0, _rms(xn_ref[...], g))
    h = jnp.concatenate([hp, _rms(x, g), hn], axis=0).astype(_BF)
    def up_proj(c):
        return [_dot(h, wup_ref[:, part * D_FF + c * FF_CH: part * D_FF + (c + 1) * FF_CH]) for part in range(2)]

    def conv(u, lo):
        cw = cw_ref[:, lo:lo + FF_CH]
        up = pltpu.roll(u, 1, 0)[8:8 + tm]
        un = pltpu.roll(u, tm + 15, 0)[8:8 + tm]
        return up * cw[0:1] + u[8:8 + tm] * cw[1:2] + un * cw[2:3] + cb_ref[:, lo:lo + FF_CH]

    acts = []
    u_next = up_proj(0)
    for c in range(N_FF_CH):
        u_val, u_gate = u_next
        if c + 1 < N_FF_CH:
            u_next = up_proj(c + 1)
        a = conv(u_val, c * FF_CH)
        half_gate = conv(u_gate, D_FF + c * FF_CH)
        acts.append((half_gate * (1.0 + jnp.tanh(half_gate)) * a).astype(_BF))
    y_ref[...] = x + _dot(jnp.concatenate(acts, axis=1), wdn_ref[...])


def _ffn(x, lw, layer, seq):
    t = x.shape[0]
    nblk_seq = seq // TM
    nhalo = t // 8
    per = TM // 8
    return pl.pallas_call(
        functools.partial(_ffn_kernel, nblk_seq=nblk_seq),
        grid=(t // TM,),
        in_specs=[pl.BlockSpec((TM, D_MODEL), lambda i: (i, 0)),
                  pl.BlockSpec((8, D_MODEL), lambda i: (jnp.maximum(i * per - 1, 0), 0)),
                  pl.BlockSpec((8, D_MODEL), lambda i: (jnp.minimum((i + 1) * per, nhalo - 1), 0))]
        + [_layer_spec(lw[n], layer) for n in ('g_ffn', 'w_up', 'conv_w', 'conv_b', 'w_down')],
        out_specs=pl.BlockSpec((TM, D_MODEL), lambda i: (i, 0)),
        out_shape=jax.ShapeDtypeStruct((t, D_MODEL), _F32),
        compiler_params=pltpu.CompilerParams(dimension_semantics=("parallel",), vmem_limit_bytes=VMEM_LIMIT),
        name="conv_ffn",
    )(x, x, x, lw['g_ffn'], lw['w_up'], lw['conv_w'], lw['conv_b'], lw['w_down'])


def _relayout_w_in(w):
    nl = w.shape[0]
    lead = w[..., :SPLIT_CKV]
    a = w[..., SPLIT_CKV:SPLIT_CKV + 16]
    b = w[..., SPLIT_CKV + 16:SPLIT_KR]
    z32 = jnp.zeros(a.shape[:-1] + (32,), w.dtype)
    kr = jnp.concatenate([b, a, z32, a, b, z32], axis=-1)
    na = w[..., SPLIT_KR:SPLIT_NA].reshape(nl, D_MODEL, 3, NA_HEADS // 2, LANES)
    na_ab = na[:, :, :, 0:2].reshape(nl, D_MODEL, 3 * 2 * LANES)
    na_c = na[:, :, :, 2].reshape(nl, D_MODEL, 3 * LANES)
    dl = w[..., SPLIT_NA:].reshape(nl, D_MODEL, 3, DIL_GROUPS, 2, 2, 2, HEAD_DIM // 2)
    qk = dl[:, :, 0:2].transpose(0, 1, 3, 2, 4, 6, 5, 7).reshape(nl, D_MODEL, DIL_GROUPS, 4 * LANES)
    v = dl[:, :, 2].reshape(nl, D_MODEL, DIL_GROUPS, 2 * LANES)
    dil = jnp.concatenate([qk, v], axis=-1).reshape(nl, D_MODEL, DIL_GROUPS * 6 * LANES)
    out = jnp.concatenate([lead, kr, na_ab, na_c, dil], axis=-1)
    assert out.shape[-1] == D_IN_P
    return out


def _pad_heads(w, real, n_heads):
    nl, k, _ = w.shape
    w = w.reshape(nl, k, n_heads, real)
    return jnp.pad(w, ((0, 0), (0, 0), (0, 0), (0, LANES - real))).reshape(nl, k, n_heads * LANES)


def _rows(v):
    return v[:, None, :].astype(_F32)


def _prep_layers(p):
    nl = p['w_in'].shape[0]
    kn, qn = p['mla_kn'], p['mla_qn']
    z32 = jnp.zeros((nl, 32), _F32)
    z64 = jnp.zeros((nl, 64), _F32)
    dq, dk = p['dil_qn'], p['dil_kn']

    def rot(gain):
        return jnp.concatenate([gain[:, :32], gain[:, :32], gain[:, 32:], gain[:, 32:]], axis=-1)

    gate_half = jnp.concatenate([jnp.ones((D_FF,), _F32), jnp.full((D_FF,), 0.5, _F32)])

    return {
        'g_mix': _rows(p['norm_mix']), 'w_in': _relayout_w_in(p['w_in'].astype(_BF)),
        'g_cq': _rows(p['mla_q_norm']), 'g_ckv': _rows(p['mla_kv_norm']),
        'w_uq': _pad_heads(p['w_uq'].astype(_BF), MLA_QK, MLA_HEADS),
        'w_uk': _pad_heads(p['w_uk'].astype(_BF), MLA_NOPE, MLA_HEADS),
        'w_uv': _pad_heads(p['w_uv'].astype(_BF), MLA_V, MLA_HEADS),
        'g_mq': _rows(jnp.concatenate([qn, z32], axis=-1) * (MLA_QK ** -0.5 * LOG2E)),
        'g_mk': _rows(jnp.concatenate([kn[:, :MLA_NOPE], z64], axis=-1)),
        'g_kr': _rows(jnp.concatenate([kn[:, 80:96], kn[:, 64:80], z32, kn[:, 64:80], kn[:, 80:96], z32], axis=-1)),
        'g_nq': _rows(jnp.tile(p['na_qn'], (1, 2)) * HEAD_DIM ** -0.5), 'g_nk': _rows(jnp.tile(p['na_kn'], (1, 2))),
        'g_dq': _rows(rot(dq) * HEAD_DIM ** -0.5), 'g_dk': _rows(rot(dk)),
        'w_o': p['w_o'].astype(_BF),
        'g_cross': _rows(p['norm_cross']), 'w_cq': p['w_cq'].astype(_BF),
        'g_xq': _rows(p['x_qn']), 'g_xk': _rows(p['x_kn']), 'g_mem': _rows(p['norm_mem']),
        'w_ckv': p['w_ckv'].astype(_BF), 'w_co': p['w_co'].astype(_BF),
        'g_ffn': _rows(p['norm_ffn']), 'w_up': p['w_up'].astype(_BF),
        'conv_w': p['conv_w'] * gate_half, 'conv_b': _rows(p['conv_b'] * gate_half),
        'w_down': p['w_down'].astype(_BF),
        'na_bias': _na_bias(p['na_rpb']),
    }


def _na_bias(rpb):
    nl = rpb.shape[0]
    dl = np.arange(NA_WIN_R)
    j = np.arange(NA_WIN_R)
    r_off = j[None, :] - dl[:, None] + (NA_WIN_R - 1)
    qc = np.arange(GRID_W)
    kc = np.arange(GRID_W)
    c_start = np.clip(qc - NA_WIN_C // 2, 0, GRID_W - NA_WIN_C)
    valid = (kc[None, :] >= c_start[:, None]) & (kc[None, :] < c_start[:, None] + NA_WIN_C)
    c_off = kc[None, :] - qc[:, None] + (NA_WIN_C - 1)
    pick = (np.arange(2 * NA_WIN_C - 1)[:, None, None] == c_off[None]) & valid[None]
    rows = rpb[:, :, r_off].reshape(nl, NA_HEADS // 2, 2, NA_WIN_R, NA_WIN_R, 2 * NA_WIN_C - 1)
    b = jnp.einsum('lpedjc,cqk->lpdeqjk', rows, jnp.asarray(pick, _F32), precision=lax.Precision.HIGHEST)
    b = b + jnp.asarray(np.where(valid, 0.0, NEG), _F32)[:, None, :]
    return b.reshape(nl, NA_HEADS // 2, NA_WIN_R, 2 * GRID_W, NA_WIN_R * GRID_W)


def _band_bias():
    i = np.arange(DIL_TQ)[:, None]
    c = np.arange(2 * DIL_TQ)[None, :]
    band = (c >= i) & (c <= i + DIL_TQ)
    out = []
    for edge in range(4):
        ok = band
        if edge & 1:
            ok = ok & (c >= DIL_TQ // 2)
        if edge & 2:
            ok = ok & (c < 2 * DIL_TQ - DIL_TQ // 2)
        out.append(np.tile(np.where(ok, 0.0, NEG), (2, 1)))
    return jnp.asarray(np.stack(out), _F32)


def _rope_tables(seq):
    pos = jnp.arange(seq, dtype=_F32)[:, None]

    def cs(half):
        inv = ROPE_THETA ** (-jnp.arange(half, dtype=_F32) / half)
        ang = pos * inv[None, :]
        return jnp.cos(ang), jnp.sin(ang)

    c16, s16 = cs(MLA_ROPE // 2)
    c32, s32 = cs(HEAD_DIM // 2)
    one = jnp.ones((seq, 64), _F32)
    z16 = jnp.zeros((seq, 16), _F32)
    z32 = jnp.zeros((seq, 32), _F32)
    z64 = jnp.zeros((seq, 64), _F32)
    return {
        'q_c': jnp.concatenate([one, c16, c16, z32], axis=1),
        'q_s1': jnp.concatenate([z64, z16, s16, z32], axis=1),
        'q_s2': jnp.concatenate([z64, -s16, z16, z32], axis=1),
        'k_c': jnp.concatenate([z64, c16, c16, z32], axis=1),
        'k_s': jnp.concatenate([z64, -s16, s16, z32], axis=1),
        'd_c': jnp.concatenate([c32, c32, c32, c32], axis=1),
        'd_s': jnp.concatenate([-s32, -s32, s32, s32], axis=1),
    }


def _trunk_layer(x, mk, mv, lw, layer, tabs, band, batch, seq):
    qm, km, vm, qn, kn, vn, qd, kd, vd = _mix_in(x, lw, layer, tabs, seq)
    oa = _mla(qm, km, vm, batch, seq)
    ob = _na(qn, kn, vn, lw['na_bias'], layer, batch, seq)
    oc = _dil(qd, kd, vd, band, batch, seq)
    x = _mix_out(x, oa, ob, oc, mk, mv, lw, layer, seq)
    return _ffn(x, lw, layer, seq)


def kernel(x_prompt, x_sample, mem_prompt, mem_sample, norm_mix, w_in, mla_q_norm, mla_kv_norm, w_uq, w_uk, w_uv, mla_qn, mla_kn, na_qn, na_kn, na_rpb, dil_qn, dil_kn, w_o, norm_cross, norm_mem, w_cq, w_ckv, x_qn, x_kn, w_co, norm_ffn, w_up, conv_w, conv_b, w_down):
    stacked = {
        'norm_mix': norm_mix, 'w_in': w_in, 'mla_q_norm': mla_q_norm, 'mla_kv_norm': mla_kv_norm,
        'w_uq': w_uq, 'w_uk': w_uk, 'w_uv': w_uv, 'mla_qn': mla_qn, 'mla_kn': mla_kn,
        'na_qn': na_qn, 'na_kn': na_kn, 'na_rpb': na_rpb, 'dil_qn': dil_qn, 'dil_kn': dil_kn,
        'w_o': w_o, 'norm_cross': norm_cross, 'norm_mem': norm_mem, 'w_cq': w_cq, 'w_ckv': w_ckv,
        'x_qn': x_qn, 'x_kn': x_kn, 'w_co': w_co, 'norm_ffn': norm_ffn, 'w_up': w_up,
        'conv_w': conv_w, 'conv_b': conv_b, 'w_down': w_down,
    }
    lw = _prep_layers(stacked)
    band = _band_bias()
    outs = []
    for x, mem in ((x_prompt, mem_prompt), (x_sample, mem_sample)):
        batch, seq, _ = x.shape
        tabs = _rope_tables(seq)
        mk, mv = _mem_kv(mem, lw)
        xt = x.reshape(batch * seq, D_MODEL)
        for layer in range(DEPTH):
            xt = _trunk_layer(xt, mk, mv, lw, layer, tabs, band, batch, seq)
        outs.append(xt.reshape(batch, seq, D_MODEL))
    return tuple(outs)
```

```python
import functools

import jax
import jax.numpy as jnp
import numpy as np
from jax import lax
from jax.experimental import pallas as pl
from jax.experimental.pallas import tpu as pltpu

D_MODEL = 1024
DEPTH = 4
GRID_W = 64
HEAD_DIM = 64
ROPE_THETA = 10000.0
EPS = 1e-6
MLA_HEADS = 6
MLA_Q_RANK = 256
MLA_KV_RANK = 128
MLA_NOPE = 64
MLA_ROPE = 32
MLA_V = 64
MLA_QK = MLA_NOPE + MLA_ROPE
NA_HEADS = 6
NA_WIN_R = 8
NA_WIN_C = 16
DIL_WINDOWS = (128, 512, 2048)
DIL_RATES = (1, 4, 16)
DIL_GROUPS = 3
DIL_HEADS = 4
DIL_ALL = DIL_GROUPS * DIL_HEADS
X_HEADS = 4
X_HEAD_DIM = D_MODEL // X_HEADS
D_FF = 2816
CONV_W = 3
SPLIT_CQ = MLA_Q_RANK
SPLIT_CKV = SPLIT_CQ + MLA_KV_RANK
SPLIT_KR = SPLIT_CKV + MLA_ROPE
SPLIT_NA = SPLIT_KR + 3 * NA_HEADS * HEAD_DIM
D_IN = SPLIT_NA + 3 * DIL_ALL * HEAD_DIM

LANES = 128
NEG = -1e30
LOG2E = 1.4426950408889634
TM = 512
TM_OUT = 1024
DIL_TQ = 128
DIL_SB = DIL_TQ * max(DIL_RATES)
DIL_UNROLL = 16
NA_RB = 32
FF_CH = 256
N_FF_CH = D_FF // FF_CH
VMEM_LIMIT = 56 * 1024 * 1024

COL_MLA = 0
COL_NA = 512
COL_DIL = COL_NA + 3 * NA_HEADS * HEAD_DIM
D_IN_P = COL_DIL + 3 * DIL_ALL * HEAD_DIM

_BF = jnp.bfloat16
_F32 = jnp.float32


def _dot(a, b):
    return jnp.dot(a, b, preferred_element_type=_F32)


def _dot_t(a, b):
    return lax.dot_general(a, b, (((1,), (1,)), ((), ())), preferred_element_type=_F32)


def _rms(x, g):
    ms = jnp.mean(x * x, axis=-1, keepdims=True)
    return x * lax.rsqrt(ms + EPS) * g


def _lane(shape):
    return lax.broadcasted_iota(jnp.int32, shape, len(shape) - 1)


def _head_join_matrices():
    i = np.arange(2 * LANES)
    same_group = (i[:, None] // LANES) == (i[None, :] // LANES)
    natural = (i[:, None] // HEAD_DIM) == (i[None, :] // HEAD_DIM)
    half = ((i % HEAD_DIM) < HEAD_DIM // 2)
    rotary = same_group & (half[:, None] == half[None, :])
    return [jnp.asarray(m, _BF) for m in (same_group, natural, rotary)]


def _seg_inv(slabs, bd_ref, dim):
    rows = slabs[0].shape[0]
    sq = jnp.concatenate([(x2 * x2).astype(_BF) for x2 in slabs], axis=0)
    inv = lax.rsqrt(_dot(sq, bd_ref[...]) * (1.0 / dim) + EPS)
    return [inv[n * rows:(n + 1) * rows] for n in range(len(slabs))]


def _mix_in_kernel(x_ref, gmix_ref, win_ref, gcq_ref, gckv_ref, wuq_ref, wuk_ref, wuv_ref,
                   gmq_ref, gmk_ref, gkr_ref, gnq_ref, gnk_ref, gdq_ref, gdk_ref,
                   bd128_ref, bd64_ref, bdr_ref,
                   tqc_ref, tqs1_ref, tqs2_ref, tkc_ref, tks_ref, tdc_ref, tds_ref,
                   qm_ref, km_ref, vm_ref, qn_ref, kn_ref, vn_ref, qd_ref, kd_ref, vd_ref,
                   perm_ref):
    tm = x_ref.shape[0]
    x = x_ref[...]
    xg = (x * gmix_ref[...]).astype(_BF)
    inv_x = lax.rsqrt(jnp.mean(x * x, axis=-1, keepdims=True) + EPS)
    lane = _lane((tm, LANES))

    def project(lo, hi):
        return _dot(xg, win_ref[:, lo:hi]) * inv_x

    def latent(z):
        cq = _rms(z[:, 0:MLA_Q_RANK], gcq_ref[...]).astype(_BF)
        ckv = _rms(z[:, MLA_Q_RANK:MLA_Q_RANK + MLA_KV_RANK], gckv_ref[...]).astype(_BF)
        kr = z[:, 384:512]
        kr_own = jnp.where((lane >= MLA_NOPE) & (lane < MLA_QK), kr, 0.0)
        krg = kr * gkr_ref[...]
        kr_rot = krg * tkc_ref[...] + pltpu.roll(krg, 64, 1) * tks_ref[...]
        qpre = _dot(cq, wuq_ref[...])
        kpre = _dot(ckv, wuk_ref[...])
        vpre = _dot(ckv, wuv_ref[...])
        row_t = lax.broadcasted_iota(jnp.int32, (LANES, tm), 0)
        npair = MLA_HEADS // 2
        q_pairs = [qpre[:, 2 * c * LANES:(2 * c + 2) * LANES] for c in range(npair)]
        k_pairs = [kpre[:, 2 * c * LANES:(2 * c + 2) * LANES] for c in range(npair)]
        kr2 = jnp.concatenate([kr_own, kr_own], axis=1)
        invs = _seg_inv(q_pairs + [k2 + kr2 for k2 in k_pairs], bd128_ref, MLA_QK)
        for c in range(npair):
            q2, k2, inv_q, inv_k = q_pairs[c], k_pairs[c], invs[c], invs[npair + c]
            for hh in range(2):
                hd = 2 * c + hh
                sl = slice(hh * LANES, (hh + 1) * LANES)
                qg = q2[:, sl] * gmq_ref[...]
                qr = (qg * tqc_ref[...] + pltpu.roll(qg, 16, 1) * tqs1_ref[...]
                      + pltpu.roll(qg, 112, 1) * tqs2_ref[...])
                qm_ref[hd] = (qr * inv_q[:, sl]).astype(_BF)
                km_ref[hd] = ((k2[:, sl] * gmk_ref[...] + kr_rot) * inv_k[:, sl]).astype(_BF)
                vt = vpre[:, hd * LANES:(hd + 1) * LANES].T
                vm_ref[hd, 0] = jnp.where(row_t == MLA_V, 1.0, vt).astype(_BF)

    def neighbourhood(z):
        q2 = z[:, 0:256]
        k2 = z[:, 256:512]
        qk2 = z[:, 768:1024]
        inv_q, inv_k, inv_qk = _seg_inv([q2, k2, qk2], bd64_ref, HEAD_DIM)
        qn2, kn2, qkn2 = q2 * inv_q, k2 * inv_k, qk2 * inv_qk
        for p in range(2):
            sl = slice(p * LANES, (p + 1) * LANES)
            qn_ref[p] = (qn2[:, sl] * gnq_ref[...]).astype(_BF)
            kn_ref[p] = (kn2[:, sl] * gnk_ref[...]).astype(_BF)
            vn_ref[p] = z[:, 512 + p * LANES: 512 + (p + 1) * LANES].astype(_BF)
        qn_ref[2] = (qkn2[:, :LANES] * gnq_ref[...]).astype(_BF)
        kn_ref[2] = (qkn2[:, LANES:] * gnk_ref[...]).astype(_BF)
        vn_ref[2] = z[:, 1024:1152].astype(_BF)

    def put(out_ref, p, val, rate, slot):
        if rate == 1:
            out_ref[p] = val.astype(_BF)
        else:
            rows = tm // rate
            perm_ref[slot] = val
            for rho in range(rate):
                out_ref[p, rho * rows:(rho + 1) * rows, :] = (
                    perm_ref[slot, pl.ds(rho, rows, stride=rate), :].astype(_BF))

    def dilated(g, z):
        q2 = z[:, 0:256]
        k2 = z[:, 256:512]
        inv_q, inv_k = _seg_inv([q2, k2], bdr_ref, HEAD_DIM)
        qn2, kn2 = q2 * inv_q, k2 * inv_k
        for pp in range(2):
            p = 2 * g + pp
            sl = slice(pp * LANES, (pp + 1) * LANES)
            qx = qn2[:, sl] * gdq_ref[...]
            qx = qx * tdc_ref[...] + pltpu.roll(qx, 64, 1) * tds_ref[...]
            kx = kn2[:, sl] * gdk_ref[...]
            kx = kx * tdc_ref[...] + pltpu.roll(kx, 64, 1) * tds_ref[...]
            put(qd_ref, p, qx, DIL_RATES[g], 3 * pp)
            put(kd_ref, p, kx, DIL_RATES[g], 3 * pp + 1)
            put(vd_ref, p, z[:, 512 + pp * LANES: 512 + (pp + 1) * LANES], DIL_RATES[g], 3 * pp + 2)

    dil_lo = [COL_DIL + g * 768 for g in range(DIL_GROUPS)]
    stages = [(COL_MLA, COL_NA, latent),
              (dil_lo[2], dil_lo[2] + 768, functools.partial(dilated, 2)),
              (dil_lo[1], dil_lo[1] + 768, functools.partial(dilated, 1)),
              (dil_lo[0], dil_lo[0] + 768, functools.partial(dilated, 0)),
              (COL_NA, COL_DIL, neighbourhood)]
    z_next = project(stages[0][0], stages[0][1])
    for idx, (_, _, finish) in enumerate(stages):
        z = z_next
        if idx + 1 < len(stages):
            z_next = project(stages[idx + 1][0], stages[idx + 1][1])
        finish(z)


def _const_spec(shape):
    nd = len(shape)
    return pl.BlockSpec(shape, lambda *_, _nd=nd: (0,) * _nd)


def _layer_spec(arr, layer):
    nd = arr.ndim
    return pl.BlockSpec((None,) + arr.shape[1:], lambda *_, _l=layer, _nd=nd: (_l,) + (0,) * (_nd - 1))


def _mix_in(x, lw, layer, tabs, seq):
    t = x.shape[0]
    nblk_seq = seq // TM
    tab_spec = pl.BlockSpec((TM, LANES), lambda i: (i % nblk_seq, 0))
    weights = [lw[n] for n in ('g_mix', 'w_in', 'g_cq', 'g_ckv', 'w_uq', 'w_uk', 'w_uv',
                               'g_mq', 'g_mk', 'g_kr', 'g_nq', 'g_nk', 'g_dq', 'g_dk')]
    joins = _head_join_matrices()
    tables = [tabs['q_c'], tabs['q_s1'], tabs['q_s2'], tabs['k_c'], tabs['k_s'], tabs['d_c'], tabs['d_s']]

    def out(n):
        return (jax.ShapeDtypeStruct((n, t, LANES), _BF), pl.BlockSpec((n, TM, LANES), lambda i: (0, i, 0)))

    vt_out = (jax.ShapeDtypeStruct((MLA_HEADS, t // TM, LANES, TM), _BF),
              pl.BlockSpec((MLA_HEADS, 1, LANES, TM), lambda i: (0, i, 0, 0)))
    outs = [out(6), out(6), vt_out, out(3), out(3), out(3), out(6), out(6), out(6)]
    return pl.pallas_call(
        _mix_in_kernel,
        grid=(t // TM,),
        in_specs=[pl.BlockSpec((TM, D_MODEL), lambda i: (i, 0))]
        + [_layer_spec(w, layer) for w in weights] + [_const_spec(m.shape) for m in joins]
        + [tab_spec] * len(tables),
        out_specs=[o[1] for o in outs],
        out_shape=[o[0] for o in outs],
        scratch_shapes=[pltpu.VMEM((6, TM, LANES), _F32)],
        compiler_params=pltpu.CompilerParams(dimension_semantics=("parallel",), vmem_limit_bytes=VMEM_LIMIT),
        name="mix_in",
    )(x, *weights, *joins, *tables)


def _mla_kernel(q_ref, k_ref, vt_ref, o_ref, sa_sc, sb_sc, m_sc, acc_sc, *, tk):
    tq = q_ref.shape[1]
    nk = k_ref.shape[1] // tk
    sub = tk // TM
    m_sc[...] = jnp.full(m_sc.shape, NEG, _F32)
    acc_sc[...] = jnp.zeros(acc_sc.shape, _F32)

    def scores(j, dst):
        rows = pl.ds(pl.multiple_of(j * tk, tk), tk)
        for hh in range(2):
            dst[hh] = _dot_t(k_ref[hh, rows, :], q_ref[hh])

    def consume(j, src):
        for hh in range(2):
            s = src[hh]
            m_prev = m_sc[hh]
            m_new = jnp.maximum(m_prev, jnp.max(s, axis=0, keepdims=True))
            p = jnp.exp2(s - m_new).astype(_BF)
            vt = jnp.concatenate([vt_ref[hh, j * sub + c] for c in range(sub)], axis=1)
            acc_sc[hh] = jnp.exp2(m_prev - m_new) * acc_sc[hh] + _dot(vt, p)
            m_sc[hh] = m_new

    def pair(i, carry):
        j = 2 * i
        scores(j + 1, sb_sc)
        consume(j, sa_sc)
        scores(j + 2, sa_sc)
        consume(j + 1, sb_sc)
        return carry

    scores(0, sa_sc)
    lax.fori_loop(0, nk // 2 - 1, pair, 0)
    scores(nk - 1, sb_sc)
    consume(nk - 2, sa_sc)
    consume(nk - 1, sb_sc)

    lane = _lane((tq, LANES))
    outs = []
    for hh in range(2):
        acc = acc_sc[hh]
        outs.append((acc / acc[MLA_V:MLA_V + 1, :]).T)
    o_ref[0] = jnp.where(lane < MLA_V, outs[0], pltpu.roll(outs[1], MLA_V, 1)).astype(o_ref.dtype)


def _mla(q, k, vt, batch, seq, tq=2048, tk=512):
    t = q.shape[1]
    nq = seq // tq
    nblk = seq // TM
    assert seq % (2 * tk) == 0 and tk % TM == 0
    return pl.pallas_call(
        functools.partial(_mla_kernel, tk=tk),
        grid=(batch, MLA_HEADS // 2, nq),
        in_specs=[pl.BlockSpec((2, tq, LANES), lambda b, p, i: (p, b * nq + i, 0)),
                  pl.BlockSpec((2, seq, LANES), lambda b, p, i: (p, b, 0)),
                  pl.BlockSpec((2, nblk, LANES, TM), lambda b, p, i: (p, b, 0, 0))],
        out_specs=pl.BlockSpec((1, tq, LANES), lambda b, p, i: (p, b * nq + i, 0)),
        out_shape=jax.ShapeDtypeStruct((MLA_HEADS // 2, t, LANES), _BF),
        scratch_shapes=[pltpu.VMEM((2, tk, tq), _F32), pltpu.VMEM((2, tk, tq), _F32),
                        pltpu.VMEM((2, 1, tq), _F32), pltpu.VMEM((2, LANES, tq), _F32)],
        compiler_params=pltpu.CompilerParams(
            dimension_semantics=("parallel", "parallel", "arbitrary"), vmem_limit_bytes=VMEM_LIMIT),
        name="mla_attention",
    )(q, k, vt)


def _na_kernel(q_ref, k_ref, v_ref, bias_ref, o_ref, *, rows):
    i = pl.program_id(2)
    lane = _lane((GRID_W, LANES))
    first = lane < HEAD_DIM
    nkeys = NA_WIN_R * GRID_W

    def score(rr):
        r = i * NA_RB + rr
        rs = jnp.clip(r - NA_WIN_R // 2, 0, rows - NA_WIN_R)
        start = pl.multiple_of(rs * GRID_W, GRID_W)
        q = q_ref[0, rr * GRID_W:(rr + 1) * GRID_W, :]
        zero = jnp.zeros_like(q)
        q2 = jnp.concatenate([jnp.where(first, q, zero), jnp.where(first, zero, q)], axis=0)
        return _dot_t(q2, k_ref[0, pl.ds(start, nkeys), :]) + bias_ref[0, r - rs], start

    def softmax(s):
        p = jnp.exp(s - jnp.max(s, axis=1, keepdims=True))
        return p.astype(_BF), jnp.sum(p, axis=1, keepdims=True)

    def value(rr, p, l, start):
        o2 = _dot(p, v_ref[0, pl.ds(start, nkeys), :]) / l
        o = jnp.where(first, o2[:GRID_W], o2[GRID_W:])
        o_ref[0, rr * GRID_W:(rr + 1) * GRID_W, :] = o.astype(o_ref.dtype)

    scored, normed = {}, {}
    for step in range(NA_RB + 2):
        if step < NA_RB:
            scored[step] = score(step)
        if 1 <= step <= NA_RB:
            s, start = scored.pop(step - 1)
            normed[step - 1] = softmax(s) + (start,)
        if step >= 2:
            value(step - 2, *normed.pop(step - 2))


def _na(q, k, v, bias, layer, batch, seq):
    t = q.shape[1]
    rows = seq // GRID_W
    assert rows >= NA_WIN_R and rows % NA_RB == 0
    nrb = rows // NA_RB
    qrows = NA_RB * GRID_W
    return pl.pallas_call(
        functools.partial(_na_kernel, rows=rows),
        grid=(batch, NA_HEADS // 2, nrb),
        in_specs=[pl.BlockSpec((1, qrows, LANES), lambda b, p, i: (p, b * nrb + i, 0)),
                  pl.BlockSpec((1, seq, LANES), lambda b, p, i: (p, b, 0)),
                  pl.BlockSpec((1, seq, LANES), lambda b, p, i: (p, b, 0)),
                  pl.BlockSpec((None, 1, NA_WIN_R, 2 * GRID_W, NA_WIN_R * GRID_W),
                               lambda b, p, i: (layer, p, 0, 0, 0))],
        out_specs=pl.BlockSpec((1, qrows, LANES), lambda b, p, i: (p, b * nrb + i, 0)),
        out_shape=jax.ShapeDtypeStruct((NA_HEADS // 2, t, LANES), _BF),
        compiler_params=pltpu.CompilerParams(
            dimension_semantics=("parallel", "parallel", "arbitrary"), vmem_limit_bytes=VMEM_LIMIT),
        name="na_attention",
    )(q, k, v, bias)


def _dil_kernel(q0_ref, q1_ref, q2_ref, k0_ref, k1_ref, k2_ref, v0_ref, v1_ref, v2_ref, band_ref,
                o_ref, og_sc, lse_sc, *, seq):
    sb = pl.program_id(2)
    q_refs = (q0_ref, q1_ref, q2_ref)
    k_refs = (k0_ref, k1_ref, k2_ref)
    v_refs = (v0_ref, v1_ref, v2_ref)
    lane = _lane((DIL_TQ, LANES))
    first_qk = (lane % HEAD_DIM) < (HEAD_DIM // 2)
    first_v = lane < HEAD_DIM
    n_tiles = DIL_SB // DIL_TQ
    half = DIL_TQ // 2

    for g in range(DIL_GROUPS):
        rate = DIL_RATES[g]
        length = seq // rate
        per_blk = TM // rate
        chunk = min(half, per_blk)
        n_chunks = (2 * DIL_TQ) // chunk
        tiles_per_res = n_tiles // rate
        q_ref, k_ref, v_ref = q_refs[g], k_refs[g], v_refs[g]

        def row_of(m, rho, _per_blk=per_blk):
            return (m // _per_blk) * TM + rho * _per_blk + (m % _per_blk)

        def body(tg, carry, _g=g, _rate=rate, _length=length, _per_blk=per_blk, _chunk=chunk,
                 _n_chunks=n_chunks, _tpr=tiles_per_res, _q=q_ref, _k=k_ref, _v=v_ref):
            def score(u):
                tt = tg * DIL_UNROLL + u
                rho = tt // _tpr
                mt = tt % _tpr
                m0 = sb * (DIL_SB // _rate) + mt * DIL_TQ
                q = jnp.concatenate(
                    [_q[0, pl.ds(pl.multiple_of(row_of(m0 + c * _chunk, rho), _chunk), _chunk), :]
                     for c in range(DIL_TQ // _chunk)], axis=0)
                zero = jnp.zeros_like(q)
                q2 = jnp.concatenate([jnp.where(first_qk, q, zero), jnp.where(first_qk, zero, q)], axis=0)
                win = []
                for c in range(_n_chunks):
                    ms = jnp.clip(m0 - half + c * _chunk, 0, _length - _chunk)
                    win.append(pl.multiple_of(row_of(ms, rho), _chunk))
                kw = jnp.concatenate([_k[0, pl.ds(row, _chunk), :] for row in win], axis=0)
                edge = (m0 == 0).astype(jnp.int32) + 2 * (m0 == _length - DIL_TQ).astype(jnp.int32)
                return _dot_t(q2, kw) + band_ref[edge], rho + _rate * mt * DIL_TQ, win

            def softmax(s):
                m = jnp.max(s, axis=1, keepdims=True)
                p = jnp.exp(s - m)
                return p.astype(_BF), m, jnp.sum(p, axis=1, keepdims=True)

            def value(p, m, l, base, win):
                vw = jnp.concatenate([_v[0, pl.ds(row, _chunk), :] for row in win], axis=0)
                o2 = _dot(p, vw) / l
                lse2 = jnp.broadcast_to(m + jnp.log(l), (2 * DIL_TQ, LANES))
                if _rate == 1:
                    dst = pl.ds(pl.multiple_of(base, DIL_TQ), DIL_TQ)
                else:
                    dst = pl.ds(base, DIL_TQ, stride=_rate)
                og_sc[_g, dst, :] = jnp.where(first_v, o2[:DIL_TQ], o2[DIL_TQ:])
                lse_sc[_g, dst, :] = jnp.where(first_v, lse2[:DIL_TQ], lse2[DIL_TQ:])

            scored, normed = {}, {}
            for step in range(DIL_UNROLL + 2):
                if step < DIL_UNROLL:
                    scored[step] = score(step)
                if 1 <= step <= DIL_UNROLL:
                    s, base, win = scored.pop(step - 1)
                    normed[step - 1] = softmax(s) + (base, win)
                if step >= 2:
                    value(*normed.pop(step - 2))
            return carry

        lax.fori_loop(0, n_tiles // DIL_UNROLL, body, 0)

    m = jnp.maximum(jnp.maximum(lse_sc[0], lse_sc[1]), lse_sc[2])
    w0 = jnp.exp(lse_sc[0] - m)
    w1 = jnp.exp(lse_sc[1] - m)
    w2 = jnp.exp(lse_sc[2] - m)
    o = (w0 * og_sc[0] + w1 * og_sc[1] + w2 * og_sc[2]) / (w0 + w1 + w2)
    o_ref[0] = o.astype(o_ref.dtype)


def _dil(q, k, v, band, batch, seq):
    t = q.shape[1]
    assert seq % DIL_SB == 0
    nsb = seq // DIL_SB

    def q_spec(g):
        return pl.BlockSpec((1, DIL_SB, LANES), lambda b, sp, s, _g=g: (2 * _g + sp, b * nsb + s, 0))

    def kv_spec(g):
        return pl.BlockSpec((1, seq, LANES), lambda b, sp, s, _g=g: (2 * _g + sp, b, 0))

    return pl.pallas_call(
        functools.partial(_dil_kernel, seq=seq),
        grid=(batch, 2, nsb),
        in_specs=[kv_spec(0), kv_spec(1), kv_spec(2), kv_spec(0), kv_spec(1), kv_spec(2),
                  kv_spec(0), kv_spec(1), kv_spec(2),
                  pl.BlockSpec((4, 2 * DIL_TQ, 2 * DIL_TQ), lambda b, sp, s: (0, 0, 0))],
        out_specs=pl.BlockSpec((1, DIL_SB, LANES), lambda b, sp, s: (sp, b * nsb + s, 0)),
        out_shape=jax.ShapeDtypeStruct((2, t, LANES), _BF),
        scratch_shapes=[pltpu.VMEM((DIL_GROUPS, DIL_SB, LANES), _F32),
                        pltpu.VMEM((DIL_GROUPS, DIL_SB, LANES), _F32)],
        compiler_params=pltpu.CompilerParams(
            dimension_semantics=("parallel", "parallel", "arbitrary"), vmem_limit_bytes=VMEM_LIMIT),
        name="dil_attention",
    )(q, q, q, k, k, k, v, v, v, band)


def _mem_kv_kernel(mem_ref, gmem_ref, wckv_ref, gxk_ref, k_ref, v_ref):
    hm = _rms(mem_ref[0], gmem_ref[...]).astype(_BF)
    kv = _dot(hm, wckv_ref[...])
    for hd in range(X_HEADS):
        kh = kv[:, hd * X_HEAD_DIM:(hd + 1) * X_HEAD_DIM]
        k_ref[0, :, hd * X_HEAD_DIM:(hd + 1) * X_HEAD_DIM] = _rms(kh, gxk_ref[...]).astype(_BF)
    v_ref[0] = kv[:, D_MODEL:].astype(_BF)


def _mem_kv(mem, lw):
    b, m, _ = mem.shape
    weights = [lw[n] for n in ('g_mem', 'w_ckv', 'g_xk')]
    nl = weights[0].shape[0]

    def per_layer(arr):
        nd = arr.ndim
        return pl.BlockSpec((None,) + arr.shape[1:], lambda l, i, _nd=nd: (l,) + (0,) * (_nd - 1))

    out_spec = pl.BlockSpec((None, 1, m, D_MODEL), lambda l, i: (l, i, 0, 0))
    return pl.pallas_call(
        _mem_kv_kernel,
        grid=(nl, b),
        in_specs=[pl.BlockSpec((1, m, D_MODEL), lambda l, i: (i, 0, 0))] + [per_layer(w) for w in weights],
        out_specs=[out_spec] * 2,
        out_shape=[jax.ShapeDtypeStruct((nl, b, m, D_MODEL), _BF)] * 2,
        compiler_params=pltpu.CompilerParams(
            dimension_semantics=("parallel", "parallel"), vmem_limit_bytes=VMEM_LIMIT),
        name="mem_kv",
    )(mem, *weights)


def _mix_out_kernel(x_ref, oa_ref, ob_ref, oc_ref, wo_ref, gcross_ref, wcq_ref, gxq_ref,
                    mk_ref, mv_ref, wco_ref, y_ref):
    mix = jnp.concatenate([oa_ref[p] for p in range(3)] + [ob_ref[p] for p in range(3)]
                          + [oc_ref[p] for p in range(2)], axis=1)
    x1 = x_ref[...] + _dot(mix, wo_ref[...])
    inv_x = lax.rsqrt(jnp.mean(x1 * x1, axis=-1, keepdims=True) + EPS)
    q = _dot((x1 * gcross_ref[...]).astype(_BF), wcq_ref[...]) * inv_x
    heads = [slice(hd * X_HEAD_DIM, (hd + 1) * X_HEAD_DIM) for hd in range(X_HEADS)]
    scores = [_dot_t((_rms(q[:, sl], gxq_ref[...]) * X_HEAD_DIM ** -0.5).astype(_BF), mk_ref[0, :, sl])
              for sl in heads]
    probs = []
    for s in scores:
        p = jnp.exp(s - jnp.max(s, axis=1, keepdims=True))
        probs.append((p.astype(_BF), jnp.sum(p, axis=1, keepdims=True)))
    outs = [(_dot(p, mv_ref[0, :, sl]) / l).astype(_BF) for (p, l), sl in zip(probs, heads)]
    o = jnp.concatenate(outs, axis=1)
    y_ref[...] = x1 + _dot(o, wco_ref[...])


def _mix_out(x, oa, ob, oc, mk, mv, lw, layer, seq):
    t = x.shape[0]
    tm = TM_OUT
    nblk_seq = seq // tm
    mem_len = mk.shape[2]

    def grp(n):
        return pl.BlockSpec((n, tm, LANES), lambda i: (0, i, 0))

    mem_spec = pl.BlockSpec((None, 1, mem_len, D_MODEL), lambda i: (layer, i // nblk_seq, 0, 0))
    return pl.pallas_call(
        _mix_out_kernel,
        grid=(t // tm,),
        in_specs=[pl.BlockSpec((tm, D_MODEL), lambda i: (i, 0)), grp(3), grp(3), grp(2),
                  _layer_spec(lw['w_o'], layer), _layer_spec(lw['g_cross'], layer),
                  _layer_spec(lw['w_cq'], layer), _layer_spec(lw['g_xq'], layer),
                  mem_spec, mem_spec, _layer_spec(lw['w_co'], layer)],
        out_specs=pl.BlockSpec((tm, D_MODEL), lambda i: (i, 0)),
        out_shape=jax.ShapeDtypeStruct((t, D_MODEL), _F32),
        compiler_params=pltpu.CompilerParams(dimension_semantics=("parallel",), vmem_limit_bytes=VMEM_LIMIT),
        name="mix_out",
    )(x, oa, ob, oc, lw['w_o'], lw['g_cross'], lw['w_cq'], lw['g_xq'], mk, mv, lw['w_co'])


def _ffn_kernel(x_ref, xp_ref, xn_ref, gffn_ref, wup_ref, cw_ref, cb_ref, wdn_ref, y_ref, *, nblk_seq):
    i = pl.program_id(0)
    tm = x_ref.shape[0]
    x = x_ref[...]
    g = gffn_ref[...]
    is_first = (i % nblk_seq) == 0
    is_last = (i % nblk_seq) == nblk_seq - 1
    hp = jnp.where(is_first, 0.0, _rms(xp_ref[...], g))
    hn = jnp.where(is_last, 0.0, _rms(xn_ref[...], g))
    h = jnp.concatenate([hp, _rms(x, g), hn], axis=0).astype(_BF)
    def up_proj(c):
        return [_dot(h, wup_ref[:, part * D_FF + c * FF_CH: part * D_FF + (c + 1) * FF_CH]) for part in range(2)]

    def conv(u, lo):
        cw = cw_ref[:, lo:lo + FF_CH]
        up = pltpu.roll(u, 1, 0)[8:8 + tm]
        un = pltpu.roll(u, tm + 15, 0)[8:8 + tm]
        return up * cw[0:1] + u[8:8 + tm] * cw[1:2] + un * cw[2:3] + cb_ref[:, lo:lo + FF_CH]

    acts = []
    u_next = up_proj(0)
    for c in range(N_FF_CH):
        u_val, u_gate = u_next
        if c + 1 < N_FF_CH:
            u_next = up_proj(c + 1)
        a = conv(u_val, c * FF_CH)
        half_gate = conv(u_gate, D_FF + c * FF_CH)
        acts.append((half_gate * (1.0 + jnp.tanh(half_gate)) * a).astype(_BF))
    y_ref[...] = x + _dot(jnp.concatenate(acts, axis=1), wdn_ref[...])


def _ffn(x, lw, layer, seq):
    t = x.shape[0]
    nblk_seq = seq // TM
    nhalo = t // 8
    per = TM // 8
    return pl.pallas_call(
        functools.partial(_ffn_kernel, nblk_seq=nblk_seq),
        grid=(t // TM,),
        in_specs=[pl.BlockSpec((TM, D_MODEL), lambda i: (i, 0)),
                  pl.BlockSpec((8, D_MODEL), lambda i: (jnp.maximum(i * per - 1, 0), 0)),
                  pl.BlockSpec((8, D_MODEL), lambda i: (jnp.minimum((i + 1) * per, nhalo - 1), 0))]
        + [_layer_spec(lw[n], layer) for n in ('g_ffn', 'w_up', 'conv_w', 'conv_b', 'w_down')],
        out_specs=pl.BlockSpec((TM, D_MODEL), lambda i: (i, 0)),
        out_shape=jax.ShapeDtypeStruct((t, D_MODEL), _F32),
        compiler_params=pltpu.CompilerParams(dimension_semantics=("parallel",), vmem_limit_bytes=VMEM_LIMIT),
        name="conv_ffn",
    )(x, x, x, lw['g_ffn'], lw['w_up'], lw['conv_w'], lw['conv_b'], lw['w_down'])


def _relayout_w_in(w):
    nl = w.shape[0]
    lead = w[..., :SPLIT_CKV]
    a = w[..., SPLIT_CKV:SPLIT_CKV + 16]
    b = w[..., SPLIT_CKV + 16:SPLIT_KR]
    z32 = jnp.zeros(a.shape[:-1] + (32,), w.dtype)
    kr = jnp.concatenate([b, a, z32, a, b, z32], axis=-1)
    na = w[..., SPLIT_KR:SPLIT_NA].reshape(nl, D_MODEL, 3, NA_HEADS // 2, LANES)
    na_ab = na[:, :, :, 0:2].reshape(nl, D_MODEL, 3 * 2 * LANES)
    na_c = na[:, :, :, 2].reshape(nl, D_MODEL, 3 * LANES)
    dl = w[..., SPLIT_NA:].reshape(nl, D_MODEL, 3, DIL_GROUPS, 2, 2, 2, HEAD_DIM // 2)
    qk = dl[:, :, 0:2].transpose(0, 1, 3, 2, 4, 6, 5, 7).reshape(nl, D_MODEL, DIL_GROUPS, 4 * LANES)
    v = dl[:, :, 2].reshape(nl, D_MODEL, DIL_GROUPS, 2 * LANES)
    dil = jnp.concatenate([qk, v], axis=-1).reshape(nl, D_MODEL, DIL_GROUPS * 6 * LANES)
    out = jnp.concatenate([lead, kr, na_ab, na_c, dil], axis=-1)
    assert out.shape[-1] == D_IN_P
    return out


def _pad_heads(w, real, n_heads):
    nl, k, _ = w.shape
    w = w.reshape(nl, k, n_heads, real)
    return jnp.pad(w, ((0, 0), (0, 0), (0, 0), (0, LANES - real))).reshape(nl, k, n_heads * LANES)


def _rows(v):
    return v[:, None, :].astype(_F32)


def _prep_layers(p):
    nl = p['w_in'].shape[0]
    kn, qn = p['mla_kn'], p['mla_qn']
    z32 = jnp.zeros((nl, 32), _F32)
    z64 = jnp.zeros((nl, 64), _F32)
    dq, dk = p['dil_qn'], p['dil_kn']

    def rot(gain):
        return jnp.concatenate([gain[:, :32], gain[:, :32], gain[:, 32:], gain[:, 32:]], axis=-1)

    gate_half = jnp.concatenate([jnp.ones((D_FF,), _F32), jnp.full((D_FF,), 0.5, _F32)])

    return {
        'g_mix': _rows(p['norm_mix']), 'w_in': _relayout_w_in(p['w_in'].astype(_BF)),
        'g_cq': _rows(p['mla_q_norm']), 'g_ckv': _rows(p['mla_kv_norm']),
        'w_uq': _pad_heads(p['w_uq'].astype(_BF), MLA_QK, MLA_HEADS),
        'w_uk': _pad_heads(p['w_uk'].astype(_BF), MLA_NOPE, MLA_HEADS),
        'w_uv': _pad_heads(p['w_uv'].astype(_BF), MLA_V, MLA_HEADS),
        'g_mq': _rows(jnp.concatenate([qn, z32], axis=-1) * (MLA_QK ** -0.5 * LOG2E)),
        'g_mk': _rows(jnp.concatenate([kn[:, :MLA_NOPE], z64], axis=-1)),
        'g_kr': _rows(jnp.concatenate([kn[:, 80:96], kn[:, 64:80], z32, kn[:, 64:80], kn[:, 80:96], z32], axis=-1)),
        'g_nq': _rows(jnp.tile(p['na_qn'], (1, 2)) * HEAD_DIM ** -0.5), 'g_nk': _rows(jnp.tile(p['na_kn'], (1, 2))),
        'g_dq': _rows(rot(dq) * HEAD_DIM ** -0.5), 'g_dk': _rows(rot(dk)),
        'w_o': p['w_o'].astype(_BF),
        'g_cross': _rows(p['norm_cross']), 'w_cq': p['w_cq'].astype(_BF),
        'g_xq': _rows(p['x_qn']), 'g_xk': _rows(p['x_kn']), 'g_mem': _rows(p['norm_mem']),
        'w_ckv': p['w_ckv'].astype(_BF), 'w_co': p['w_co'].astype(_BF),
        'g_ffn': _rows(p['norm_ffn']), 'w_up': p['w_up'].astype(_BF),
        'conv_w': p['conv_w'] * gate_half, 'conv_b': _rows(p['conv_b'] * gate_half),
        'w_down': p['w_down'].astype(_BF),
        'na_bias': _na_bias(p['na_rpb']),
    }


def _na_bias(rpb):
    nl = rpb.shape[0]
    dl = np.arange(NA_WIN_R)
    j = np.arange(NA_WIN_R)
    r_off = j[None, :] - dl[:, None] + (NA_WIN_R - 1)
    qc = np.arange(GRID_W)
    kc = np.arange(GRID_W)
    c_start = np.clip(qc - NA_WIN_C // 2, 0, GRID_W - NA_WIN_C)
    valid = (kc[None, :] >= c_start[:, None]) & (kc[None, :] < c_start[:, None] + NA_WIN_C)
    c_off = kc[None, :] - qc[:, None] + (NA_WIN_C - 1)
    pick = (np.arange(2 * NA_WIN_C - 1)[:, None, None] == c_off[None]) & valid[None]
    rows = rpb[:, :, r_off].reshape(nl, NA_HEADS // 2, 2, NA_WIN_R, NA_WIN_R, 2 * NA_WIN_C - 1)
    b = jnp.einsum('lpedjc,cqk->lpdeqjk', rows, jnp.asarray(pick, _F32), precision=lax.Precision.HIGHEST)
    b = b + jnp.asarray(np.where(valid, 0.0, NEG), _F32)[:, None, :]
    return b.reshape(nl, NA_HEADS // 2, NA_WIN_R, 2 * GRID_W, NA_WIN_R * GRID_W)


def _band_bias():
    i = np.arange(DIL_TQ)[:, None]
    c = np.arange(2 * DIL_TQ)[None, :]
    band = (c >= i) & (c <= i + DIL_TQ)
    out = []
    for edge in range(4):
        ok = band
        if edge & 1:
            ok = ok & (c >= DIL_TQ // 2)
        if edge & 2:
            ok = ok & (c < 2 * DIL_TQ - DIL_TQ // 2)
        out.append(np.tile(np.where(ok, 0.0, NEG), (2, 1)))
    return jnp.asarray(np.stack(out), _F32)


def _rope_tables(seq):
    pos = jnp.arange(seq, dtype=_F32)[:, None]

    def cs(half):
        inv = ROPE_THETA ** (-jnp.arange(half, dtype=_F32) / half)
        ang = pos * inv[None, :]
        return jnp.cos(ang), jnp.sin(ang)

    c16, s16 = cs(MLA_ROPE // 2)
    c32, s32 = cs(HEAD_DIM // 2)
    one = jnp.ones((seq, 64), _F32)
    z16 = jnp.zeros((seq, 16), _F32)
    z32 = jnp.zeros((seq, 32), _F32)
    z64 = jnp.zeros((seq, 64), _F32)
    return {
        'q_c': jnp.concatenate([one, c16, c16, z32], axis=1),
        'q_s1': jnp.concatenate([z64, z16, s16, z32], axis=1),
        'q_s2': jnp.concatenate([z64, -s16, z16, z32], axis=1),
        'k_c': jnp.concatenate([z64, c16, c16, z32], axis=1),
        'k_s': jnp.concatenate([z64, -s16, s16, z32], axis=1),
        'd_c': jnp.concatenate([c32, c32, c32, c32], axis=1),
        'd_s': jnp.concatenate([-s32, -s32, s32, s32], axis=1),
    }


def _trunk_layer(x, mk, mv, lw, layer, tabs, band, batch, seq):
    qm, km, vm, qn, kn, vn, qd, kd, vd = _mix_in(x, lw, layer, tabs, seq)
    oa = _mla(qm, km, vm, batch, seq)
    ob = _na(qn, kn, vn, lw['na_bias'], layer, batch, seq)
    oc = _dil(qd, kd, vd, band, batch, seq)
    x = _mix_out(x, oa, ob, oc, mk, mv, lw, layer, seq)
    return _ffn(x, lw, layer, seq)


def kernel(x_prompt, x_sample, mem_prompt, mem_sample, norm_mix, w_in, mla_q_norm, mla_kv_norm, w_uq, w_uk, w_uv, mla_qn, mla_kn, na_qn, na_kn, na_rpb, dil_qn, dil_kn, w_o, norm_cross, norm_mem, w_cq, w_ckv, x_qn, x_kn, w_co, norm_ffn, w_up, conv_w, conv_b, w_down):
    stacked = {
        'norm_mix': norm_mix, 'w_in': w_in, 'mla_q_norm': mla_q_norm, 'mla_kv_norm': mla_kv_norm,
        'w_uq': w_uq, 'w_uk': w_uk, 'w_uv': w_uv, 'mla_qn': mla_qn, 'mla_kn': mla_kn,
        'na_qn': na_qn, 'na_kn': na_kn, 'na_rpb': na_rpb, 'dil_qn': dil_qn, 'dil_kn': dil_kn,
        'w_o': w_o, 'norm_cross': norm_cross, 'norm_mem': norm_mem, 'w_cq': w_cq, 'w_ckv': w_ckv,
        'x_qn': x_qn, 'x_kn': x_kn, 'w_co': w_co, 'norm_ffn': norm_ffn, 'w_up': w_up,
        'conv_w': conv_w, 'conv_b': conv_b, 'w_down': w_down,
    }
    lw = _prep_layers(stacked)
    band = _band_bias()
    outs = []
    for x, mem in ((x_prompt, mem_prompt), (x_sample, mem_sample)):
        batch, seq, _ = x.shape
        tabs = _rope_tables(seq)
        mk, mv = _mem_kv(mem, lw)
        xt = x.reshape(batch * seq, D_MODEL)
        for layer in range(DEPTH):
            xt = _trunk_layer(xt, mk, mv, lw, layer, tabs, band, batch, seq)
        outs.append(xt.reshape(batch, seq, D_MODEL))
    return tuple(outs)
```
